```python
import math
import jax
import jax.numpy as jnp
from jax import lax

D_MODEL = 2048
BATCH = 4
SEQ = 2048
DEPTH = 4

GRID_W = 64
CTX_LEN = 256
N_EVEN = (DEPTH + 1) // 2
N_ODD = DEPTH // 2
EPS = 1e-6
N_DIR = 2

HEAD_DIM = 128
A_Q_HEADS = 12
A_KV_HEADS = 4
A_GROUP = A_Q_HEADS // A_KV_HEADS
A_WIDTH = A_Q_HEADS * HEAD_DIM
KV_WIDTH = A_KV_HEADS * HEAD_DIM
ROPE_THETA = 10000.0
ROPE_AXIS_DIM = HEAD_DIM // 2
Q_BLOCK = 128

B_GROUPS = 4
B_GROUP_DIM = 128
B_WIDTH = B_GROUPS * B_GROUP_DIM
CHUNK = 128

EVEN_IN = 2 * KV_WIDTH + A_WIDTH + 2 * B_WIDTH + A_WIDTH + B_WIDTH
EVEN_MIX = A_WIDTH + B_WIDTH
EVEN_SPLITS = (KV_WIDTH, 2 * KV_WIDTH, 2 * KV_WIDTH + A_WIDTH, 2 * KV_WIDTH + A_WIDTH + B_WIDTH, 2 * KV_WIDTH + A_WIDTH + 2 * B_WIDTH, 2 * KV_WIDTH + 2 * A_WIDTH + 2 * B_WIDTH)

C_WIDTH = 1024
C_GROUP_DIM = 16
C_GROUPS = C_WIDTH // C_GROUP_DIM
C_STATE = 64

D_WIDTH = 1024
D_BLOCKS = 8
D_BLOCK_DIM = D_WIDTH // D_BLOCKS
CONV_W = 4
CONV_PAD_LO = (CONV_W - 1) // 2
LRU_C = 8.0

ODD_IN = 2 * (C_WIDTH + D_WIDTH)
ODD_MIX = C_WIDTH + D_WIDTH
ODD_SPLITS = (C_WIDTH, C_WIDTH + D_WIDTH, 2 * C_WIDTH + D_WIDTH)

kernel_name = 'hybrid_diffusion_attn_sgu_s5_rglru'


def rms_norm(x, g):
    xf = x.astype(jnp.float32)
    y = xf * lax.rsqrt(jnp.mean(xf * xf, axis=-1, keepdims=True) + EPS)
    return (y * g.astype(jnp.float32)).astype(x.dtype)


def modulate(x, g, shift, scale):
    return rms_norm(x, g) * (1 + scale) + shift


def adaln(cond, w, b):
    m = jax.nn.silu(cond) @ w + b
    return jnp.split(m, 3, axis=-1)


def flip(t):
    return jnp.flip(t, axis=1)


def axial_rope_tables(rows):
    row = jnp.repeat(jnp.arange(rows), GRID_W).astype(jnp.float32)
    col = jnp.tile(jnp.arange(GRID_W), rows).astype(jnp.float32)
    inv = ROPE_THETA ** (-jnp.arange(0, ROPE_AXIS_DIM, 2, dtype=jnp.float32) / ROPE_AXIS_DIM)
    ang = jnp.concatenate([row[:, None] * inv, col[:, None] * inv], axis=-1)
    return jnp.cos(ang), jnp.sin(ang)


def apply_rope(x, cos, sin):
    xf = x.astype(jnp.float32).reshape(x.shape[:-1] + (HEAD_DIM // 2, 2))
    x0, x1 = xf[..., 0], xf[..., 1]
    cs, sn = cos[None, :, None, :], sin[None, :, None, :]
    out = jnp.stack([x0 * cs - x1 * sn, x0 * sn + x1 * cs], axis=-1)
    return out.reshape(x.shape).astype(x.dtype)


def heads(t, n):
    return t.reshape(t.shape[0], t.shape[1], n, HEAD_DIM)


def gqa_attend(q, k, v):
    s = jnp.einsum('bqkgd,bskd->bkgqs', q, k).astype(jnp.float32) * (HEAD_DIM ** -0.5)
    p = jax.nn.softmax(s, axis=-1).astype(v.dtype)
    return jnp.einsum('bkgqs,bskd->bqkgd', p, v)


def latent_attention(q_l, k_all, v_all):
    b, l = q_l.shape[:2]
    nb = l // Q_BLOCK
    qb = q_l.reshape(b, nb, Q_BLOCK, A_KV_HEADS, A_GROUP, HEAD_DIM).transpose(1, 0, 2, 3, 4, 5)
    out = lax.map(lambda qi: gqa_attend(qi, k_all, v_all), qb)
    return out.transpose(1, 0, 2, 3, 4, 5).reshape(b, l, A_WIDTH)


def chunk_mlp(u, v, norm_g, w_s, b_s):
    b, l = u.shape[:2]
    vn = rms_norm(v.reshape(b, l // CHUNK, CHUNK, B_GROUPS, B_GROUP_DIM), norm_g.reshape(B_GROUPS, B_GROUP_DIM))
    mixed = jnp.einsum('gpq,bnqgc->bnpgc', w_s, vn) + b_s.T[:, :, None]
    return u * mixed.reshape(b, l, B_WIDTH)


def even_mixer(h, hc, w_in, w_out, q_g, k_g, sgu_g, w_s, b_s, cos, sin, ctx_out):
    b, l = h.shape[:2]
    lc = hc.shape[1]
    k_l, v_l, q_l, bu_l, bv_l, ga_l, gb_l = jnp.split(h @ w_in, EVEN_SPLITS, axis=-1)
    q_l = apply_rope(rms_norm(heads(q_l, A_Q_HEADS), q_g), cos, sin)
    k_l = apply_rope(rms_norm(heads(k_l, A_KV_HEADS), k_g), cos, sin)
    v_l = heads(v_l, A_KV_HEADS)
    if ctx_out:
        k_c, v_c, q_c, bu_c, bv_c, ga_c, gb_c = jnp.split(hc @ w_in, EVEN_SPLITS, axis=-1)
    else:
        k_c, v_c = jnp.split(hc @ w_in[:, :2 * KV_WIDTH], 2, axis=-1)
    k_c = rms_norm(heads(k_c, A_KV_HEADS), k_g)
    v_c = heads(v_c, A_KV_HEADS)
    k_all = jnp.concatenate([k_c, k_l], axis=1)
    v_all = jnp.concatenate([v_c, v_l], axis=1)
    attn_l = latent_attention(q_l, k_all, v_all)
    sgu_l = chunk_mlp(bu_l, bv_l, sgu_g, w_s, b_s)
    mix_l = jnp.concatenate([attn_l * jax.nn.silu(ga_l), sgu_l * jax.nn.silu(gb_l)], axis=-1)
    out_l = mix_l @ w_out
    if not ctx_out:
        return out_l, None
    q_c = rms_norm(heads(q_c, A_Q_HEADS), q_g).reshape(b, lc, A_KV_HEADS, A_GROUP, HEAD_DIM)
    attn_c = gqa_attend(q_c, k_c, v_c).reshape(b, lc, A_WIDTH)
    sgu_c = chunk_mlp(bu_c, bv_c, sgu_g, w_s, b_s)
    mix_c = jnp.concatenate([attn_c * jax.nn.silu(ga_c), sgu_c * jax.nn.silu(gb_c)], axis=-1)
    return out_l, mix_c @ w_out


def linear_scan(a, b, h0=None):
    if h0 is not None:
        b = b.at[:, 0].add(a[:, 0] * h0)

    def combine(e1, e2):
        a1, b1 = e1
        a2, b2 = e2
        return a1 * a2, a2 * b1 + b2

    return lax.associative_scan(combine, (a, b), axis=1)[1]


def s5_direction(u_c, u_l, lam_re, lam_im, log_dt, b_re, b_im, c_re, c_im, ctx_out):
    lam = lax.complex(lam_re.astype(jnp.float32), lam_im.astype(jnp.float32))
    dt = jnp.exp(log_dt.astype(jnp.float32))[:, None]
    a_bar = jnp.exp(lam * dt)
    b_bar = ((a_bar - 1) / lam)[:, :, None] * lax.complex(b_re.astype(jnp.float32), b_im.astype(jnp.float32))
    c_mat = lax.complex(c_re.astype(jnp.float32), c_im.astype(jnp.float32))

    def drive(u):
        bu = jnp.einsum('blgp,gnp->blgn', u.astype(jnp.float32).astype(jnp.complex64), b_bar)
        return jnp.broadcast_to(a_bar, bu.shape), bu

    def readout(hs):
        return jnp.einsum('blgn,gpn->blgp', hs, c_mat).real

    a_c, bu_c = drive(u_c)
    h_c = linear_scan(a_c, bu_c)
    a_l, bu_l = drive(u_l)
    h_l = linear_scan(a_l, bu_l, h_c[:, -1])
    return (readout(h_c) if ctx_out else None), readout(h_l)


def s5_mixer(u_c, u_l, lam_re, lam_im, log_dt, b_re, b_im, c_re, c_im, d_skip, glu_w, glu_b, ctx_out):
    def grp(t):
        return t.reshape(t.shape[0], t.shape[1], C_GROUPS, C_GROUP_DIM)

    yc_f, yl_f = s5_direction(grp(u_c), grp(u_l), lam_re[0], lam_im[0], log_dt[0], b_re[0], b_im[0], c_re[0], c_im[0], ctx_out)
    yc_b, yl_b = s5_direction(grp(flip(u_c)), grp(flip(u_l)), lam_re[1], lam_im[1], log_dt[1], b_re[1], b_im[1], c_re[1], c_im[1], ctx_out)

    def finish(y_f, y_b, u):
        y = (y_f + flip(y_b)).reshape(u.shape).astype(u.dtype) + d_skip * u
        y = jax.nn.gelu(y)
        return y * jax.nn.sigmoid(y @ glu_w + glu_b)

    y_l = finish(yl_f, yl_b, u_l)
    return (finish(yc_f, yc_b, u_c) if ctx_out else None), y_l


def short_conv(x, w, b):
    l = x.shape[1]
    xp = jnp.pad(x, ((0, 0), (CONV_PAD_LO, CONV_W - 1 - CONV_PAD_LO), (0, 0)))
    return sum(xp[:, k:k + l] * w[k] for k in range(CONV_W)) + b


def rglru_direction(x_c, x_l, lam, wa, ba, wx, bx, ctx_out):
    def coeffs(xv):
        b, l = xv.shape[:2]
        xg = xv.reshape(b, l, D_BLOCKS, D_BLOCK_DIM)
        r = jax.nn.sigmoid(jnp.einsum('blhi,hij->blhj', xg, wa).reshape(b, l, D_WIDTH) + ba)
        i = jax.nn.sigmoid(jnp.einsum('blhi,hij->blhj', xg, wx).reshape(b, l, D_WIDTH) + bx)
        log_a = (-LRU_C * jax.nn.softplus(-lam) * r).astype(jnp.float32)
        mult = jnp.sqrt(-jnp.expm1(2 * log_a))
        return jnp.exp(log_a), mult * (i * xv).astype(jnp.float32)

    a_c, b_c = coeffs(x_c)
    h_c = linear_scan(a_c, b_c)
    a_l, b_l = coeffs(x_l)
    h_l = linear_scan(a_l, b_l, h_c[:, -1])
    return (h_c if ctx_out else None), h_l


def rglru_mixer(x_c, x_l, conv_w, conv_b, lam, wa, ba, wx, bx, ctx_out):
    x_c = short_conv(x_c, conv_w, conv_b)
    x_l = short_conv(x_l, conv_w, conv_b)
    hc_f, hl_f = rglru_direction(x_c, x_l, lam[0], wa[0], ba[0], wx[0], bx[0], ctx_out)
    hc_b, hl_b = rglru_direction(flip(x_c), flip(x_l), lam[1], wa[1], ba[1], wx[1], bx[1], ctx_out)
    y_l = (hl_f + flip(hl_b)).astype(x_l.dtype)
    y_c = (hc_f + flip(hc_b)).astype(x_c.dtype) if ctx_out else None
    return y_c, y_l


def odd_mixer(h, hc, w_in, w_out, lam_re, lam_im, log_dt, b_re, b_im, c_re, c_im, d_skip, glu_w, glu_b, conv_w, conv_b, lam, wa, ba, wx, bx, ctx_out):
    u_l, xd_l, gc_l, gd_l = jnp.split(h @ w_in, ODD_SPLITS, axis=-1)
    if ctx_out:
        u_c, xd_c, gc_c, gd_c = jnp.split(hc @ w_in, ODD_SPLITS, axis=-1)
    else:
        u_c, xd_c = jnp.split(hc @ w_in[:, :C_WIDTH + D_WIDTH], (C_WIDTH,), axis=-1)
    s5_c, s5_l = s5_mixer(u_c, u_l, lam_re, lam_im, log_dt, b_re, b_im, c_re, c_im, d_skip, glu_w, glu_b, ctx_out)
    lru_c, lru_l = rglru_mixer(xd_c, xd_l, conv_w, conv_b, lam, wa, ba, wx, bx, ctx_out)
    out_l = jnp.concatenate([s5_l * jax.nn.silu(gc_l), lru_l * jax.nn.silu(gd_l)], axis=-1) @ w_out
    if not ctx_out:
        return out_l, None
    out_c = jnp.concatenate([s5_c * jax.nn.silu(gc_c), lru_c * jax.nn.silu(gd_c)], axis=-1) @ w_out
    return out_l, out_c


def setup_inputs(seed: int = 0) -> dict:
    key = jax.random.key(seed)
    keys = jax.random.split(key, 40)
    ks = [keys[i] for i in range(40)]
    f32 = jnp.float32

    def nrm(shape, std):
        return jax.random.normal(ks.pop(), shape, f32) * std

    def near_one(shape):
        return 1.0 + nrm(shape, 0.02)

    d = D_MODEL
    x = nrm((BATCH, SEQ, d), 1.0)
    c = nrm((BATCH, d), 1.0)
    ctx = nrm((BATCH, CTX_LEN, d), 1.0)
    c_ctx = nrm((d,), 1.0)
    ada_w = nrm((DEPTH, d, 3 * d), 0.5 * d ** -0.5)
    ada_b = nrm((DEPTH, 3 * d), 0.02)
    norm_g = near_one((DEPTH, d))
    ev_w_in = nrm((N_EVEN, d, EVEN_IN), d ** -0.5)
    ev_w_out = nrm((N_EVEN, EVEN_MIX, d), EVEN_MIX ** -0.5)
    ev_q_g = near_one((N_EVEN, HEAD_DIM))
    ev_k_g = near_one((N_EVEN, HEAD_DIM))
    ev_sgu_g = near_one((N_EVEN, B_WIDTH))
    ev_ws = nrm((N_EVEN, B_GROUPS, CHUNK, CHUNK), CHUNK ** -0.5)
    ev_bs = near_one((N_EVEN, B_GROUPS, CHUNK))
    od_w_in = nrm((N_ODD, d, ODD_IN), d ** -0.5)
    od_w_out = nrm((N_ODD, ODD_MIX, d), ODD_MIX ** -0.5)
    sshape = (N_ODD, N_DIR, C_GROUPS, C_STATE)
    s5_lam_re = -0.5 + nrm(sshape, 0.01)
    s5_lam_im = math.pi * jnp.arange(C_STATE, dtype=f32) + nrm(sshape, 0.01)
    s5_log_dt = jax.random.uniform(ks.pop(), (N_ODD, N_DIR, C_GROUPS), f32, math.log(1e-3), math.log(1e-1))
    s5_b_re = nrm((N_ODD, N_DIR, C_GROUPS, C_STATE, C_GROUP_DIM), C_GROUP_DIM ** -0.5)
    s5_b_im = nrm((N_ODD, N_DIR, C_GROUPS, C_STATE, C_GROUP_DIM), C_GROUP_DIM ** -0.5)
    s5_c_re = nrm((N_ODD, N_DIR, C_GROUPS, C_GROUP_DIM, C_STATE), 0.5 ** 0.5)
    s5_c_im = nrm((N_ODD, N_DIR, C_GROUPS, C_GROUP_DIM, C_STATE), 0.5 ** 0.5)
    s5_d = nrm((N_ODD, C_WIDTH), 1.0)
    s5_glu_w = nrm((N_ODD, C_WIDTH, C_WIDTH), C_WIDTH ** -0.5)
    s5_glu_b = nrm((N_ODD, C_WIDTH), 0.02)
    lru_conv_w = nrm((N_ODD, CONV_W, D_WIDTH), CONV_W ** -0.5)
    lru_conv_b = nrm((N_ODD, D_WIDTH), 0.02)
    a0 = jax.random.uniform(ks.pop(), (N_ODD, N_DIR, D_WIDTH), f32, 0.9, 0.999)
    s = a0 ** (1.0 / LRU_C)
    lru_lam = jnp.log(s) - jnp.log1p(-s)
    lru_wa = nrm((N_ODD, N_DIR, D_BLOCKS, D_BLOCK_DIM, D_BLOCK_DIM), D_BLOCK_DIM ** -0.5)
    lru_ba = nrm((N_ODD, N_DIR, D_WIDTH), 0.02)
    lru_wx = nrm((N_ODD, N_DIR, D_BLOCKS, D_BLOCK_DIM, D_BLOCK_DIM), D_BLOCK_DIM ** -0.5)
    lru_bx = nrm((N_ODD, N_DIR, D_WIDTH), 0.02)
    return {'x': x, 'c': c, 'ctx': ctx, 'c_ctx': c_ctx, 'ada_w': ada_w, 'ada_b': ada_b, 'norm_g': norm_g,
            'ev_w_in': ev_w_in, 'ev_w_out': ev_w_out, 'ev_q_g': ev_q_g, 'ev_k_g': ev_k_g, 'ev_sgu_g': ev_sgu_g,
            'ev_ws': ev_ws, 'ev_bs': ev_bs, 'od_w_in': od_w_in, 'od_w_out': od_w_out,
            's5_lam_re': s5_lam_re, 's5_lam_im': s5_lam_im, 's5_log_dt': s5_log_dt, 's5_b_re': s5_b_re,
            's5_b_im': s5_b_im, 's5_c_re': s5_c_re, 's5_c_im': s5_c_im, 's5_d': s5_d, 's5_glu_w': s5_glu_w,
            's5_glu_b': s5_glu_b, 'lru_conv_w': lru_conv_w, 'lru_conv_b': lru_conv_b, 'lru_lam': lru_lam,
            'lru_wa': lru_wa, 'lru_ba': lru_ba, 'lru_wx': lru_wx, 'lru_bx': lru_bx}


def reference(x, c, ctx, c_ctx, ada_w, ada_b, norm_g, ev_w_in, ev_w_out, ev_q_g, ev_k_g, ev_sgu_g, ev_ws, ev_bs,
              od_w_in, od_w_out, s5_lam_re, s5_lam_im, s5_log_dt, s5_b_re, s5_b_im, s5_c_re, s5_c_im, s5_d,
              s5_glu_w, s5_glu_b, lru_conv_w, lru_conv_b, lru_lam, lru_wa, lru_ba, lru_wx, lru_bx):
    rows = x.shape[1] // GRID_W
    cos, sin = axial_rope_tables(rows)
    for layer in range(DEPTH):
        ctx_out = layer < DEPTH - 1
        shift, scale, gate = adaln(c, ada_w[layer], ada_b[layer])
        shift_c, scale_c, gate_c = adaln(c_ctx, ada_w[layer], ada_b[layer])
        h = modulate(x, norm_g[layer], shift[:, None], scale[:, None])
        hc = modulate(ctx, norm_g[layer], shift_c, scale_c)
        j = layer // 2
        if layer % 2 == 0:
            out_l, out_c = even_mixer(h, hc, ev_w_in[j], ev_w_out[j], ev_q_g[j], ev_k_g[j], ev_sgu_g[j], ev_ws[j], ev_bs[j], cos, sin, ctx_out)
        else:
            out_l, out_c = odd_mixer(h, hc, od_w_in[j], od_w_out[j], s5_lam_re[j], s5_lam_im[j], s5_log_dt[j], s5_b_re[j], s5_b_im[j], s5_c_re[j], s5_c_im[j], s5_d[j], s5_glu_w[j], s5_glu_b[j], lru_conv_w[j], lru_conv_b[j], lru_lam[j], lru_wa[j], lru_ba[j], lru_wx[j], lru_bx[j], ctx_out)
        x = x + gate[:, None] * out_l
        if ctx_out:
            ctx = ctx + gate_c * out_c
    return x
```

```python
import functools
import math

import jax
import jax.numpy as jnp
from jax import lax
from jax.experimental import pallas as pl
from jax.experimental.pallas import tpu as pltpu

F32 = jnp.float32
BF16 = jnp.bfloat16

EPS = 1e-6
GRID_W = 64
HEAD_DIM = 128
A_Q_HEADS = 12
A_KV_HEADS = 4
A_GROUP = A_Q_HEADS // A_KV_HEADS
B_GROUPS = 4
CHUNK = 128
ROPE_THETA = 10000.0
C_GROUP_DIM = 16
C_STATE = 64
D_BLOCKS = 8
CONV_W = 4
LRU_C = 8.0

LANES = 128
SUBLANES = 8
SUB = 8
PIECE_GROUPS = LANES // C_GROUP_DIM
VMEM_LIMIT = 50 * 1024 * 1024


def _cparams(*sem):
    return pltpu.CompilerParams(dimension_semantics=sem, vmem_limit_bytes=VMEM_LIMIT)


def _silu(x):
    return x * jax.nn.sigmoid(x)


def _pick_tile(rows, target, align=16):
    best = None
    for cand in range(align, min(rows, target) + 1, align):
        if rows % cand == 0:
            best = cand
    assert best is not None, (rows, target)
    return best


def _adaln_kernel(cc_ref, w_ref, b_ref, o_ref):
    s = _silu(cc_ref[...])
    o_ref[0] = jnp.dot(s, w_ref[0], preferred_element_type=F32,
                       precision=lax.Precision.HIGHEST) + b_ref[0]


def adaln_all(cc, ada_w, ada_b):
    depth, d, n3 = ada_w.shape
    tn = 512
    return pl.pallas_call(
        _adaln_kernel,
        grid=(depth, n3 // tn),
        in_specs=[pl.BlockSpec((8, d), lambda l, j: (0, 0)),
                  pl.BlockSpec((1, d, tn), lambda l, j: (l, 0, j)),
                  pl.BlockSpec((1, 1, tn), lambda l, j: (l, 0, j))],
        out_specs=pl.BlockSpec((1, 8, tn), lambda l, j: (l, 0, j)),
        out_shape=jax.ShapeDtypeStruct((depth, 8, n3), F32),
        compiler_params=_cparams("parallel", "parallel"),
        name="adaln",
    )(cc, ada_w, ada_b.reshape(depth, 1, n3))


def _inproj_kernel(x_ref, shl_ref, scl_ref, shc_ref, scc_ref, g_ref, w_ref, o_ref, h_scr, *, tm, ctx_len):
    i = pl.program_id(1)

    @pl.when(pl.program_id(2) == 0)
    def _():
        x = x_ref[0]
        y = x * lax.rsqrt(jnp.mean(x * x, axis=-1, keepdims=True) + EPS) * g_ref[...]
        row = i * tm + lax.broadcasted_iota(jnp.int32, (tm, 1), 0)
        is_ctx = row < ctx_len
        sc = jnp.where(is_ctx, scc_ref[...], scl_ref[0])
        sh = jnp.where(is_ctx, shc_ref[...], shl_ref[0])
        h_scr[...] = (y * (1.0 + sc) + sh).astype(BF16)

    o_ref[0, 0] = jnp.dot(h_scr[...], w_ref[...], preferred_element_type=F32)


def inproj(x_all, shift_l, scale_l, shift_c, scale_c, g, w, ctx_len, tn, tm):
    b, t, d = x_all.shape
    n = w.shape[1]
    return pl.pallas_call(
        functools.partial(_inproj_kernel, tm=tm, ctx_len=ctx_len),
        grid=(b, t // tm, n // tn),
        in_specs=[pl.BlockSpec((1, tm, d), lambda bb, i, j: (bb, i, 0)),
                  pl.BlockSpec((1, 1, d), lambda bb, i, j: (bb, 0, 0)),
                  pl.BlockSpec((1, 1, d), lambda bb, i, j: (bb, 0, 0)),
                  pl.BlockSpec((1, d), lambda bb, i, j: (0, 0)),
                  pl.BlockSpec((1, d), lambda bb, i, j: (0, 0)),
                  pl.BlockSpec((1, d), lambda bb, i, j: (0, 0)),
                  pl.BlockSpec((d, tn), lambda bb, i, j: (0, j))],
        out_specs=pl.BlockSpec((1, 1, tm, tn), lambda bb, i, j: (j, bb, i, 0)),
        out_shape=jax.ShapeDtypeStruct((n // tn, b, t, tn), F32),
        scratch_shapes=[pltpu.VMEM((tm, d), BF16)],
        compiler_params=_cparams("parallel", "parallel", "arbitrary"),
        name="inproj",
    )(x_all, shift_l, scale_l, shift_c, scale_c, g, w)


def _outproj_kernel(ma_ref, mb_ref, w1_ref, w2_ref, x_ref, gl_ref, gc_ref, o_ref, *, ctx_tiles):
    i = pl.program_id(1)
    acc = jnp.dot(ma_ref[0], w1_ref[...], preferred_element_type=F32)
    acc = acc + jnp.dot(mb_ref[0], w2_ref[...], preferred_element_type=F32)
    gate = jnp.where(i < ctx_tiles, gc_ref[...], gl_ref[0])
    o_ref[0] = x_ref[0] + gate * acc


def outproj(ma, mb, w1, w2, x_all, gate_l, gate_c, ctx_len, latent_only):
    b, t, d = x_all.shape
    k1, k2 = ma.shape[2], mb.shape[2]
    tm = ctx_len
    off = 1 if latent_only else 0
    t_out = t - ctx_len if latent_only else t
    return pl.pallas_call(
        functools.partial(_outproj_kernel, ctx_tiles=0 if latent_only else 1),
        grid=(b, t_out // tm),
        in_specs=[pl.BlockSpec((1, tm, k1), lambda bb, i: (bb, i + off, 0)),
                  pl.BlockSpec((1, tm, k2), lambda bb, i: (bb, i + off, 0)),
                  pl.BlockSpec((k1, d), lambda bb, i: (0, 0)),
                  pl.BlockSpec((k2, d), lambda bb, i: (0, 0)),
                  pl.BlockSpec((1, tm, d), lambda bb, i: (bb, i + off, 0)),
                  pl.BlockSpec((1, 1, d), lambda bb, i: (bb, 0, 0)),
                  pl.BlockSpec((1, d), lambda bb, i: (0, 0))],
        out_specs=pl.BlockSpec((1, tm, d), lambda bb, i: (bb, i, 0)),
        out_shape=jax.ShapeDtypeStruct((b, t_out, d), F32),
        compiler_params=_cparams("parallel", "parallel"),
        name="outproj",
    )(ma, mb, w1, w2, x_all, gate_l, gate_c)


def _qkprep_kernel(x_ref, g_ref, cos_ref, sin_ref, o_ref):
    x = x_ref[0, 0]
    y = x * lax.rsqrt(jnp.mean(x * x, axis=-1, keepdims=True) + EPS) * g_ref[0]
    lane = lax.broadcasted_iota(jnp.int32, y.shape, 1)
    partner = jnp.where(lane % 2 == 0, pltpu.roll(y, HEAD_DIM - 1, 1), pltpu.roll(y, 1, 1))
    o_ref[0] = (y * cos_ref[...] + partner * sin_ref[...]).astype(BF16)


def qkprep(slabs, gains, cos_t, sin_t):
    _, b, t, tn = slabs.shape
    per = tn // HEAD_DIM
    nh = A_KV_HEADS + A_Q_HEADS

    def x_map(bb, hh):
        qh = hh - A_KV_HEADS
        slab = jnp.where(hh < A_KV_HEADS, 0, 2 + qh // per)
        col = jnp.where(hh < A_KV_HEADS, hh, qh % per)
        return (slab, bb, 0, col)

    return pl.pallas_call(
        _qkprep_kernel,
        grid=(b, nh),
        in_specs=[pl.BlockSpec((1, 1, t, HEAD_DIM), x_map),
                  pl.BlockSpec((1, 1, HEAD_DIM), lambda bb, hh: (hh, 0, 0)),
                  pl.BlockSpec((t, HEAD_DIM), lambda bb, hh: (0, 0)),
                  pl.BlockSpec((t, HEAD_DIM), lambda bb, hh: (0, 0))],
        out_specs=pl.BlockSpec((1, t, HEAD_DIM), lambda bb, hh: (bb, 0, hh)),
        out_shape=jax.ShapeDtypeStruct((b, t, nh * HEAD_DIM), BF16),
        compiler_params=_cparams("parallel", "parallel"),
        name="qkprep",
    )(slabs, gains, cos_t, sin_t)


def _attn_kernel(q_ref, k_ref, v_ref, ga_ref, o_ref, *, ctx_len):
    iq = pl.program_id(3)
    q = q_ref[0]

    def attend(k, v):
        s = lax.dot_general(q, k, (((1,), (1,)), ((), ())), preferred_element_type=F32)
        m = jnp.max(s, axis=-1, keepdims=True)
        p = jnp.exp(s - m)
        l = jnp.sum(p, axis=-1, keepdims=True)
        o = jnp.dot(p.astype(BF16), v.astype(BF16), preferred_element_type=F32) / l
        o_ref[0] = (o * _silu(ga_ref[0, 0])).astype(BF16)

    @pl.when(iq == 0)
    def _():
        attend(k_ref[0, :ctx_len], v_ref[0, 0, :ctx_len])

    @pl.when(iq > 0)
    def _():
        attend(k_ref[0], v_ref[0, 0])


def attention(qk, slabs, ctx_len):
    b, t, _ = qk.shape
    tn = slabs.shape[3]
    per = tn // HEAD_DIM
    tq = ctx_len

    def ga_map(bb, kv, g, iq):
        h = kv * A_GROUP + g
        return (7 + h // per, bb, iq, h % per)

    return pl.pallas_call(
        functools.partial(_attn_kernel, ctx_len=ctx_len),
        grid=(b, A_KV_HEADS, A_GROUP, t // tq),
        in_specs=[pl.BlockSpec((1, tq, HEAD_DIM), lambda bb, kv, g, iq: (bb, iq, A_KV_HEADS + kv * A_GROUP + g)),
                  pl.BlockSpec((1, t, HEAD_DIM), lambda bb, kv, g, iq: (bb, 0, kv)),
                  pl.BlockSpec((1, 1, t, HEAD_DIM), lambda bb, kv, g, iq: (1, bb, 0, kv)),
                  pl.BlockSpec((1, 1, tq, HEAD_DIM), ga_map)],
        out_specs=pl.BlockSpec((1, tq, HEAD_DIM), lambda bb, kv, g, iq: (bb, iq, kv * A_GROUP + g)),
        out_shape=jax.ShapeDtypeStruct((b, t, A_Q_HEADS * HEAD_DIM), BF16),
        compiler_params=_cparams("parallel", "parallel", "parallel", "arbitrary"),
        name="attention",
    )(qk, qk, slabs, slabs)


def _sgu_kernel(u_ref, v_ref, gb_ref, ws_ref, bs_ref, g_ref, o_ref):
    outs = []
    for grp in range(B_GROUPS):
        cols = slice(grp * LANES, (grp + 1) * LANES)
        v = v_ref[0, 0, :, cols]
        vn = v * lax.rsqrt(jnp.mean(v * v, axis=-1, keepdims=True) + EPS) * g_ref[:, cols]
        mixed = jnp.dot(ws_ref[grp], vn.astype(BF16), preferred_element_type=F32) + bs_ref[grp]
        outs.append(u_ref[0, 0, :, cols] * mixed * _silu(gb_ref[0, 0, :, cols]))
    o_ref[0] = jnp.concatenate(outs, axis=1).astype(BF16)


def sgu(slabs, ws, bs, g):
    _, b, t, tn = slabs.shape
    spec = lambda s: pl.BlockSpec((1, 1, CHUNK, tn), lambda bb, c: (s, bb, c, 0))
    return pl.pallas_call(
        _sgu_kernel,
        grid=(b, t // CHUNK),
        in_specs=[spec(5), spec(6), spec(10),
                  pl.BlockSpec((B_GROUPS, CHUNK, CHUNK), lambda bb, c: (0, 0, 0)),
                  pl.BlockSpec((B_GROUPS, CHUNK, 1), lambda bb, c: (0, 0, 0)),
                  pl.BlockSpec((1, tn), lambda bb, c: (0, 0))],
        out_specs=pl.BlockSpec((1, CHUNK, tn), lambda bb, c: (bb, c, 0)),
        out_shape=jax.ShapeDtypeStruct((b, t, tn), BF16),
        compiler_params=_cparams("parallel", "parallel"),
        name="sgu",
    )(slabs, slabs, slabs, ws, bs, g)


def _s5_piece_cols(ref, piece, width):
    return jnp.concatenate(
        [ref[:, pl.ds(t * width + piece * LANES, LANES)] for t in range(SUB)], axis=1)


def _s5_in_kernel(u_ref, w_ref, o_ref, *, n_piece, width):
    i = pl.program_id(1)
    for piece in range(n_piece):
        @pl.when(i == piece)
        def _(piece=piece):
            x = _s5_piece_cols(u_ref, piece, width).astype(BF16)
            o_ref[...] = jnp.dot(x, w_ref[0], preferred_element_type=F32)


def s5_in(u2, w_in, tm):
    m, wide = u2.shape
    n_piece = w_in.shape[0]
    ncol = w_in.shape[2]
    return pl.pallas_call(
        functools.partial(_s5_in_kernel, n_piece=n_piece, width=wide // SUB),
        grid=(m // tm, n_piece),
        in_specs=[pl.BlockSpec((tm, wide), lambda r, i: (r, 0)),
                  pl.BlockSpec((1, w_in.shape[1], ncol), lambda r, i: (i, 0, 0))],
        out_specs=pl.BlockSpec((tm, ncol), lambda r, i: (r, i)),
        out_shape=jax.ShapeDtypeStruct((m, n_piece * ncol), F32),
        compiler_params=_cparams("parallel", "arbitrary"),
        name="s5_in",
    )(u2, w_in)


def _s5_scan_kernel(g_ref, c_ref, o_ref, *, ctx_pairs, all_pairs, half):
    rows = 2 * SUBLANES
    rowi = lax.broadcasted_iota(jnp.int32, (SUBLANES, half), 0)

    def tile(d, g_re, g_im, c_re, c_im):
        bwd = d == 1
        b_re, b_im = g_re, g_im
        for idx, dd in enumerate((1, 2, 4)):
            a_re, a_im = c_ref[d, idx, 0], c_ref[d, idx, 1]
            sh = SUBLANES - dd if bwd else dd
            s_re, s_im = pltpu.roll(b_re, sh, 0), pltpu.roll(b_im, sh, 0)
            b_re, b_im = (b_re + a_re * s_re - a_im * s_im,
                          b_im + a_re * s_im + a_im * s_re)
        a_re, a_im = c_ref[d, 3, 0], c_ref[d, 3, 1]
        h_re = b_re + a_re * c_re - a_im * c_im
        h_im = b_im + a_re * c_im + a_im * c_re
        if bwd:
            p_re = jnp.where(rowi == SUBLANES - 1, c_re, pltpu.roll(h_re, SUBLANES - 1, 0))
            p_im = jnp.where(rowi == SUBLANES - 1, c_im, pltpu.roll(h_im, SUBLANES - 1, 0))
            return p_re, p_im, h_re[0:1], h_im[0:1]
        p_re = jnp.where(rowi == 0, c_re, pltpu.roll(h_re, 1, 0))
        p_im = jnp.where(rowi == 0, c_im, pltpu.roll(h_im, 1, 0))
        return p_re, p_im, h_re[SUBLANES - 1:SUBLANES], h_im[SUBLANES - 1:SUBLANES]

    def pair(d, pi, carry):
        c_re, c_im = carry
        r0 = pl.multiple_of(pi * rows, rows)
        base = d * 2 * half
        order = (1, 0) if d == 1 else (0, 1)
        res = [None, None]
        for which in order:
            rr = pl.ds(r0 + which * SUBLANES, SUBLANES)
            g_re = g_ref[0, rr, pl.ds(base, half)]
            g_im = g_ref[0, rr, pl.ds(base + half, half)]
            p_re, p_im, c_re, c_im = tile(d, g_re, g_im, c_re, c_im)
            res[which] = (p_re, p_im)
        o_ref[0, pl.ds(r0, rows), pl.ds(base, half)] = jnp.concatenate(
            [res[0][0], res[1][0]], axis=0).astype(BF16)
        o_ref[0, pl.ds(r0, rows), pl.ds(base + half, half)] = jnp.concatenate(
            [res[0][1], res[1][1]], axis=0).astype(BF16)
        return c_re, c_im

    zero = (jnp.zeros((1, half), F32), jnp.zeros((1, half), F32))
    lax.fori_loop(0, all_pairs, lambda pi, c: pair(0, pi, c), zero)
    c = lax.fori_loop(0, ctx_pairs, lambda n, c: pair(1, ctx_pairs - 1 - n, c), zero)
    lax.fori_loop(0, all_pairs - ctx_pairs, lambda n, c: pair(1, all_pairs - 1 - n, c), c)


def s5_scan(g3, consts, ctx_rows):
    b, rows, wide = g3.shape
    half = PIECE_GROUPS * C_STATE
    n_piece = wide // (4 * half)
    return pl.pallas_call(
        functools.partial(_s5_scan_kernel, ctx_pairs=ctx_rows // (2 * SUBLANES),
                          all_pairs=rows // (2 * SUBLANES), half=half),
        grid=(b, n_piece),
        in_specs=[pl.BlockSpec((1, rows, 4 * half), lambda bb, i: (bb, 0, i)),
                  pl.BlockSpec((2, 4, 2, SUBLANES, half), lambda bb, i: (0, 0, 0, 0, i))],
        out_specs=pl.BlockSpec((1, rows, 4 * half), lambda bb, i: (bb, 0, i)),
        out_shape=jax.ShapeDtypeStruct((b, rows, wide), BF16),
        compiler_params=_cparams("parallel", "parallel"),
        name="s5_scan",
    )(g3, consts)


def _s5_out_kernel(u_ref, hp_ref, t_ref, w_ref, y_ref, *, n_piece, width):
    i = pl.program_id(1)
    for piece in range(n_piece):
        @pl.when(i == piece)
        def _(piece=piece):
            x = _s5_piece_cols(u_ref, piece, width).astype(BF16)
            y = jnp.dot(x, t_ref[0], preferred_element_type=F32)
            y = y + jnp.dot(hp_ref[...], w_ref[0], preferred_element_type=F32)
            for t in range(SUB):
                y_ref[:, pl.ds(t * width + piece * LANES, LANES)] = y[:, t * LANES:(t + 1) * LANES]


def s5_out(u2, hp2, t_intra, w_out, tm):
    m, wide = u2.shape
    n_piece = t_intra.shape[0]
    kh = w_out.shape[1]
    return pl.pallas_call(
        functools.partial(_s5_out_kernel, n_piece=n_piece, width=wide // SUB),
        grid=(m // tm, n_piece),
        in_specs=[pl.BlockSpec((tm, wide), lambda r, i: (r, 0)),
                  pl.BlockSpec((tm, kh), lambda r, i: (r, i)),
                  pl.BlockSpec((1,) + t_intra.shape[1:], lambda r, i: (i, 0, 0)),
                  pl.BlockSpec((1,) + w_out.shape[1:], lambda r, i: (i, 0, 0))],
        out_specs=pl.BlockSpec((tm, wide), lambda r, i: (r, 0)),
        out_shape=jax.ShapeDtypeStruct((m, wide), F32),
        compiler_params=_cparams("parallel", "arbitrary"),
        name="s5_out",
    )(u2, hp2, t_intra, w_out)


def _s5_finish_kernel(y_ref, u_ref, gc_ref, d_ref, w_ref, b_ref, o_ref):
    y = y_ref[...] + d_ref[...] * u_ref[0]
    y = jax.nn.gelu(y)
    z = jnp.dot(y.astype(BF16), w_ref[...], preferred_element_type=F32) + b_ref[...]
    o_ref[...] = (y * jax.nn.sigmoid(z) * _silu(gc_ref[0])).astype(BF16)


def s5_finish(y, slabs2, d_skip, glu_w, glu_b, tm):
    r, w = y.shape
    return pl.pallas_call(
        _s5_finish_kernel,
        grid=(r // tm,),
        in_specs=[pl.BlockSpec((tm, w), lambda i: (i, 0)),
                  pl.BlockSpec((1, tm, w), lambda i: (0, i, 0)),
                  pl.BlockSpec((1, tm, w), lambda i: (2, i, 0)),
                  pl.BlockSpec((1, w), lambda i: (0, 0)),
                  pl.BlockSpec((w, w), lambda i: (0, 0)),
                  pl.BlockSpec((1, w), lambda i: (0, 0))],
        out_specs=pl.BlockSpec((tm, w), lambda i: (i, 0)),
        out_shape=jax.ShapeDtypeStruct((r, w), BF16),
        compiler_params=_cparams("parallel"),
        name="s5_finish",
    )(y, slabs2, slabs2, d_skip, glu_w, glu_b)


def s5_derive(lam_re, lam_im, log_dt, b_re, b_im, c_re, c_im):
    hi = lax.Precision.HIGHEST
    n_dir, g, n = lam_re.shape
    p = b_re.shape[-1]
    npiece = g // PIECE_GROUPS
    dt = jnp.exp(log_dt)[None, :, :, None]

    def powers(ks):
        k = jnp.asarray(list(ks), F32).reshape(-1, 1, 1, 1)
        mag = jnp.exp(k * lam_re[None] * dt)
        ang = k * lam_im[None] * dt
        return mag * jnp.cos(ang), mag * jnp.sin(ang)

    pw_re, pw_im = powers(range(SUB + 1))
    den = lam_re * lam_re + lam_im * lam_im
    nr, ni = pw_re[1] - 1.0, pw_im[1]
    cf_re = (nr * lam_re + ni * lam_im) / den
    cf_im = (ni * lam_re - nr * lam_im) / den
    bb_re = cf_re[..., None] * b_re - cf_im[..., None] * b_im
    bb_im = cf_re[..., None] * b_im + cf_im[..., None] * b_re
    ca_re = c_re[None] * pw_re[:, :, :, None, :] - c_im[None] * pw_im[:, :, :, None, :]
    ca_im = c_re[None] * pw_im[:, :, :, None, :] + c_im[None] * pw_re[:, :, :, None, :]

    kern = (jnp.einsum('tdgpn,dgnq->tdgpq', ca_re[:SUB], bb_re, precision=hi)
            - jnp.einsum('tdgpn,dgnq->tdgpq', ca_im[:SUB], bb_im, precision=hi))
    s_idx = jnp.arange(SUB)[:, None]
    t_idx = jnp.arange(SUB)[None, :]
    lag = t_idx - s_idx
    kf = kern[jnp.clip(lag, 0, SUB - 1), 0] * (lag >= 0)[:, :, None, None, None].astype(F32)
    kb = kern[jnp.clip(-lag, 0, SUB - 1), 1] * (lag <= 0)[:, :, None, None, None].astype(F32)
    kc = (kf + kb).reshape(SUB, SUB, npiece, PIECE_GROUPS, p, p)
    eye = jnp.eye(PIECE_GROUPS, dtype=F32)
    t_intra = jnp.einsum('stigpq,gh->isgqthp', kc, eye).reshape(
        npiece, SUB * PIECE_GROUPS * p, SUB * PIECE_GROUPS * p)

    def cmul(xr, xi, yr, yi):
        return xr * yr - xi * yi, xr * yi + xi * yr

    dn_re, dn_im = powers(range(SUB - 1, -1, -1))
    f_re, f_im = cmul(dn_re[:, 0][:, :, :, None], dn_im[:, 0][:, :, :, None],
                      bb_re[0][None], bb_im[0][None])
    r_re, r_im = cmul(pw_re[:SUB, 1][:, :, :, None], pw_im[:SUB, 1][:, :, :, None],
                      bb_re[1][None], bb_im[1][None])
    v = jnp.stack([jnp.stack([f_re, f_im]), jnp.stack([r_re, r_im])])
    v = v.reshape(n_dir, 2, SUB, npiece, PIECE_GROUPS, n, p)
    w_in = jnp.einsum('dcsignq,gh->isgqdchn', v, eye).reshape(
        npiece, SUB * PIECE_GROUPS * p, n_dir * 2 * PIECE_GROUPS * n)

    o_f = jnp.stack([ca_re[1:SUB + 1, 0], -ca_im[1:SUB + 1, 0]])
    up_re, up_im = powers(range(SUB, 0, -1))
    cb_re = c_re[1][None] * up_re[:, 1, :, None, :] - c_im[1][None] * up_im[:, 1, :, None, :]
    cb_im = c_re[1][None] * up_im[:, 1, :, None, :] + c_im[1][None] * up_re[:, 1, :, None, :]
    o_b = jnp.stack([cb_re, -cb_im])
    o = jnp.stack([o_f, o_b]).reshape(n_dir, 2, SUB, npiece, PIECE_GROUPS, p, n)
    w_out = jnp.einsum('dctigpn,gh->idcgnthp', o, eye).reshape(
        npiece, n_dir * 2 * PIECE_GROUPS * n, SUB * PIECE_GROUPS * p)

    q_re, q_im = powers([SUB * m for m in range(1, SUBLANES + 1)])
    q_re = q_re.reshape(SUBLANES, n_dir, g * n)
    q_im = q_im.reshape(SUBLANES, n_dir, g * n)
    row = jnp.arange(SUBLANES)
    consts = []
    for d in range(n_dir):
        kinds = []
        for dd in (1, 2, 4):
            keep = (row + dd <= SUBLANES - 1) if d == 1 else (row >= dd)
            m = keep.astype(F32)[:, None]
            kinds.append(jnp.stack([m * q_re[dd - 1, d][None], m * q_im[dd - 1, d][None]]))
        sel = (SUBLANES - 1 - row) if d == 1 else row
        kinds.append(jnp.stack([q_re[sel, d], q_im[sel, d]]))
        consts.append(jnp.stack(kinds))
    consts = jnp.stack(consts)
    return t_intra.astype(BF16), w_in.astype(BF16), w_out.astype(BF16), consts


def _lru_kernel(*refs, bwd, nch, tc, width):
    if bwd:
        (x_ref, xp_ref, xn_ref, cw_ref, cb_ref, lam_ref, wa_ref, ba_ref, wx_ref, bx_ref,
         hf_ref, gd_ref, o_ref, pad_scr, a_scr, b_scr, carry_scr) = refs
    else:
        (x_ref, xp_ref, xn_ref, cw_ref, cb_ref, lam_ref, wa_ref, ba_ref, wx_ref, bx_ref,
         o_ref, pad_scr, a_scr, b_scr, carry_scr) = refs
    k = pl.program_id(1)
    chunk = jnp.where(k == 0, 0, nch - k) if bwd else k
    prev_ok = jnp.logical_and(chunk != 0, chunk != 1)
    next_ok = jnp.logical_and(chunk != 0, chunk != nch - 1)
    pad_scr[0:SUBLANES] = jnp.where(prev_ok, xp_ref[0, 0], 0.0)
    pad_scr[SUBLANES:SUBLANES + tc] = x_ref[0, 0]
    pad_scr[SUBLANES + tc:2 * SUBLANES + tc] = jnp.where(next_ok, xn_ref[0, 0], 0.0)
    xc = cb_ref[...]
    for tap in range(CONV_W):
        xc = xc + cw_ref[tap:tap + 1] * pad_scr[SUBLANES - 1 + tap:SUBLANES - 1 + tap + tc]

    bw = width // D_BLOCKS
    r_parts, i_parts = [], []
    for blk in range(D_BLOCKS):
        xb = xc[:, blk * bw:(blk + 1) * bw].astype(BF16)
        r_parts.append(jnp.dot(xb, wa_ref[blk], preferred_element_type=F32))
        i_parts.append(jnp.dot(xb, wx_ref[blk], preferred_element_type=F32))
    r = jax.nn.sigmoid(jnp.concatenate(r_parts, axis=1) + ba_ref[...])
    ig = jax.nn.sigmoid(jnp.concatenate(i_parts, axis=1) + bx_ref[...])
    nl = -lam_ref[...]
    e = jnp.exp(-jnp.abs(nl))
    e1 = 1.0 + e
    log1p_e = jnp.where(e1 == 1.0, e, jnp.log(e1) * (e / jnp.where(e1 == 1.0, 1.0, e1 - 1.0)))
    softplus = jnp.maximum(nl, 0.0) + log1p_e
    log_a = (-LRU_C * softplus) * r
    a_scr[...] = jnp.exp(log_a)
    th = jnp.tanh(log_a)
    b_scr[...] = jnp.sqrt(-2.0 * th / (1.0 - th)) * (ig * xc)

    @pl.when(k == 0)
    def _():
        carry_scr[...] = jnp.zeros_like(carry_scr)

    rowi = lax.broadcasted_iota(jnp.int32, (SUBLANES, width), 0)
    n_tiles = tc // SUBLANES

    def body(n, carry):
        ti = n_tiles - 1 - n if bwd else n
        rr = pl.ds(pl.multiple_of(ti * SUBLANES, SUBLANES), SUBLANES)
        a, b = a_scr[rr], b_scr[rr]
        for dd in (1, 2, 4):
            keep = (rowi + dd <= SUBLANES - 1) if bwd else (rowi >= dd)
            sh = SUBLANES - dd if bwd else dd
            b = b + a * jnp.where(keep, pltpu.roll(b, sh, 0), 0.0)
            a = a * jnp.where(keep, pltpu.roll(a, sh, 0), 1.0)
        h = b + a * carry
        b_scr[rr] = h
        return h[0:1] if bwd else h[SUBLANES - 1:SUBLANES]

    carry_scr[...] = lax.fori_loop(0, n_tiles, body, carry_scr[...])
    if bwd:
        o_ref[0] = ((hf_ref[0] + b_scr[...]) * _silu(gd_ref[0, 0])).astype(BF16)
    else:
        o_ref[0] = b_scr[...]


def lru_pass(slabs, conv_w, conv_b, lam, wa, ba, wx, bx, ctx_len, hf=None):
    _, b, t, w = slabs.shape
    bwd = hf is not None
    tc = ctx_len
    nch = t // tc
    hb = tc // SUBLANES
    nhb = t // SUBLANES

    def ch(k):
        return jnp.where(k == 0, 0, nch - k) if bwd else k

    in_specs = [pl.BlockSpec((1, 1, tc, w), lambda bb, k: (1, bb, ch(k), 0)),
                pl.BlockSpec((1, 1, SUBLANES, w), lambda bb, k: (1, bb, jnp.maximum(ch(k) * hb - 1, 0), 0)),
                pl.BlockSpec((1, 1, SUBLANES, w), lambda bb, k: (1, bb, jnp.minimum((ch(k) + 1) * hb, nhb - 1), 0)),
                pl.BlockSpec((CONV_W, w), lambda bb, k: (0, 0)),
                pl.BlockSpec((1, w), lambda bb, k: (0, 0)),
                pl.BlockSpec((1, w), lambda bb, k: (0, 0)),
                pl.BlockSpec(wa.shape, lambda bb, k: (0, 0, 0)),
                pl.BlockSpec((1, w), lambda bb, k: (0, 0)),
                pl.BlockSpec(wx.shape, lambda bb, k: (0, 0, 0)),
                pl.BlockSpec((1, w), lambda bb, k: (0, 0))]
    args = [slabs, slabs, slabs, conv_w, conv_b, lam, wa, ba, wx, bx]
    if bwd:
        in_specs += [pl.BlockSpec((1, tc, w), lambda bb, k: (bb, ch(k), 0)),
                     pl.BlockSpec((1, 1, tc, w), lambda bb, k: (3, bb, ch(k), 0))]
        args += [hf, slabs]
    return pl.pallas_call(
        functools.partial(_lru_kernel, bwd=bwd, nch=nch, tc=tc, width=w),
        grid=(b, nch),
        in_specs=in_specs,
        out_specs=pl.BlockSpec((1, tc, w), lambda bb, k: (bb, ch(k), 0)),
        out_shape=jax.ShapeDtypeStruct((b, t, w), BF16 if bwd else F32),
        scratch_shapes=[pltpu.VMEM((tc + 2 * SUBLANES, w), F32),
                        pltpu.VMEM((tc, w), F32),
                        pltpu.VMEM((tc, w), F32),
                        pltpu.VMEM((1, w), F32)],
        compiler_params=_cparams("parallel", "arbitrary"),
        name="lru_bwd" if bwd else "lru_fwd",
    )(*args)


def rope_tables(ctx_len, seq):
    rows = seq // GRID_W
    axis_dim = HEAD_DIM // 2
    row = jnp.repeat(jnp.arange(rows), GRID_W).astype(F32)
    col = jnp.tile(jnp.arange(GRID_W), rows).astype(F32)
    inv = ROPE_THETA ** (-jnp.arange(0, axis_dim, 2, dtype=F32) / axis_dim)
    ang = jnp.concatenate([row[:, None] * inv, col[:, None] * inv], axis=-1)
    cos, sin = jnp.cos(ang), jnp.sin(ang)
    cos_t = jnp.repeat(cos, 2, axis=-1)
    sin_t = jnp.stack([-sin, sin], axis=-1).reshape(seq, HEAD_DIM)
    cos_t = jnp.concatenate([jnp.ones((ctx_len, HEAD_DIM), F32), cos_t], axis=0)
    sin_t = jnp.concatenate([jnp.zeros((ctx_len, HEAD_DIM), F32), sin_t], axis=0)
    return cos_t, sin_t


def even_layer(x_all, mod, g, w_in, w_out, q_g, k_g, sgu_g, ws, bs, cos_t, sin_t, ctx_len, latent_only):
    shift_l, scale_l, gate_l, shift_c, scale_c, gate_c = mod
    slabs = inproj(x_all, shift_l, scale_l, shift_c, scale_c, g, w_in.astype(BF16), ctx_len, tn=512,
                   tm=_pick_tile(x_all.shape[1], 768))
    gains = jnp.concatenate([jnp.broadcast_to(k_g, (A_KV_HEADS, HEAD_DIM)),
                             jnp.broadcast_to(q_g * HEAD_DIM ** -0.5, (A_Q_HEADS, HEAD_DIM))])[:, None, :]
    qk = qkprep(slabs, gains, cos_t, sin_t)
    mix_a = attention(qk, slabs, ctx_len)
    mix_b = sgu(slabs, ws.astype(BF16), bs[:, :, None], sgu_g[None, :])
    ka = mix_a.shape[2]
    w_out = w_out.astype(BF16)
    return outproj(mix_a, mix_b, w_out[:ka], w_out[ka:], x_all, gate_l, gate_c, ctx_len, latent_only)


def odd_layer(x_all, mod, g, w_in, w_out, s5p, d_skip, glu_w, glu_b, conv_w, conv_b, lam, wa, ba, wx, bx,
              ctx_len, latent_only):
    shift_l, scale_l, gate_l, shift_c, scale_c, gate_c = mod
    b, t, _ = x_all.shape
    slabs = inproj(x_all, shift_l, scale_l, shift_c, scale_c, g, w_in.astype(BF16), ctx_len, tn=1024,
                   tm=_pick_tile(t, 768))
    cw = slabs.shape[3]
    rows = b * t // SUB
    t_intra, w_s5in, w_s5out, consts = s5_derive(*s5p)
    u2 = slabs[0].reshape(rows, SUB * cw)
    gmat = s5_in(u2, w_s5in, tm=_pick_tile(rows, 288))
    hp = s5_scan(gmat.reshape(b, t // SUB, -1), consts, ctx_len // SUB)
    y = s5_out(u2, hp.reshape(rows, -1), t_intra, w_s5out, tm=_pick_tile(rows, 144))
    slabs2 = slabs.reshape(slabs.shape[0], b * t, cw)
    mix_c = s5_finish(y.reshape(b * t, cw), slabs2, d_skip[None, :], glu_w.astype(BF16), glu_b[None, :],
                      tm=_pick_tile(b * t, 512))
    hf = lru_pass(slabs, conv_w, conv_b[None, :], lam[0][None, :], wa[0].astype(BF16), ba[0][None, :],
                  wx[0].astype(BF16), bx[0][None, :], ctx_len)
    mix_d = lru_pass(slabs, conv_w, conv_b[None, :], lam[1][None, :], wa[1].astype(BF16), ba[1][None, :],
                     wx[1].astype(BF16), bx[1][None, :], ctx_len, hf=hf)
    w_out = w_out.astype(BF16)
    return outproj(mix_c.reshape(b, t, cw), mix_d, w_out[:cw], w_out[cw:], x_all, gate_l, gate_c,
                   ctx_len, latent_only)


def kernel(x, c, ctx, c_ctx, ada_w, ada_b, norm_g, ev_w_in, ev_w_out, ev_q_g, ev_k_g, ev_sgu_g, ev_ws, ev_bs,
           od_w_in, od_w_out, s5_lam_re, s5_lam_im, s5_log_dt, s5_b_re, s5_b_im, s5_c_re, s5_c_im, s5_d,
           s5_glu_w, s5_glu_b, lru_conv_w, lru_conv_b, lru_lam, lru_wa, lru_ba, lru_wx, lru_bx):
    b, seq, d = x.shape
    ctx_len = ctx.shape[1]
    depth = ada_w.shape[0]
    assert b + 1 <= SUBLANES and seq % ctx_len == 0
    cos_t, sin_t = rope_tables(ctx_len, seq)
    cc = jnp.zeros((SUBLANES, d), F32).at[:b].set(c).at[b].set(c_ctx)
    mods = adaln_all(cc, ada_w, ada_b)
    x_all = jnp.concatenate([ctx, x], axis=1)
    for layer in range(depth):
        m = mods[layer]
        mod = tuple(m[:b, None, k * d:(k + 1) * d] for k in range(3)) + \
            tuple(m[b:b + 1, k * d:(k + 1) * d] for k in range(3))
        last = layer == depth - 1
        j = layer // 2
        g = norm_g[layer][None, :]
        if layer % 2 == 0:
            x_all = even_layer(x_all, mod, g, ev_w_in[j], ev_w_out[j], ev_q_g[j], ev_k_g[j], ev_sgu_g[j],
                               ev_ws[j], ev_bs[j], cos_t, sin_t, ctx_len, last)
        else:
            s5p = (s5_lam_re[j], s5_lam_im[j], s5_log_dt[j], s5_b_re[j], s5_b_im[j], s5_c_re[j], s5_c_im[j])
            x_all = odd_layer(x_all, mod, g, od_w_in[j], od_w_out[j], s5p, s5_d[j], s5_glu_w[j], s5_glu_b[j],
                              lru_conv_w[j], lru_conv_b[j], lru_lam[j], lru_wa[j], lru_ba[j], lru_wx[j],
                              lru_bx[j], ctx_len, last)
    return x_all
```

```python
import functools
import math

import jax
import jax.numpy as jnp
from jax import lax
from jax.experimental import pallas as pl
from jax.experimental.pallas import tpu as pltpu

F32 = jnp.float32
BF16 = jnp.bfloat16

EPS = 1e-6
GRID_W = 64
HEAD_DIM = 128
A_Q_HEADS = 12
A_KV_HEADS = 4
A_GROUP = A_Q_HEADS // A_KV_HEADS
B_GROUPS = 4
CHUNK = 128
ROPE_THETA = 10000.0
C_GROUP_DIM = 16
C_STATE = 64
D_BLOCKS = 8
CONV_W = 4
LRU_C = 8.0

LANES = 128
SUBLANES = 8
SUB = 8
PIECE_GROUPS = LANES // C_GROUP_DIM
VMEM_LIMIT = 50 * 1024 * 1024


def _cparams(*sem):
    return pltpu.CompilerParams(dimension_semantics=sem, vmem_limit_bytes=VMEM_LIMIT)


def _silu(x):
    return x * jax.nn.sigmoid(x)


def _pick_tile(rows, target, align=16):
    best = None
    for cand in range(align, min(rows, target) + 1, align):
        if rows % cand == 0:
            best = cand
    assert best is not None, (rows, target)
    return best


def _adaln_kernel(cc_ref, w_ref, b_ref, o_ref):
    s = _silu(cc_ref[...])
    o_ref[0] = jnp.dot(s, w_ref[0], preferred_element_type=F32,
                       precision=lax.Precision.HIGHEST) + b_ref[0]


def adaln_all(cc, ada_w, ada_b):
    depth, d, n3 = ada_w.shape
    tn = 512
    return pl.pallas_call(
        _adaln_kernel,
        grid=(depth, n3 // tn),
        in_specs=[pl.BlockSpec((8, d), lambda l, j: (0, 0)),
                  pl.BlockSpec((1, d, tn), lambda l, j: (l, 0, j)),
                  pl.BlockSpec((1, 1, tn), lambda l, j: (l, 0, j))],
        out_specs=pl.BlockSpec((1, 8, tn), lambda l, j: (l, 0, j)),
        out_shape=jax.ShapeDtypeStruct((depth, 8, n3), F32),
        compiler_params=_cparams("parallel", "parallel"),
        name="adaln",
    )(cc, ada_w, ada_b.reshape(depth, 1, n3))


def _inproj_kernel(x_ref, shl_ref, scl_ref, shc_ref, scc_ref, g_ref, w_ref, o_ref, h_scr, *, tm, ctx_len):
    i = pl.program_id(1)

    @pl.when(pl.program_id(2) == 0)
    def _():
        x = x_ref[0]
        y = x * lax.rsqrt(jnp.mean(x * x, axis=-1, keepdims=True) + EPS) * g_ref[...]
        row = i * tm + lax.broadcasted_iota(jnp.int32, (tm, 1), 0)
        is_ctx = row < ctx_len
        sc = jnp.where(is_ctx, scc_ref[...], scl_ref[0])
        sh = jnp.where(is_ctx, shc_ref[...], shl_ref[0])
        h_scr[...] = (y * (1.0 + sc) + sh).astype(BF16)

    o_ref[0, 0] = jnp.dot(h_scr[...], w_ref[...], preferred_element_type=F32)


def inproj(x_all, shift_l, scale_l, shift_c, scale_c, g, w, ctx_len, tn, tm):
    b, t, d = x_all.shape
    n = w.shape[1]
    return pl.pallas_call(
        functools.partial(_inproj_kernel, tm=tm, ctx_len=ctx_len),
        grid=(b, t // tm, n // tn),
        in_specs=[pl.BlockSpec((1, tm, d), lambda bb, i, j: (bb, i, 0)),
                  pl.BlockSpec((1, 1, d), lambda bb, i, j: (bb, 0, 0)),
                  pl.BlockSpec((1, 1, d), lambda bb, i, j: (bb, 0, 0)),
                  pl.BlockSpec((1, d), lambda bb, i, j: (0, 0)),
                  pl.BlockSpec((1, d), lambda bb, i, j: (0, 0)),
                  pl.BlockSpec((1, d), lambda bb, i, j: (0, 0)),
                  pl.BlockSpec((d, tn), lambda bb, i, j: (0, j))],
        out_specs=pl.BlockSpec((1, 1, tm, tn), lambda bb, i, j: (j, bb, i, 0)),
        out_shape=jax.ShapeDtypeStruct((n // tn, b, t, tn), F32),
        scratch_shapes=[pltpu.VMEM((tm, d), BF16)],
        compiler_params=_cparams("parallel", "parallel", "arbitrary"),
        name="inproj",
    )(x_all, shift_l, scale_l, shift_c, scale_c, g, w)


def _outproj_kernel(ma_ref, mb_ref, w1_ref, w2_ref, x_ref, gl_ref, gc_ref, o_ref, *, ctx_tiles):
    i = pl.program_id(1)
    acc = jnp.dot(ma_ref[0], w1_ref[...], preferred_element_type=F32)
    acc = acc + jnp.dot(mb_ref[0], w2_ref[...], preferred_element_type=F32)
    gate = jnp.where(i < ctx_tiles, gc_ref[...], gl_ref[0])
    o_ref[0] = x_ref[0] + gate * acc


def outproj(ma, mb, w1, w2, x_all, gate_l, gate_c, ctx_len, latent_only):
    b, t, d = x_all.shape
    k1, k2 = ma.shape[2], mb.shape[2]
    tm = ctx_len
    off = 1 if latent_only else 0
    t_out = t - ctx_len if latent_only else t
    return pl.pallas_call(
        functools.partial(_outproj_kernel, ctx_tiles=0 if latent_only else 1),
        grid=(b, t_out // tm),
        in_specs=[pl.BlockSpec((1, tm, k1), lambda bb, i: (bb, i + off, 0)),
                  pl.BlockSpec((1, tm, k2), lambda bb, i: (bb, i + off, 0)),
                  pl.BlockSpec((k1, d), lambda bb, i: (0, 0)),
                  pl.BlockSpec((k2, d), lambda bb, i: (0, 0)),
                  pl.BlockSpec((1, tm, d), lambda bb, i: (bb, i + off, 0)),
                  pl.BlockSpec((1, 1, d), lambda bb, i: (bb, 0, 0)),
                  pl.BlockSpec((1, d), lambda bb, i: (0, 0))],
        out_specs=pl.BlockSpec((1, tm, d), lambda bb, i: (bb, i, 0)),
        out_shape=jax.ShapeDtypeStruct((b, t_out, d), F32),
        compiler_params=_cparams("parallel", "parallel"),
        name="outproj",
    )(ma, mb, w1, w2, x_all, gate_l, gate_c)


def _qkprep_kernel(x_ref, g_ref, cos_ref, sin_ref, o_ref):
    x = x_ref[0, 0]
    y = x * lax.rsqrt(jnp.mean(x * x, axis=-1, keepdims=True) + EPS) * g_ref[0]
    lane = lax.broadcasted_iota(jnp.int32, y.shape, 1)
    partner = jnp.where(lane % 2 == 0, pltpu.roll(y, HEAD_DIM - 1, 1), pltpu.roll(y, 1, 1))
    o_ref[0] = (y * cos_ref[...] + partner * sin_ref[...]).astype(BF16)


def qkprep(slabs, gains, cos_t, sin_t):
    _, b, t, tn = slabs.shape
    per = tn // HEAD_DIM
    nh = A_KV_HEADS + A_Q_HEADS

    def x_map(bb, hh):
        qh = hh - A_KV_HEADS
        slab = jnp.where(hh < A_KV_HEADS, 0, 2 + qh // per)
        col = jnp.where(hh < A_KV_HEADS, hh, qh % per)
        return (slab, bb, 0, col)

    return pl.pallas_call(
        _qkprep_kernel,
        grid=(b, nh),
        in_specs=[pl.BlockSpec((1, 1, t, HEAD_DIM), x_map),
                  pl.BlockSpec((1, 1, HEAD_DIM), lambda bb, hh: (hh, 0, 0)),
                  pl.BlockSpec((t, HEAD_DIM), lambda bb, hh: (0, 0)),
                  pl.BlockSpec((t, HEAD_DIM), lambda bb, hh: (0, 0))],
        out_specs=pl.BlockSpec((1, t, HEAD_DIM), lambda bb, hh: (bb, 0, hh)),
        out_shape=jax.ShapeDtypeStruct((b, t, nh * HEAD_DIM), BF16),
        compiler_params=_cparams("parallel", "parallel"),
        name="qkprep",
    )(slabs, gains, cos_t, sin_t)


def _attn_kernel(q_ref, k_ref, v_ref, ga_ref, o_ref, *, ctx_len):
    iq = pl.program_id(3)
    q = q_ref[0]

    def attend(k, v):
        s = lax.dot_general(q, k, (((1,), (1,)), ((), ())), preferred_element_type=F32)
        m = jnp.max(s, axis=-1, keepdims=True)
        p = jnp.exp(s - m)
        l = jnp.sum(p, axis=-1, keepdims=True)
        o = jnp.dot(p.astype(BF16), v.astype(BF16), preferred_element_type=F32) / l
        o_ref[0] = (o * _silu(ga_ref[0, 0])).astype(BF16)

    @pl.when(iq == 0)
    def _():
        attend(k_ref[0, :ctx_len], v_ref[0, 0, :ctx_len])

    @pl.when(iq > 0)
    def _():
        attend(k_ref[0], v_ref[0, 0])


def attention(qk, slabs, ctx_len):
    b, t, _ = qk.shape
    tn = slabs.shape[3]
    per = tn // HEAD_DIM
    tq = ctx_len

    def ga_map(bb, kv, g, iq):
        h = kv * A_GROUP + g
        return (7 + h // per, bb, iq, h % per)

    return pl.pallas_call(
        functools.partial(_attn_kernel, ctx_len=ctx_len),
        grid=(b, A_KV_HEADS, A_GROUP, t // tq),
        in_specs=[pl.BlockSpec((1, tq, HEAD_DIM), lambda bb, kv, g, iq: (bb, iq, A_KV_HEADS + kv * A_GROUP + g)),
                  pl.BlockSpec((1, t, HEAD_DIM), lambda bb, kv, g, iq: (bb, 0, kv)),
                  pl.BlockSpec((1, 1, t, HEAD_DIM), lambda bb, kv, g, iq: (1, bb, 0, kv)),
                  pl.BlockSpec((1, 1, tq, HEAD_DIM), ga_map)],
        out_specs=pl.BlockSpec((1, tq, HEAD_DIM), lambda bb, kv, g, iq: (bb, iq, kv * A_GROUP + g)),
        out_shape=jax.ShapeDtypeStruct((b, t, A_Q_HEADS * HEAD_DIM), BF16),
        compiler_params=_cparams("parallel", "parallel", "parallel", "arbitrary"),
        name="attention",
    )(qk, qk, slabs, slabs)


def _sgu_kernel(u_ref, v_ref, gb_ref, ws_ref, bs_ref, g_ref, o_ref):
    outs = []
    for grp in range(B_GROUPS):
        cols = slice(grp * LANES, (grp + 1) * LANES)
        v = v_ref[0, 0, :, cols]
        vn = v * lax.rsqrt(jnp.mean(v * v, axis=-1, keepdims=True) + EPS) * g_ref[:, cols]
        mixed = jnp.dot(ws_ref[grp], vn.astype(BF16), preferred_element_type=F32) + bs_ref[grp]
        outs.append(u_ref[0, 0, :, cols] * mixed * _silu(gb_ref[0, 0, :, cols]))
    o_ref[0] = jnp.concatenate(outs, axis=1).astype(BF16)


def sgu(slabs, ws, bs, g):
    _, b, t, tn = slabs.shape
    spec = lambda s: pl.BlockSpec((1, 1, CHUNK, tn), lambda bb, c: (s, bb, c, 0))
    return pl.pallas_call(
        _sgu_kernel,
        grid=(b, t // CHUNK),
        in_specs=[spec(5), spec(6), spec(10),
                  pl.BlockSpec((B_GROUPS, CHUNK, CHUNK), lambda bb, c: (0, 0, 0)),
                  pl.BlockSpec((B_GROUPS, CHUNK, 1), lambda bb, c: (0, 0, 0)),
                  pl.BlockSpec((1, tn), lambda bb, c: (0, 0))],
        out_specs=pl.BlockSpec((1, CHUNK, tn), lambda bb, c: (bb, c, 0)),
        out_shape=jax.ShapeDtypeStruct((b, t, tn), BF16),
        compiler_params=_cparams("parallel", "parallel"),
        name="sgu",
    )(slabs, slabs, slabs, ws, bs, g)


def _s5_piece_cols(ref, piece, width):
    return jnp.concatenate(
        [ref[0, :, pl.ds(t * width + piece * LANES, LANES)] for t in range(SUB)], axis=1)


def _s5_in_kernel(u_ref, w_ref, o_ref, *, n_piece, width):
    i = pl.program_id(1)
    for piece in range(n_piece):
        @pl.when(i == piece)
        def _(piece=piece):
            x = _s5_piece_cols(u_ref, piece, width).astype(BF16)
            o_ref[...] = jnp.dot(x, w_ref[0], preferred_element_type=F32)


def s5_in(u2, w_in, tm):
    _, m, wide = u2.shape
    n_piece = w_in.shape[0]
    ncol = w_in.shape[2]
    return pl.pallas_call(
        functools.partial(_s5_in_kernel, n_piece=n_piece, width=wide // SUB),
        grid=(m // tm, n_piece),
        in_specs=[pl.BlockSpec((1, tm, wide), lambda r, i: (0, r, 0)),
                  pl.BlockSpec((1, w_in.shape[1], ncol), lambda r, i: (i, 0, 0))],
        out_specs=pl.BlockSpec((tm, ncol), lambda r, i: (r, i)),
        out_shape=jax.ShapeDtypeStruct((m, n_piece * ncol), F32),
        compiler_params=_cparams("parallel", "arbitrary"),
        name="s5_in",
    )(u2, w_in)


def _s5_scan_kernel(g_ref, c_ref, o_ref, *, ctx_pairs, all_pairs, half):
    rows = 2 * SUBLANES
    rowi = lax.broadcasted_iota(jnp.int32, (SUBLANES, half), 0)

    def tile(d, g_re, g_im, c_re, c_im):
        bwd = d == 1
        b_re, b_im = g_re, g_im
        for idx, dd in enumerate((1, 2, 4)):
            a_re, a_im = c_ref[d, idx, 0], c_ref[d, idx, 1]
            sh = SUBLANES - dd if bwd else dd
            s_re, s_im = pltpu.roll(b_re, sh, 0), pltpu.roll(b_im, sh, 0)
            b_re, b_im = (b_re + a_re * s_re - a_im * s_im,
                          b_im + a_re * s_im + a_im * s_re)
        a_re, a_im = c_ref[d, 3, 0], c_ref[d, 3, 1]
        h_re = b_re + a_re * c_re - a_im * c_im
        h_im = b_im + a_re * c_im + a_im * c_re
        if bwd:
            p_re = jnp.where(rowi == SUBLANES - 1, c_re, pltpu.roll(h_re, SUBLANES - 1, 0))
            p_im = jnp.where(rowi == SUBLANES - 1, c_im, pltpu.roll(h_im, SUBLANES - 1, 0))
            return p_re, p_im, h_re[0:1], h_im[0:1]
        p_re = jnp.where(rowi == 0, c_re, pltpu.roll(h_re, 1, 0))
        p_im = jnp.where(rowi == 0, c_im, pltpu.roll(h_im, 1, 0))
        return p_re, p_im, h_re[SUBLANES - 1:SUBLANES], h_im[SUBLANES - 1:SUBLANES]

    def pair(d, pi, carry):
        c_re, c_im = carry
        r0 = pl.multiple_of(pi * rows, rows)
        base = d * 2 * half
        order = (1, 0) if d == 1 else (0, 1)
        res = [None, None]
        for which in order:
            rr = pl.ds(r0 + which * SUBLANES, SUBLANES)
            g_re = g_ref[0, rr, pl.ds(base, half)]
            g_im = g_ref[0, rr, pl.ds(base + half, half)]
            p_re, p_im, c_re, c_im = tile(d, g_re, g_im, c_re, c_im)
            res[which] = (p_re, p_im)
        o_ref[0, pl.ds(r0, rows), pl.ds(base, half)] = jnp.concatenate(
            [res[0][0], res[1][0]], axis=0).astype(BF16)
        o_ref[0, pl.ds(r0, rows), pl.ds(base + half, half)] = jnp.concatenate(
            [res[0][1], res[1][1]], axis=0).astype(BF16)
        return c_re, c_im

    zero = (jnp.zeros((1, half), F32), jnp.zeros((1, half), F32))
    lax.fori_loop(0, all_pairs, lambda pi, c: pair(0, pi, c), zero)
    c = lax.fori_loop(0, ctx_pairs, lambda n, c: pair(1, ctx_pairs - 1 - n, c), zero)
    lax.fori_loop(0, all_pairs - ctx_pairs, lambda n, c: pair(1, all_pairs - 1 - n, c), c)


def s5_scan(g3, consts, ctx_rows):
    b, rows, wide = g3.shape
    half = PIECE_GROUPS * C_STATE
    n_piece = wide // (4 * half)
    return pl.pallas_call(
        functools.partial(_s5_scan_kernel, ctx_pairs=ctx_rows // (2 * SUBLANES),
                          all_pairs=rows // (2 * SUBLANES), half=half),
        grid=(b, n_piece),
        in_specs=[pl.BlockSpec((1, rows, 4 * half), lambda bb, i: (bb, 0, i)),
                  pl.BlockSpec((2, 4, 2, SUBLANES, half), lambda bb, i: (0, 0, 0, 0, i))],
        out_specs=pl.BlockSpec((1, rows, 4 * half), lambda bb, i: (bb, 0, i)),
        out_shape=jax.ShapeDtypeStruct((b, rows, wide), BF16),
        compiler_params=_cparams("parallel", "parallel"),
        name="s5_scan",
    )(g3, consts)


def _s5_out_kernel(u_ref, hp_ref, t_ref, w_ref, y_ref, *, n_piece, width):
    i = pl.program_id(1)
    for piece in range(n_piece):
        @pl.when(i == piece)
        def _(piece=piece):
            x = _s5_piece_cols(u_ref, piece, width).astype(BF16)
            y = jnp.dot(x, t_ref[0], preferred_element_type=F32)
            y = y + jnp.dot(hp_ref[...], w_ref[0], preferred_element_type=F32)
            for t in range(SUB):
                y_ref[:, pl.ds(t * width + piece * LANES, LANES)] = y[:, t * LANES:(t + 1) * LANES]


def s5_out(u2, hp2, t_intra, w_out, tm):
    _, m, wide = u2.shape
    n_piece = t_intra.shape[0]
    kh = w_out.shape[1]
    return pl.pallas_call(
        functools.partial(_s5_out_kernel, n_piece=n_piece, width=wide // SUB),
        grid=(m // tm, n_piece),
        in_specs=[pl.BlockSpec((1, tm, wide), lambda r, i: (0, r, 0)),
                  pl.BlockSpec((tm, kh), lambda r, i: (r, i)),
                  pl.BlockSpec((1,) + t_intra.shape[1:], lambda r, i: (i, 0, 0)),
                  pl.BlockSpec((1,) + w_out.shape[1:], lambda r, i: (i, 0, 0))],
        out_specs=pl.BlockSpec((tm, wide), lambda r, i: (r, 0)),
        out_shape=jax.ShapeDtypeStruct((m, wide), F32),
        compiler_params=_cparams("parallel", "arbitrary"),
        name="s5_out",
    )(u2, hp2, t_intra, w_out)


def _s5_finish_kernel(y_ref, u_ref, gc_ref, d_ref, w_ref, b_ref, o_ref):
    y = y_ref[...] + d_ref[...] * u_ref[0]
    y = jax.nn.gelu(y)
    z = jnp.dot(y.astype(BF16), w_ref[...], preferred_element_type=F32) + b_ref[...]
    o_ref[...] = (y * jax.nn.sigmoid(z) * _silu(gc_ref[0])).astype(BF16)


def s5_finish(y, slabs2, d_skip, glu_w, glu_b, tm):
    r, w = y.shape
    return pl.pallas_call(
        _s5_finish_kernel,
        grid=(r // tm,),
        in_specs=[pl.BlockSpec((tm, w), lambda i: (i, 0)),
                  pl.BlockSpec((1, tm, w), lambda i: (0, i, 0)),
                  pl.BlockSpec((1, tm, w), lambda i: (2, i, 0)),
                  pl.BlockSpec((1, w), lambda i: (0, 0)),
                  pl.BlockSpec((w, w), lambda i: (0, 0)),
                  pl.BlockSpec((1, w), lambda i: (0, 0))],
        out_specs=pl.BlockSpec((tm, w), lambda i: (i, 0)),
        out_shape=jax.ShapeDtypeStruct((r, w), BF16),
        compiler_params=_cparams("parallel"),
        name="s5_finish",
    )(y, slabs2, slabs2, d_skip, glu_w, glu_b)


def _expand_block_diag(src, unit, row_unit):
    rows, cols = src.shape
    wide = cols * PIECE_GROUPS
    lg = lambda v: int(math.log2(v))
    k = lax.broadcasted_iota(jnp.int32, (cols, wide), 0)
    j = lax.broadcasted_iota(jnp.int32, (cols, wide), 1)
    src_col = ((j >> lg(PIECE_GROUPS * unit)) << lg(unit)) + (j & (unit - 1))
    spread = jnp.where(k == src_col, 1.0, 0.0).astype(BF16)
    out = jnp.dot(src.astype(BF16), spread, preferred_element_type=F32)
    rg = (lax.broadcasted_iota(jnp.int32, (rows, wide), 0) >> lg(row_unit)) & (PIECE_GROUPS - 1)
    ch = (lax.broadcasted_iota(jnp.int32, (rows, wide), 1) >> lg(unit)) & (PIECE_GROUPS - 1)
    return jnp.where(rg == ch, out, 0.0).astype(BF16)


def _s5_expand_kernel(ts_ref, wi_ref, wo_ref, t_ref, win_ref, wout_ref, *, p, n):
    t_ref[0] = _expand_block_diag(ts_ref[0], p, p)
    win_ref[0] = _expand_block_diag(wi_ref[0], n, p)
    wout_ref[0] = _expand_block_diag(wo_ref[0], p, n)


def s5_expand(t_src, win_src, wout_src, p, n):
    npiece = t_src.shape[0]
    spec = lambda a, mult: pl.BlockSpec((1, a.shape[1], a.shape[2] * mult), lambda i: (i, 0, 0))
    shape = lambda a: jax.ShapeDtypeStruct((npiece, a.shape[1], a.shape[2] * PIECE_GROUPS), BF16)
    return pl.pallas_call(
        functools.partial(_s5_expand_kernel, p=p, n=n),
        grid=(npiece,),
        in_specs=[spec(t_src, 1), spec(win_src, 1), spec(wout_src, 1)],
        out_specs=[spec(t_src, PIECE_GROUPS), spec(win_src, PIECE_GROUPS), spec(wout_src, PIECE_GROUPS)],
        out_shape=[shape(t_src), shape(win_src), shape(wout_src)],
        compiler_params=_cparams("parallel"),
        name="s5_expand",
    )(t_src, win_src, wout_src)


def s5_derive(lam_re, lam_im, log_dt, b_re, b_im, c_re, c_im):
    hi = lax.Precision.HIGHEST
    n_dir, g, n = lam_re.shape
    p = b_re.shape[-1]
    npiece = g // PIECE_GROUPS
    dt = jnp.exp(log_dt)[None, :, :, None]

    def powers(ks):
        k = jnp.asarray(list(ks), F32).reshape(-1, 1, 1, 1)
        mag = jnp.exp(k * lam_re[None] * dt)
        ang = k * lam_im[None] * dt
        return mag * jnp.cos(ang), mag * jnp.sin(ang)

    pw_re, pw_im = powers(range(SUB + 1))
    den = lam_re * lam_re + lam_im * lam_im
    nr, ni = pw_re[1] - 1.0, pw_im[1]
    cf_re = (nr * lam_re + ni * lam_im) / den
    cf_im = (ni * lam_re - nr * lam_im) / den
    bb_re = cf_re[..., None] * b_re - cf_im[..., None] * b_im
    bb_im = cf_re[..., None] * b_im + cf_im[..., None] * b_re
    ca_re = c_re[None] * pw_re[:, :, :, None, :] - c_im[None] * pw_im[:, :, :, None, :]
    ca_im = c_re[None] * pw_im[:, :, :, None, :] + c_im[None] * pw_re[:, :, :, None, :]

    kern = (jnp.einsum('tdgpn,dgnq->tdgpq', ca_re[:SUB], bb_re, precision=hi)
            - jnp.einsum('tdgpn,dgnq->tdgpq', ca_im[:SUB], bb_im, precision=hi))
    s_idx = jnp.arange(SUB)[:, None]
    t_idx = jnp.arange(SUB)[None, :]
    lag = t_idx - s_idx
    kf = kern[jnp.clip(lag, 0, SUB - 1), 0] * (lag >= 0)[:, :, None, None, None].astype(F32)
    kb = kern[jnp.clip(-lag, 0, SUB - 1), 1] * (lag <= 0)[:, :, None, None, None].astype(F32)
    kc = (kf + kb).reshape(SUB, SUB, npiece, PIECE_GROUPS, p, p)
    t_src = kc.transpose(2, 0, 3, 5, 1, 4).reshape(npiece, SUB * PIECE_GROUPS * p, SUB * p)

    def cmul(xr, xi, yr, yi):
        return xr * yr - xi * yi, xr * yi + xi * yr

    dn_re, dn_im = powers(range(SUB - 1, -1, -1))
    f_re, f_im = cmul(dn_re[:, 0][:, :, :, None], dn_im[:, 0][:, :, :, None],
                      bb_re[0][None], bb_im[0][None])
    r_re, r_im = cmul(pw_re[:SUB, 1][:, :, :, None], pw_im[:SUB, 1][:, :, :, None],
                      bb_re[1][None], bb_im[1][None])
    v = jnp.stack([jnp.stack([f_re, f_im]), jnp.stack([r_re, r_im])])
    v = v.reshape(n_dir, 2, SUB, npiece, PIECE_GROUPS, n, p)
    win_src = v.transpose(3, 2, 4, 6, 0, 1, 5).reshape(npiece, SUB * PIECE_GROUPS * p, n_dir * 2 * n)

    o_f = jnp.stack([ca_re[1:SUB + 1, 0], -ca_im[1:SUB + 1, 0]])
    up_re, up_im = powers(range(SUB, 0, -1))
    cb_re = c_re[1][None] * up_re[:, 1, :, None, :] - c_im[1][None] * up_im[:, 1, :, None, :]
    cb_im = c_re[1][None] * up_im[:, 1, :, None, :] + c_im[1][None] * up_re[:, 1, :, None, :]
    o_b = jnp.stack([cb_re, -cb_im])
    o = jnp.stack([o_f, o_b]).reshape(n_dir, 2, SUB, npiece, PIECE_GROUPS, p, n)
    wout_src = o.transpose(3, 0, 1, 4, 6, 2, 5).reshape(npiece, n_dir * 2 * PIECE_GROUPS * n, SUB * p)

    q_re, q_im = powers([SUB * m for m in range(1, SUBLANES + 1)])
    q_re = q_re.reshape(SUBLANES, n_dir, g * n)
    q_im = q_im.reshape(SUBLANES, n_dir, g * n)
    row = jnp.arange(SUBLANES)
    consts = []
    for d in range(n_dir):
        kinds = []
        for dd in (1, 2, 4):
            keep = (row + dd <= SUBLANES - 1) if d == 1 else (row >= dd)
            m = keep.astype(F32)[:, None]
            kinds.append(jnp.stack([m * q_re[dd - 1, d][None], m * q_im[dd - 1, d][None]]))
        sel = (SUBLANES - 1 - row) if d == 1 else row
        kinds.append(jnp.stack([q_re[sel, d], q_im[sel, d]]))
        consts.append(jnp.stack(kinds))
    consts = jnp.stack(consts)
    t_intra, w_in, w_out = s5_expand(t_src, win_src, wout_src, p, n)
    return t_intra, w_in, w_out, consts


def _lru_kernel(*refs, bwd, nch, tc, width):
    if bwd:
        (x_ref, xp_ref, xn_ref, cw_ref, cb_ref, lam_ref, wa_ref, ba_ref, wx_ref, bx_ref,
         hf_ref, gd_ref, o_ref, pad_scr, a_scr, b_scr, carry_scr) = refs
    else:
        (x_ref, xp_ref, xn_ref, cw_ref, cb_ref, lam_ref, wa_ref, ba_ref, wx_ref, bx_ref,
         o_ref, pad_scr, a_scr, b_scr, carry_scr) = refs
    k = pl.program_id(1)
    chunk = jnp.where(k == 0, 0, nch - k) if bwd else k
    prev_ok = jnp.logical_and(chunk != 0, chunk != 1)
    next_ok = jnp.logical_and(chunk != 0, chunk != nch - 1)
    pad_scr[0:SUBLANES] = jnp.where(prev_ok, xp_ref[0, 0], 0.0)
    pad_scr[SUBLANES:SUBLANES + tc] = x_ref[0, 0]
    pad_scr[SUBLANES + tc:2 * SUBLANES + tc] = jnp.where(next_ok, xn_ref[0, 0], 0.0)
    xc = cb_ref[...]
    for tap in range(CONV_W):
        xc = xc + cw_ref[tap:tap + 1] * pad_scr[SUBLANES - 1 + tap:SUBLANES - 1 + tap + tc]

    bw = width // D_BLOCKS
    r_parts, i_parts = [], []
    for blk in range(D_BLOCKS):
        xb = xc[:, blk * bw:(blk + 1) * bw].astype(BF16)
        r_parts.append(jnp.dot(xb, wa_ref[blk], preferred_element_type=F32))
        i_parts.append(jnp.dot(xb, wx_ref[blk], preferred_element_type=F32))
    r = jax.nn.sigmoid(jnp.concatenate(r_parts, axis=1) + ba_ref[...])
    ig = jax.nn.sigmoid(jnp.concatenate(i_parts, axis=1) + bx_ref[...])
    nl = -lam_ref[...]
    e = jnp.exp(-jnp.abs(nl))
    e1 = 1.0 + e
    log1p_e = jnp.where(e1 == 1.0, e, jnp.log(e1) * (e / jnp.where(e1 == 1.0, 1.0, e1 - 1.0)))
    softplus = jnp.maximum(nl, 0.0) + log1p_e
    log_a = (-LRU_C * softplus) * r
    a_scr[...] = jnp.exp(log_a)
    th = jnp.tanh(log_a)
    b_scr[...] = jnp.sqrt(-2.0 * th / (1.0 - th)) * (ig * xc)

    @pl.when(k == 0)
    def _():
        carry_scr[...] = jnp.zeros_like(carry_scr)

    rowi = lax.broadcasted_iota(jnp.int32, (SUBLANES, width), 0)
    n_tiles = tc // SUBLANES

    def body(n, carry):
        ti = n_tiles - 1 - n if bwd else n
        rr = pl.ds(pl.multiple_of(ti * SUBLANES, SUBLANES), SUBLANES)
        a, b = a_scr[rr], b_scr[rr]
        for dd in (1, 2, 4):
            keep = (rowi + dd <= SUBLANES - 1) if bwd else (rowi >= dd)
            sh = SUBLANES - dd if bwd else dd
            b = b + a * jnp.where(keep, pltpu.roll(b, sh, 0), 0.0)
            a = a * jnp.where(keep, pltpu.roll(a, sh, 0), 1.0)
        h = b + a * carry
        b_scr[rr] = h
        return h[0:1] if bwd else h[SUBLANES - 1:SUBLANES]

    carry_scr[...] = lax.fori_loop(0, n_tiles, body, carry_scr[...])
    if bwd:
        o_ref[0] = ((hf_ref[0] + b_scr[...]) * _silu(gd_ref[0, 0])).astype(BF16)
    else:
        o_ref[0] = b_scr[...]


def lru_pass(slabs, conv_w, conv_b, lam, wa, ba, wx, bx, ctx_len, hf=None):
    _, b, t, w = slabs.shape
    bwd = hf is not None
    tc = ctx_len
    nch = t // tc
    hb = tc // SUBLANES
    nhb = t // SUBLANES

    def ch(k):
        return jnp.where(k == 0, 0, nch - k) if bwd else k

    in_specs = [pl.BlockSpec((1, 1, tc, w), lambda bb, k: (1, bb, ch(k), 0)),
                pl.BlockSpec((1, 1, SUBLANES, w), lambda bb, k: (1, bb, jnp.maximum(ch(k) * hb - 1, 0), 0)),
                pl.BlockSpec((1, 1, SUBLANES, w), lambda bb, k: (1, bb, jnp.minimum((ch(k) + 1) * hb, nhb - 1), 0)),
                pl.BlockSpec((CONV_W, w), lambda bb, k: (0, 0)),
                pl.BlockSpec((1, w), lambda bb, k: (0, 0)),
                pl.BlockSpec((1, w), lambda bb, k: (0, 0)),
                pl.BlockSpec(wa.shape, lambda bb, k: (0, 0, 0)),
                pl.BlockSpec((1, w), lambda bb, k: (0, 0)),
                pl.BlockSpec(wx.shape, lambda bb, k: (0, 0, 0)),
                pl.BlockSpec((1, w), lambda bb, k: (0, 0))]
    args = [slabs, slabs, slabs, conv_w, conv_b, lam, wa, ba, wx, bx]
    if bwd:
        in_specs += [pl.BlockSpec((1, tc, w), lambda bb, k: (bb, ch(k), 0)),
                     pl.BlockSpec((1, 1, tc, w), lambda bb, k: (3, bb, ch(k), 0))]
        args += [hf, slabs]
    return pl.pallas_call(
        functools.partial(_lru_kernel, bwd=bwd, nch=nch, tc=tc, width=w),
        grid=(b, nch),
        in_specs=in_specs,
        out_specs=pl.BlockSpec((1, tc, w), lambda bb, k: (bb, ch(k), 0)),
        out_shape=jax.ShapeDtypeStruct((b, t, w), BF16 if bwd else F32),
        scratch_shapes=[pltpu.VMEM((tc + 2 * SUBLANES, w), F32),
                        pltpu.VMEM((tc, w), F32),
                        pltpu.VMEM((tc, w), F32),
                        pltpu.VMEM((1, w), F32)],
        compiler_params=_cparams("parallel", "arbitrary"),
        name="lru_bwd" if bwd else "lru_fwd",
    )(*args)


def rope_tables(ctx_len, seq):
    rows = seq // GRID_W
    axis_dim = HEAD_DIM // 2
    row = jnp.repeat(jnp.arange(rows), GRID_W).astype(F32)
    col = jnp.tile(jnp.arange(GRID_W), rows).astype(F32)
    inv = ROPE_THETA ** (-jnp.arange(0, axis_dim, 2, dtype=F32) / axis_dim)
    ang = jnp.concatenate([row[:, None] * inv, col[:, None] * inv], axis=-1)
    cos, sin = jnp.cos(ang), jnp.sin(ang)
    cos_t = jnp.repeat(cos, 2, axis=-1)
    sin_t = jnp.stack([-sin, sin], axis=-1).reshape(seq, HEAD_DIM)
    cos_t = jnp.concatenate([jnp.ones((ctx_len, HEAD_DIM), F32), cos_t], axis=0)
    sin_t = jnp.concatenate([jnp.zeros((ctx_len, HEAD_DIM), F32), sin_t], axis=0)
    return cos_t, sin_t


def even_layer(x_all, mod, g, w_in, w_out, q_g, k_g, sgu_g, ws, bs, cos_t, sin_t, ctx_len, latent_only):
    shift_l, scale_l, gate_l, shift_c, scale_c, gate_c = mod
    slabs = inproj(x_all, shift_l, scale_l, shift_c, scale_c, g, w_in.astype(BF16), ctx_len, tn=512,
                   tm=_pick_tile(x_all.shape[1], 768))
    gains = jnp.concatenate([jnp.broadcast_to(k_g, (A_KV_HEADS, HEAD_DIM)),
                             jnp.broadcast_to(q_g * HEAD_DIM ** -0.5, (A_Q_HEADS, HEAD_DIM))])[:, None, :]
    qk = qkprep(slabs, gains, cos_t, sin_t)
    mix_a = attention(qk, slabs, ctx_len)
    mix_b = sgu(slabs, ws.astype(BF16), bs[:, :, None], sgu_g[None, :])
    ka = mix_a.shape[2]
    w_out = w_out.astype(BF16)
    return outproj(mix_a, mix_b, w_out[:ka], w_out[ka:], x_all, gate_l, gate_c, ctx_len, latent_only)


def odd_layer(x_all, mod, g, w_in, w_out, s5p, d_skip, glu_w, glu_b, conv_w, conv_b, lam, wa, ba, wx, bx,
              ctx_len, latent_only):
    shift_l, scale_l, gate_l, shift_c, scale_c, gate_c = mod
    b, t, _ = x_all.shape
    slabs = inproj(x_all, shift_l, scale_l, shift_c, scale_c, g, w_in.astype(BF16), ctx_len, tn=1024,
                   tm=_pick_tile(t, 768))
    cw = slabs.shape[3]
    rows = b * t // SUB
    t_intra, w_s5in, w_s5out, consts = s5_derive(*s5p)
    u2 = slabs.reshape(slabs.shape[0], rows, SUB * cw)
    gmat = s5_in(u2, w_s5in, tm=_pick_tile(rows, 288))
    hp = s5_scan(gmat.reshape(b, t // SUB, -1), consts, ctx_len // SUB)
    y = s5_out(u2, hp.reshape(rows, -1), t_intra, w_s5out, tm=_pick_tile(rows, 144))
    slabs2 = slabs.reshape(slabs.shape[0], b * t, cw)
    mix_c = s5_finish(y.reshape(b * t, cw), slabs2, d_skip[None, :], glu_w.astype(BF16), glu_b[None, :],
                      tm=_pick_tile(b * t, 512))
    hf = lru_pass(slabs, conv_w, conv_b[None, :], lam[0][None, :], wa[0].astype(BF16), ba[0][None, :],
                  wx[0].astype(BF16), bx[0][None, :], ctx_len)
    mix_d = lru_pass(slabs, conv_w, conv_b[None, :], lam[1][None, :], wa[1].astype(BF16), ba[1][None, :],
                     wx[1].astype(BF16), bx[1][None, :], ctx_len, hf=hf)
    w_out = w_out.astype(BF16)
    return outproj(mix_c.reshape(b, t, cw), mix_d, w_out[:cw], w_out[cw:], x_all, gate_l, gate_c,
                   ctx_len, latent_only)


def kernel(x, c, ctx, c_ctx, ada_w, ada_b, norm_g, ev_w_in, ev_w_out, ev_q_g, ev_k_g, ev_sgu_g, ev_ws, ev_bs,
           od_w_in, od_w_out, s5_lam_re, s5_lam_im, s5_log_dt, s5_b_re, s5_b_im, s5_c_re, s5_c_im, s5_d,
           s5_glu_w, s5_glu_b, lru_conv_w, lru_conv_b, lru_lam, lru_wa, lru_ba, lru_wx, lru_bx):
    b, seq, d = x.shape
    ctx_len = ctx.shape[1]
    depth = ada_w.shape[0]
    assert b + 1 <= SUBLANES and seq % ctx_len == 0
    cos_t, sin_t = rope_tables(ctx_len, seq)
    cc = jnp.zeros((SUBLANES, d), F32).at[:b].set(c).at[b].set(c_ctx)
    mods = adaln_all(cc, ada_w, ada_b)
    x_all = jnp.concatenate([ctx, x], axis=1)
    for layer in range(depth):
        m = mods[layer]
        mod = tuple(m[:b, None, k * d:(k + 1) * d] for k in range(3)) + \
            tuple(m[b:b + 1, k * d:(k + 1) * d] for k in range(3))
        last = layer == depth - 1
        j = layer // 2
        g = norm_g[layer][None, :]
        if layer % 2 == 0:
            x_all = even_layer(x_all, mod, g, ev_w_in[j], ev_w_out[j], ev_q_g[j], ev_k_g[j], ev_sgu_g[j],
                               ev_ws[j], ev_bs[j], cos_t, sin_t, ctx_len, last)
        else:
            s5p = (s5_lam_re[j], s5_lam_im[j], s5_log_dt[j], s5_b_re[j], s5_b_im[j], s5_c_re[j], s5_c_im[j])
            x_all = odd_layer(x_all, mod, g, od_w_in[j], od_w_out[j], s5p, s5_d[j], s5_glu_w[j], s5_glu_b[j],
                              lru_conv_w[j], lru_conv_b[j], lru_lam[j], lru_wa[j], lru_ba[j], lru_wx[j],
                              lru_bx[j], ctx_len, last)
    return x_all
```

```python
import functools
import math

import jax
import jax.numpy as jnp
from jax import lax
from jax.experimental import pallas as pl
from jax.experimental.pallas import tpu as pltpu

F32 = jnp.float32
BF16 = jnp.bfloat16

EPS = 1e-6
GRID_W = 64
HEAD_DIM = 128
A_Q_HEADS = 12
A_KV_HEADS = 4
A_GROUP = A_Q_HEADS // A_KV_HEADS
B_GROUPS = 4
CHUNK = 128
ROPE_THETA = 10000.0
C_GROUP_DIM = 16
C_STATE = 64
D_BLOCKS = 8
CONV_W = 4
LRU_C = 8.0

LANES = 128
SUBLANES = 8
SUB = 8
PIECE_GROUPS = LANES // C_GROUP_DIM
VMEM_LIMIT = 50 * 1024 * 1024


def _cparams(*sem):
    return pltpu.CompilerParams(dimension_semantics=sem, vmem_limit_bytes=VMEM_LIMIT)


def _silu(x):
    return x * jax.nn.sigmoid(x)


def _pick_tile(rows, target, align=16):
    best = None
    for cand in range(align, min(rows, target) + 1, align):
        if rows % cand == 0:
            best = cand
    assert best is not None, (rows, target)
    return best


def _adaln_kernel(cc_ref, w_ref, b_ref, o_ref):
    s = _silu(cc_ref[...])
    o_ref[0] = jnp.dot(s, w_ref[0], preferred_element_type=F32,
                       precision=lax.Precision.HIGHEST) + b_ref[0]


def adaln_all(cc, ada_w, ada_b):
    depth, d, n3 = ada_w.shape
    tn = 512
    return pl.pallas_call(
        _adaln_kernel,
        grid=(depth, n3 // tn),
        in_specs=[pl.BlockSpec((8, d), lambda l, j: (0, 0)),
                  pl.BlockSpec((1, d, tn), lambda l, j: (l, 0, j)),
                  pl.BlockSpec((1, 1, tn), lambda l, j: (l, 0, j))],
        out_specs=pl.BlockSpec((1, 8, tn), lambda l, j: (l, 0, j)),
        out_shape=jax.ShapeDtypeStruct((depth, 8, n3), F32),
        compiler_params=_cparams("parallel", "parallel"),
        name="adaln",
    )(cc, ada_w, ada_b.reshape(depth, 1, n3))


def _inproj_kernel(x_ref, shl_ref, scl_ref, shc_ref, scc_ref, g_ref, w_ref, o_ref, h_scr, *, tm, ctx_len):
    i = pl.program_id(1)
    first = pl.program_id(2) == 0

    def modulate(rows, sc_ref, sh_ref):
        x = x_ref[0, rows]
        r = lax.rsqrt(jnp.mean(x * x, axis=-1, keepdims=True) + EPS)
        h_scr[rows] = ((x * r) * (g_ref[...] * (1.0 + sc_ref[...])) + sh_ref[...]).astype(BF16)

    scl, shl = scl_ref.at[0], shl_ref.at[0]
    if tm >= ctx_len:
        @pl.when(jnp.logical_and(first, i == 0))
        def _():
            modulate(slice(0, ctx_len), scc_ref, shc_ref)
            if tm > ctx_len:
                modulate(slice(ctx_len, tm), scl, shl)

        @pl.when(jnp.logical_and(first, i > 0))
        def _():
            modulate(slice(0, tm), scl, shl)
    else:
        @pl.when(jnp.logical_and(first, i < ctx_len // tm))
        def _():
            modulate(slice(0, tm), scc_ref, shc_ref)

        @pl.when(jnp.logical_and(first, i >= ctx_len // tm))
        def _():
            modulate(slice(0, tm), scl, shl)

    o_ref[0, 0] = jnp.dot(h_scr[...], w_ref[...], preferred_element_type=F32)


def inproj(x_all, shift_l, scale_l, shift_c, scale_c, g, w, ctx_len, tn, tm):
    b, t, d = x_all.shape
    n = w.shape[1]
    assert tm % ctx_len == 0 or ctx_len % tm == 0
    return pl.pallas_call(
        functools.partial(_inproj_kernel, tm=tm, ctx_len=ctx_len),
        grid=(b, t // tm, n // tn),
        in_specs=[pl.BlockSpec((1, tm, d), lambda bb, i, j: (bb, i, 0)),
                  pl.BlockSpec((1, 1, d), lambda bb, i, j: (bb, 0, 0)),
                  pl.BlockSpec((1, 1, d), lambda bb, i, j: (bb, 0, 0)),
                  pl.BlockSpec((1, d), lambda bb, i, j: (0, 0)),
                  pl.BlockSpec((1, d), lambda bb, i, j: (0, 0)),
                  pl.BlockSpec((1, d), lambda bb, i, j: (0, 0)),
                  pl.BlockSpec((d, tn), lambda bb, i, j: (0, j))],
        out_specs=pl.BlockSpec((1, 1, tm, tn), lambda bb, i, j: (j, bb, i, 0)),
        out_shape=jax.ShapeDtypeStruct((n // tn, b, t, tn), F32),
        scratch_shapes=[pltpu.VMEM((tm, d), BF16)],
        compiler_params=_cparams("parallel", "parallel", "arbitrary"),
        name="inproj",
    )(x_all, shift_l, scale_l, shift_c, scale_c, g, w)


def _outproj_kernel(ma_ref, mb_ref, w1_ref, w2_ref, x_ref, gl_ref, gc_ref, o_ref, *, ctx_tiles):
    i = pl.program_id(1)
    acc = jnp.dot(ma_ref[0], w1_ref[...], preferred_element_type=F32)
    acc = acc + jnp.dot(mb_ref[0], w2_ref[...], preferred_element_type=F32)
    gate = jnp.where(i < ctx_tiles, gc_ref[...], gl_ref[0])
    o_ref[0] = x_ref[0] + gate * acc


def outproj(ma, mb, w1, w2, x_all, gate_l, gate_c, ctx_len, latent_only):
    b, t, d = x_all.shape
    k1, k2 = ma.shape[2], mb.shape[2]
    tm = ctx_len
    off = 1 if latent_only else 0
    t_out = t - ctx_len if latent_only else t
    return pl.pallas_call(
        functools.partial(_outproj_kernel, ctx_tiles=0 if latent_only else 1),
        grid=(b, t_out // tm),
        in_specs=[pl.BlockSpec((1, tm, k1), lambda bb, i: (bb, i + off, 0)),
                  pl.BlockSpec((1, tm, k2), lambda bb, i: (bb, i + off, 0)),
                  pl.BlockSpec((k1, d), lambda bb, i: (0, 0)),
                  pl.BlockSpec((k2, d), lambda bb, i: (0, 0)),
                  pl.BlockSpec((1, tm, d), lambda bb, i: (bb, i + off, 0)),
                  pl.BlockSpec((1, 1, d), lambda bb, i: (bb, 0, 0)),
                  pl.BlockSpec((1, d), lambda bb, i: (0, 0))],
        out_specs=pl.BlockSpec((1, tm, d), lambda bb, i: (bb, i, 0)),
        out_shape=jax.ShapeDtypeStruct((b, t_out, d), F32),
        compiler_params=_cparams("parallel", "parallel"),
        name="outproj",
    )(ma, mb, w1, w2, x_all, gate_l, gate_c)


def _qkprep_kernel(x_ref, g_ref, cos_ref, sin_ref, o_ref):
    x = x_ref[0, 0]
    y = x * lax.rsqrt(jnp.mean(x * x, axis=-1, keepdims=True) + EPS) * g_ref[0]
    lane = lax.broadcasted_iota(jnp.int32, y.shape, 1)
    partner = jnp.where(lane % 2 == 0, pltpu.roll(y, HEAD_DIM - 1, 1), pltpu.roll(y, 1, 1))
    o_ref[0] = (y * cos_ref[...] + partner * sin_ref[...]).astype(BF16)


def qkprep(slabs, gains, cos_t, sin_t):
    _, b, t, tn = slabs.shape
    per = tn // HEAD_DIM
    nh = A_KV_HEADS + A_Q_HEADS

    def x_map(bb, hh):
        qh = hh - A_KV_HEADS
        slab = jnp.where(hh < A_KV_HEADS, 0, 2 + qh // per)
        col = jnp.where(hh < A_KV_HEADS, hh, qh % per)
        return (slab, bb, 0, col)

    return pl.pallas_call(
        _qkprep_kernel,
        grid=(b, nh),
        in_specs=[pl.BlockSpec((1, 1, t, HEAD_DIM), x_map),
                  pl.BlockSpec((1, 1, HEAD_DIM), lambda bb, hh: (hh, 0, 0)),
                  pl.BlockSpec((t, HEAD_DIM), lambda bb, hh: (0, 0)),
                  pl.BlockSpec((t, HEAD_DIM), lambda bb, hh: (0, 0))],
        out_specs=pl.BlockSpec((1, t, HEAD_DIM), lambda bb, hh: (
            bb, 0, jnp.where(hh < A_KV_HEADS, A_Q_HEADS + hh, hh - A_KV_HEADS))),
        out_shape=jax.ShapeDtypeStruct((b, t, nh * HEAD_DIM), BF16),
        compiler_params=_cparams("parallel", "parallel"),
        name="qkprep",
    )(slabs, gains, cos_t, sin_t)


def _attn_kernel(q_ref, k_ref, v_ref, ga0_ref, ga1_ref, ga2_ref, o_ref, vaug_scr, *, ctx_len):
    iq = pl.program_id(2)

    @pl.when(iq == 0)
    def _():
        vaug_scr[:, :HEAD_DIM] = v_ref[0, 0].astype(BF16)
        vaug_scr[:, HEAD_DIM:] = jnp.ones((vaug_scr.shape[0], HEAD_DIM), BF16)

    def attend(n_keys):
        k = k_ref[0, :n_keys]
        vaug = vaug_scr[:n_keys]
        for g, ga_ref in enumerate((ga0_ref, ga1_ref, ga2_ref)):
            cols = slice(g * HEAD_DIM, (g + 1) * HEAD_DIM)
            s = lax.dot_general(q_ref[0, :, cols], k, (((1,), (1,)), ((), ())), preferred_element_type=F32)
            p = jnp.exp2(s - jnp.max(s, axis=-1, keepdims=True)).astype(BF16)
            pv = jnp.dot(p, vaug, preferred_element_type=F32)
            o = pv[:, :HEAD_DIM] / pv[:, HEAD_DIM:]
            o_ref[0, :, cols] = (o * _silu(ga_ref[0, 0])).astype(BF16)

    @pl.when(iq == 0)
    def _():
        attend(ctx_len)

    @pl.when(iq > 0)
    def _():
        attend(k_ref.shape[1])


def attention(qk, slabs, ctx_len):
    b, t, _ = qk.shape
    tn = slabs.shape[3]
    per = tn // HEAD_DIM
    tq = ctx_len
    gw = A_GROUP * HEAD_DIM

    def ga_spec(g):
        def ga_map(bb, kv, iq):
            h = kv * A_GROUP + g
            return (7 + h // per, bb, iq, h % per)
        return pl.BlockSpec((1, 1, tq, HEAD_DIM), ga_map)

    return pl.pallas_call(
        functools.partial(_attn_kernel, ctx_len=ctx_len),
        grid=(b, A_KV_HEADS, t // tq),
        in_specs=[pl.BlockSpec((1, tq, gw), lambda bb, kv, iq: (bb, iq, kv)),
                  pl.BlockSpec((1, t, HEAD_DIM), lambda bb, kv, iq: (bb, 0, A_Q_HEADS + kv)),
                  pl.BlockSpec((1, 1, t, HEAD_DIM), lambda bb, kv, iq: (1, bb, 0, kv)),
                  ga_spec(0), ga_spec(1), ga_spec(2)],
        out_specs=pl.BlockSpec((1, tq, gw), lambda bb, kv, iq: (bb, iq, kv)),
        out_shape=jax.ShapeDtypeStruct((b, t, A_Q_HEADS * HEAD_DIM), BF16),
        scratch_shapes=[pltpu.VMEM((t, 2 * HEAD_DIM), BF16)],
        compiler_params=_cparams("parallel", "parallel", "arbitrary"),
        name="attention",
    )(qk, qk, slabs, slabs, slabs, slabs)


def _sgu_kernel(u_ref, v_ref, gb_ref, ws_ref, bs_ref, g_ref, o_ref):
    outs = []
    for grp in range(B_GROUPS):
        cols = slice(grp * LANES, (grp + 1) * LANES)
        v = v_ref[0, 0, :, cols]
        vn = v * lax.rsqrt(jnp.mean(v * v, axis=-1, keepdims=True) + EPS) * g_ref[:, cols]
        mixed = jnp.dot(ws_ref[grp], vn.astype(BF16), preferred_element_type=F32) + bs_ref[grp]
        outs.append(u_ref[0, 0, :, cols] * mixed * _silu(gb_ref[0, 0, :, cols]))
    o_ref[0] = jnp.concatenate(outs, axis=1).astype(BF16)


def sgu(slabs, ws, bs, g):
    _, b, t, tn = slabs.shape
    spec = lambda s: pl.BlockSpec((1, 1, CHUNK, tn), lambda bb, c: (s, bb, c, 0))
    return pl.pallas_call(
        _sgu_kernel,
        grid=(b, t // CHUNK),
        in_specs=[spec(5), spec(6), spec(10),
                  pl.BlockSpec((B_GROUPS, CHUNK, CHUNK), lambda bb, c: (0, 0, 0)),
                  pl.BlockSpec((B_GROUPS, CHUNK, 1), lambda bb, c: (0, 0, 0)),
                  pl.BlockSpec((1, tn), lambda bb, c: (0, 0))],
        out_specs=pl.BlockSpec((1, CHUNK, tn), lambda bb, c: (bb, c, 0)),
        out_shape=jax.ShapeDtypeStruct((b, t, tn), BF16),
        compiler_params=_cparams("parallel", "parallel"),
        name="sgu",
    )(slabs, slabs, slabs, ws, bs, g)


def _s5_fold_rows(u_ref, tm):
    return jnp.concatenate([u_ref[pl.ds(t, tm, stride=SUB), :] for t in range(SUB)], axis=1)


def _s5_in_kernel(u_ref, w_ref, o_ref, *, tm):
    x = _s5_fold_rows(u_ref, tm).astype(BF16)
    o_ref[...] = jnp.dot(x, w_ref[0], preferred_element_type=F32)


def s5_in(slabs2, w_in, tm):
    _, r, _ = slabs2.shape
    m = r // SUB
    n_piece = w_in.shape[0]
    ncol = w_in.shape[2]
    return pl.pallas_call(
        functools.partial(_s5_in_kernel, tm=tm),
        grid=(n_piece, m // tm),
        in_specs=[pl.BlockSpec((None, tm * SUB, LANES), lambda i, rr: (0, rr, i)),
                  pl.BlockSpec((1, w_in.shape[1], ncol), lambda i, rr: (i, 0, 0))],
        out_specs=pl.BlockSpec((tm, ncol), lambda i, rr: (rr, i)),
        out_shape=jax.ShapeDtypeStruct((m, n_piece * ncol), F32),
        compiler_params=_cparams("parallel", "parallel"),
        name="s5_in",
    )(slabs2, w_in)


def _s5_scan_kernel(g_ref, c_ref, o_ref, *, ctx_pairs, all_pairs, half):
    rows = 2 * SUBLANES
    rowi = lax.broadcasted_iota(jnp.int32, (SUBLANES, half), 0)

    def tile(d, g_re, g_im, c_re, c_im):
        bwd = d == 1
        b_re, b_im = g_re, g_im
        for idx, dd in enumerate((1, 2, 4)):
            a_re, a_im = c_ref[d, idx, 0], c_ref[d, idx, 1]
            sh = SUBLANES - dd if bwd else dd
            s_re, s_im = pltpu.roll(b_re, sh, 0), pltpu.roll(b_im, sh, 0)
            b_re, b_im = (b_re + a_re * s_re - a_im * s_im,
                          b_im + a_re * s_im + a_im * s_re)
        a_re, a_im = c_ref[d, 3, 0], c_ref[d, 3, 1]
        h_re = b_re + a_re * c_re - a_im * c_im
        h_im = b_im + a_re * c_im + a_im * c_re
        if bwd:
            p_re = jnp.where(rowi == SUBLANES - 1, c_re, pltpu.roll(h_re, SUBLANES - 1, 0))
            p_im = jnp.where(rowi == SUBLANES - 1, c_im, pltpu.roll(h_im, SUBLANES - 1, 0))
            return p_re, p_im, h_re[0:1], h_im[0:1]
        p_re = jnp.where(rowi == 0, c_re, pltpu.roll(h_re, 1, 0))
        p_im = jnp.where(rowi == 0, c_im, pltpu.roll(h_im, 1, 0))
        return p_re, p_im, h_re[SUBLANES - 1:SUBLANES], h_im[SUBLANES - 1:SUBLANES]

    def pair(d, pi, carry):
        c_re, c_im = carry
        r0 = pl.multiple_of(pi * rows, rows)
        base = d * 2 * half
        order = (1, 0) if d == 1 else (0, 1)
        res = [None, None]
        for which in order:
            rr = pl.ds(r0 + which * SUBLANES, SUBLANES)
            g_re = g_ref[0, rr, pl.ds(base, half)]
            g_im = g_ref[0, rr, pl.ds(base + half, half)]
            p_re, p_im, c_re, c_im = tile(d, g_re, g_im, c_re, c_im)
            res[which] = (p_re, p_im)
        o_ref[0, pl.ds(r0, rows), pl.ds(base, half)] = jnp.concatenate(
            [res[0][0], res[1][0]], axis=0).astype(BF16)
        o_ref[0, pl.ds(r0, rows), pl.ds(base + half, half)] = jnp.concatenate(
            [res[0][1], res[1][1]], axis=0).astype(BF16)
        return c_re, c_im

    zero = (jnp.zeros((1, half), F32), jnp.zeros((1, half), F32))
    lax.fori_loop(0, all_pairs, lambda pi, c: pair(0, pi, c), zero)
    c = lax.fori_loop(0, ctx_pairs, lambda n, c: pair(1, ctx_pairs - 1 - n, c), zero)
    lax.fori_loop(0, all_pairs - ctx_pairs, lambda n, c: pair(1, all_pairs - 1 - n, c), c)


def s5_scan(g3, consts, ctx_rows):
    b, rows, wide = g3.shape
    half = PIECE_GROUPS * C_STATE
    n_piece = wide // (4 * half)
    return pl.pallas_call(
        functools.partial(_s5_scan_kernel, ctx_pairs=ctx_rows // (2 * SUBLANES),
                          all_pairs=rows // (2 * SUBLANES), half=half),
        grid=(b, n_piece),
        in_specs=[pl.BlockSpec((1, rows, 4 * half), lambda bb, i: (bb, 0, i)),
                  pl.BlockSpec((2, 4, 2, SUBLANES, half), lambda bb, i: (0, 0, 0, 0, i))],
        out_specs=pl.BlockSpec((1, rows, 4 * half), lambda bb, i: (bb, 0, i)),
        out_shape=jax.ShapeDtypeStruct((b, rows, wide), BF16),
        compiler_params=_cparams("parallel", "parallel"),
        name="s5_scan",
    )(g3, consts)


def _s5_out_kernel(u_ref, hp_ref, t_ref, w_ref, y_ref, *, tm):
    x = _s5_fold_rows(u_ref, tm).astype(BF16)
    y = jnp.dot(x, t_ref[0], preferred_element_type=F32)
    y = y + jnp.dot(hp_ref[...], w_ref[0], preferred_element_type=F32)
    for t in range(SUB):
        y_ref[pl.ds(t, tm, stride=SUB), :] = y[:, t * LANES:(t + 1) * LANES]


def s5_out(slabs2, hp2, t_intra, w_out, tm):
    _, r, width = slabs2.shape
    m = r // SUB
    n_piece = t_intra.shape[0]
    kh = w_out.shape[1]
    return pl.pallas_call(
        functools.partial(_s5_out_kernel, tm=tm),
        grid=(n_piece, m // tm),
        in_specs=[pl.BlockSpec((None, tm * SUB, LANES), lambda i, rr: (0, rr, i)),
                  pl.BlockSpec((tm, kh), lambda i, rr: (rr, i)),
                  pl.BlockSpec((1,) + t_intra.shape[1:], lambda i, rr: (i, 0, 0)),
                  pl.BlockSpec((1,) + w_out.shape[1:], lambda i, rr: (i, 0, 0))],
        out_specs=pl.BlockSpec((tm * SUB, LANES), lambda i, rr: (rr, i)),
        out_shape=jax.ShapeDtypeStruct((r, width), F32),
        compiler_params=_cparams("parallel", "parallel"),
        name="s5_out",
    )(slabs2, hp2, t_intra, w_out)


def _s5_finish_kernel(y_ref, u_ref, gc_ref, d_ref, w_ref, b_ref, o_ref):
    y = y_ref[...] + d_ref[...] * u_ref[0]
    y = jax.nn.gelu(y)
    z = jnp.dot(y.astype(BF16), w_ref[...], preferred_element_type=F32) + b_ref[...]
    o_ref[...] = (y * jax.nn.sigmoid(z) * _silu(gc_ref[0])).astype(BF16)


def s5_finish(y, slabs2, d_skip, glu_w, glu_b, tm):
    r, w = y.shape
    return pl.pallas_call(
        _s5_finish_kernel,
        grid=(r // tm,),
        in_specs=[pl.BlockSpec((tm, w), lambda i: (i, 0)),
                  pl.BlockSpec((1, tm, w), lambda i: (0, i, 0)),
                  pl.BlockSpec((1, tm, w), lambda i: (2, i, 0)),
                  pl.BlockSpec((1, w), lambda i: (0, 0)),
                  pl.BlockSpec((w, w), lambda i: (0, 0)),
                  pl.BlockSpec((1, w), lambda i: (0, 0))],
        out_specs=pl.BlockSpec((tm, w), lambda i: (i, 0)),
        out_shape=jax.ShapeDtypeStruct((r, w), BF16),
        compiler_params=_cparams("parallel"),
        name="s5_finish",
    )(y, slabs2, slabs2, d_skip, glu_w, glu_b)


def _expand_block_diag(src, unit, row_unit):
    rows, cols = src.shape
    wide = cols * PIECE_GROUPS
    lg = lambda v: int(math.log2(v))
    k = lax.broadcasted_iota(jnp.int32, (cols, wide), 0)
    j = lax.broadcasted_iota(jnp.int32, (cols, wide), 1)
    src_col = ((j >> lg(PIECE_GROUPS * unit)) << lg(unit)) + (j & (unit - 1))
    spread = jnp.where(k == src_col, 1.0, 0.0).astype(BF16)
    out = jnp.dot(src.astype(BF16), spread, preferred_element_type=F32)
    rg = (lax.broadcasted_iota(jnp.int32, (rows, wide), 0) >> lg(row_unit)) & (PIECE_GROUPS - 1)
    ch = (lax.broadcasted_iota(jnp.int32, (rows, wide), 1) >> lg(unit)) & (PIECE_GROUPS - 1)
    return jnp.where(rg == ch, out, 0.0).astype(BF16)


def _s5_expand_kernel(ts_ref, wi_ref, wo_ref, t_ref, win_ref, wout_ref, *, p, n):
    t_ref[0] = _expand_block_diag(ts_ref[0], p, p)
    win_ref[0] = _expand_block_diag(wi_ref[0], n, p)
    wout_ref[0] = _expand_block_diag(wo_ref[0], p, n)


def s5_expand(t_src, win_src, wout_src, p, n):
    npiece = t_src.shape[0]
    spec = lambda a, mult: pl.BlockSpec((1, a.shape[1], a.shape[2] * mult), lambda i: (i, 0, 0))
    shape = lambda a: jax.ShapeDtypeStruct((npiece, a.shape[1], a.shape[2] * PIECE_GROUPS), BF16)
    return pl.pallas_call(
        functools.partial(_s5_expand_kernel, p=p, n=n),
        grid=(npiece,),
        in_specs=[spec(t_src, 1), spec(win_src, 1), spec(wout_src, 1)],
        out_specs=[spec(t_src, PIECE_GROUPS), spec(win_src, PIECE_GROUPS), spec(wout_src, PIECE_GROUPS)],
        out_shape=[shape(t_src), shape(win_src), shape(wout_src)],
        compiler_params=_cparams("parallel"),
        name="s5_expand",
    )(t_src, win_src, wout_src)


def s5_derive(lam_re, lam_im, log_dt, b_re, b_im, c_re, c_im):
    hi = lax.Precision.HIGHEST
    n_dir, g, n = lam_re.shape
    p = b_re.shape[-1]
    npiece = g // PIECE_GROUPS
    dt = jnp.exp(log_dt)[None, :, :, None]

    def powers(ks):
        k = jnp.asarray(list(ks), F32).reshape(-1, 1, 1, 1)
        mag = jnp.exp(k * lam_re[None] * dt)
        ang = k * lam_im[None] * dt
        return mag * jnp.cos(ang), mag * jnp.sin(ang)

    pw_re, pw_im = powers(range(SUB + 1))
    den = lam_re * lam_re + lam_im * lam_im
    nr, ni = pw_re[1] - 1.0, pw_im[1]
    cf_re = (nr * lam_re + ni * lam_im) / den
    cf_im = (ni * lam_re - nr * lam_im) / den
    bb_re = cf_re[..., None] * b_re - cf_im[..., None] * b_im
    bb_im = cf_re[..., None] * b_im + cf_im[..., None] * b_re
    ca_re = c_re[None] * pw_re[:, :, :, None, :] - c_im[None] * pw_im[:, :, :, None, :]
    ca_im = c_re[None] * pw_im[:, :, :, None, :] + c_im[None] * pw_re[:, :, :, None, :]

    kern = (jnp.einsum('tdgpn,dgnq->tdgpq', ca_re[:SUB], bb_re, precision=hi)
            - jnp.einsum('tdgpn,dgnq->tdgpq', ca_im[:SUB], bb_im, precision=hi))
    s_idx = jnp.arange(SUB)[:, None]
    t_idx = jnp.arange(SUB)[None, :]
    lag = t_idx - s_idx
    kf = kern[jnp.clip(lag, 0, SUB - 1), 0] * (lag >= 0)[:, :, None, None, None].astype(F32)
    kb = kern[jnp.clip(-lag, 0, SUB - 1), 1] * (lag <= 0)[:, :, None, None, None].astype(F32)
    kc = (kf + kb).reshape(SUB, SUB, npiece, PIECE_GROUPS, p, p)
    t_src = kc.transpose(2, 0, 3, 5, 1, 4).reshape(npiece, SUB * PIECE_GROUPS * p, SUB * p)

    def cmul(xr, xi, yr, yi):
        return xr * yr - xi * yi, xr * yi + xi * yr

    dn_re, dn_im = powers(range(SUB - 1, -1, -1))
    f_re, f_im = cmul(dn_re[:, 0][:, :, :, None], dn_im[:, 0][:, :, :, None],
                      bb_re[0][None], bb_im[0][None])
    r_re, r_im = cmul(pw_re[:SUB, 1][:, :, :, None], pw_im[:SUB, 1][:, :, :, None],
                      bb_re[1][None], bb_im[1][None])
    v = jnp.stack([jnp.stack([f_re, f_im]), jnp.stack([r_re, r_im])])
    v = v.reshape(n_dir, 2, SUB, npiece, PIECE_GROUPS, n, p)
    win_src = v.transpose(3, 2, 4, 6, 0, 1, 5).reshape(npiece, SUB * PIECE_GROUPS * p, n_dir * 2 * n)

    o_f = jnp.stack([ca_re[1:SUB + 1, 0], -ca_im[1:SUB + 1, 0]])
    up_re, up_im = powers(range(SUB, 0, -1))
    cb_re = c_re[1][None] * up_re[:, 1, :, None, :] - c_im[1][None] * up_im[:, 1, :, None, :]
    cb_im = c_re[1][None] * up_im[:, 1, :, None, :] + c_im[1][None] * up_re[:, 1, :, None, :]
    o_b = jnp.stack([cb_re, -cb_im])
    o = jnp.stack([o_f, o_b]).reshape(n_dir, 2, SUB, npiece, PIECE_GROUPS, p, n)
    wout_src = o.transpose(3, 0, 1, 4, 6, 2, 5).reshape(npiece, n_dir * 2 * PIECE_GROUPS * n, SUB * p)

    q_re, q_im = powers([SUB * m for m in range(1, SUBLANES + 1)])
    q_re = q_re.reshape(SUBLANES, n_dir, g * n)
    q_im = q_im.reshape(SUBLANES, n_dir, g * n)
    row = jnp.arange(SUBLANES)
    consts = []
    for d in range(n_dir):
        kinds = []
        for dd in (1, 2, 4):
            keep = (row + dd <= SUBLANES - 1) if d == 1 else (row >= dd)
            m = keep.astype(F32)[:, None]
            kinds.append(jnp.stack([m * q_re[dd - 1, d][None], m * q_im[dd - 1, d][None]]))
        sel = (SUBLANES - 1 - row) if d == 1 else row
        kinds.append(jnp.stack([q_re[sel, d], q_im[sel, d]]))
        consts.append(jnp.stack(kinds))
    consts = jnp.stack(consts)
    t_intra, w_in, w_out = s5_expand(t_src, win_src, wout_src, p, n)
    return t_intra, w_in, w_out, consts


def _lru_kernel(*refs, bwd, nch, tc, width):
    if bwd:
        (x_ref, xp_ref, xn_ref, cw_ref, cb_ref, lam_ref, wa_ref, ba_ref, wx_ref, bx_ref,
         hf_ref, gd_ref, o_ref, pad_scr, a_scr, b_scr, carry_scr) = refs
    else:
        (x_ref, xp_ref, xn_ref, cw_ref, cb_ref, lam_ref, wa_ref, ba_ref, wx_ref, bx_ref,
         o_ref, pad_scr, a_scr, b_scr, carry_scr) = refs
    k = pl.program_id(1)
    chunk = jnp.where(k == 0, 0, nch - k) if bwd else k
    prev_ok = jnp.logical_and(chunk != 0, chunk != 1)
    next_ok = jnp.logical_and(chunk != 0, chunk != nch - 1)
    pad_scr[0:SUBLANES] = jnp.where(prev_ok, xp_ref[0, 0], 0.0)
    pad_scr[SUBLANES:SUBLANES + tc] = x_ref[0, 0]
    pad_scr[SUBLANES + tc:2 * SUBLANES + tc] = jnp.where(next_ok, xn_ref[0, 0], 0.0)
    xc = cb_ref[...]
    for tap in range(CONV_W):
        xc = xc + cw_ref[tap:tap + 1] * pad_scr[SUBLANES - 1 + tap:SUBLANES - 1 + tap + tc]

    bw = width // D_BLOCKS
    r_parts, i_parts = [], []
    for blk in range(D_BLOCKS):
        xb = xc[:, blk * bw:(blk + 1) * bw].astype(BF16)
        r_parts.append(jnp.dot(xb, wa_ref[blk], preferred_element_type=F32))
        i_parts.append(jnp.dot(xb, wx_ref[blk], preferred_element_type=F32))
    r = jax.nn.sigmoid(jnp.concatenate(r_parts, axis=1) + ba_ref[...])
    ig = jax.nn.sigmoid(jnp.concatenate(i_parts, axis=1) + bx_ref[...])
    nl = -lam_ref[...]
    e = jnp.exp(-jnp.abs(nl))
    e1 = 1.0 + e
    log1p_e = jnp.where(e1 == 1.0, e, jnp.log(e1) * (e / jnp.where(e1 == 1.0, 1.0, e1 - 1.0)))
    softplus = jnp.maximum(nl, 0.0) + log1p_e
    log_a = (-LRU_C * softplus) * r
    a_scr[...] = jnp.exp(log_a)
    th = jnp.tanh(log_a)
    b_scr[...] = jnp.sqrt(-2.0 * th / (1.0 - th)) * (ig * xc)

    @pl.when(k == 0)
    def _():
        carry_scr[...] = jnp.zeros_like(carry_scr)

    rowi = lax.broadcasted_iota(jnp.int32, (SUBLANES, width), 0)
    n_tiles = tc // SUBLANES

    def body(n, carry):
        ti = n_tiles - 1 - n if bwd else n
        rr = pl.ds(pl.multiple_of(ti * SUBLANES, SUBLANES), SUBLANES)
        a, b = a_scr[rr], b_scr[rr]
        for dd in (1, 2, 4):
            keep = (rowi + dd <= SUBLANES - 1) if bwd else (rowi >= dd)
            sh = SUBLANES - dd if bwd else dd
            b = b + a * jnp.where(keep, pltpu.roll(b, sh, 0), 0.0)
            a = a * jnp.where(keep, pltpu.roll(a, sh, 0), 1.0)
        h = b + a * carry
        b_scr[rr] = h
        return h[0:1] if bwd else h[SUBLANES - 1:SUBLANES]

    carry_scr[...] = lax.fori_loop(0, n_tiles, body, carry_scr[...])
    if bwd:
        o_ref[0] = ((hf_ref[0] + b_scr[...]) * _silu(gd_ref[0, 0])).astype(BF16)
    else:
        o_ref[0] = b_scr[...]


def lru_pass(slabs, conv_w, conv_b, lam, wa, ba, wx, bx, ctx_len, hf=None):
    _, b, t, w = slabs.shape
    bwd = hf is not None
    tc = ctx_len
    nch = t // tc
    hb = tc // SUBLANES
    nhb = t // SUBLANES

    def ch(k):
        return jnp.where(k == 0, 0, nch - k) if bwd else k

    in_specs = [pl.BlockSpec((1, 1, tc, w), lambda bb, k: (1, bb, ch(k), 0)),
                pl.BlockSpec((1, 1, SUBLANES, w), lambda bb, k: (1, bb, jnp.maximum(ch(k) * hb - 1, 0), 0)),
                pl.BlockSpec((1, 1, SUBLANES, w), lambda bb, k: (1, bb, jnp.minimum((ch(k) + 1) * hb, nhb - 1), 0)),
                pl.BlockSpec((CONV_W, w), lambda bb, k: (0, 0)),
                pl.BlockSpec((1, w), lambda bb, k: (0, 0)),
                pl.BlockSpec((1, w), lambda bb, k: (0, 0)),
                pl.BlockSpec(wa.shape, lambda bb, k: (0, 0, 0)),
                pl.BlockSpec((1, w), lambda bb, k: (0, 0)),
                pl.BlockSpec(wx.shape, lambda bb, k: (0, 0, 0)),
                pl.BlockSpec((1, w), lambda bb, k: (0, 0))]
    args = [slabs, slabs, slabs, conv_w, conv_b, lam, wa, ba, wx, bx]
    if bwd:
        in_specs += [pl.BlockSpec((1, tc, w), lambda bb, k: (bb, ch(k), 0)),
                     pl.BlockSpec((1, 1, tc, w), lambda bb, k: (3, bb, ch(k), 0))]
        args += [hf, slabs]
    return pl.pallas_call(
        functools.partial(_lru_kernel, bwd=bwd, nch=nch, tc=tc, width=w),
        grid=(b, nch),
        in_specs=in_specs,
        out_specs=pl.BlockSpec((1, tc, w), lambda bb, k: (bb, ch(k), 0)),
        out_shape=jax.ShapeDtypeStruct((b, t, w), BF16 if bwd else F32),
        scratch_shapes=[pltpu.VMEM((tc + 2 * SUBLANES, w), F32),
                        pltpu.VMEM((tc, w), F32),
                        pltpu.VMEM((tc, w), F32),
                        pltpu.VMEM((1, w), F32)],
        compiler_params=_cparams("parallel", "arbitrary"),
        name="lru_bwd" if bwd else "lru_fwd",
    )(*args)


def rope_tables(ctx_len, seq):
    rows = seq // GRID_W
    axis_dim = HEAD_DIM // 2
    row = jnp.repeat(jnp.arange(rows), GRID_W).astype(F32)
    col = jnp.tile(jnp.arange(GRID_W), rows).astype(F32)
    inv = ROPE_THETA ** (-jnp.arange(0, axis_dim, 2, dtype=F32) / axis_dim)
    ang = jnp.concatenate([row[:, None] * inv, col[:, None] * inv], axis=-1)
    cos, sin = jnp.cos(ang), jnp.sin(ang)
    cos_t = jnp.repeat(cos, 2, axis=-1)
    sin_t = jnp.stack([-sin, sin], axis=-1).reshape(seq, HEAD_DIM)
    cos_t = jnp.concatenate([jnp.ones((ctx_len, HEAD_DIM), F32), cos_t], axis=0)
    sin_t = jnp.concatenate([jnp.zeros((ctx_len, HEAD_DIM), F32), sin_t], axis=0)
    return cos_t, sin_t


def even_layer(x_all, mod, g, w_in, w_out, q_g, k_g, sgu_g, ws, bs, cos_t, sin_t, ctx_len, latent_only):
    shift_l, scale_l, gate_l, shift_c, scale_c, gate_c = mod
    slabs = inproj(x_all, shift_l, scale_l, shift_c, scale_c, g, w_in.astype(BF16), ctx_len, tn=512,
                   tm=_pick_tile(x_all.shape[1], 768))
    gains = jnp.concatenate([jnp.broadcast_to(k_g, (A_KV_HEADS, HEAD_DIM)),
                             jnp.broadcast_to(q_g * (HEAD_DIM ** -0.5 * math.log2(math.e)),
                                              (A_Q_HEADS, HEAD_DIM))])[:, None, :]
    qk = qkprep(slabs, gains, cos_t, sin_t)
    mix_a = attention(qk, slabs, ctx_len)
    mix_b = sgu(slabs, ws.astype(BF16), bs[:, :, None], sgu_g[None, :])
    ka = mix_a.shape[2]
    w_out = w_out.astype(BF16)
    return outproj(mix_a, mix_b, w_out[:ka], w_out[ka:], x_all, gate_l, gate_c, ctx_len, latent_only)


def odd_layer(x_all, mod, g, w_in, w_out, s5p, d_skip, glu_w, glu_b, conv_w, conv_b, lam, wa, ba, wx, bx,
              ctx_len, latent_only):
    shift_l, scale_l, gate_l, shift_c, scale_c, gate_c = mod
    b, t, _ = x_all.shape
    slabs = inproj(x_all, shift_l, scale_l, shift_c, scale_c, g, w_in.astype(BF16), ctx_len, tn=1024,
                   tm=_pick_tile(t, 768))
    cw = slabs.shape[3]
    rows = b * t // SUB
    t_intra, w_s5in, w_s5out, consts = s5_derive(*s5p)
    slabs2 = slabs.reshape(slabs.shape[0], b * t, cw)
    gmat = s5_in(slabs2, w_s5in, tm=_pick_tile(rows, 576))
    hp = s5_scan(gmat.reshape(b, t // SUB, -1), consts, ctx_len // SUB)
    y = s5_out(slabs2, hp.reshape(rows, -1), t_intra, w_s5out, tm=_pick_tile(rows, 576))
    mix_c = s5_finish(y, slabs2, d_skip[None, :], glu_w.astype(BF16), glu_b[None, :],
                      tm=_pick_tile(b * t, 512))
    hf = lru_pass(slabs, conv_w, conv_b[None, :], lam[0][None, :], wa[0].astype(BF16), ba[0][None, :],
                  wx[0].astype(BF16), bx[0][None, :], ctx_len)
    mix_d = lru_pass(slabs, conv_w, conv_b[None, :], lam[1][None, :], wa[1].astype(BF16), ba[1][None, :],
                     wx[1].astype(BF16), bx[1][None, :], ctx_len, hf=hf)
    w_out = w_out.astype(BF16)
    return outproj(mix_c.reshape(b, t, cw), mix_d, w_out[:cw], w_out[cw:], x_all, gate_l, gate_c,
                   ctx_len, latent_only)


def kernel(x, c, ctx, c_ctx, ada_w, ada_b, norm_g, ev_w_in, ev_w_out, ev_q_g, ev_k_g, ev_sgu_g, ev_ws, ev_bs,
           od_w_in, od_w_out, s5_lam_re, s5_lam_im, s5_log_dt, s5_b_re, s5_b_im, s5_c_re, s5_c_im, s5_d,
           s5_glu_w, s5_glu_b, lru_conv_w, lru_conv_b, lru_lam, lru_wa, lru_ba, lru_wx, lru_bx):
    b, seq, d = x.shape
    ctx_len = ctx.shape[1]
    depth = ada_w.shape[0]
    assert b + 1 <= SUBLANES and seq % ctx_len == 0
    cos_t, sin_t = rope_tables(ctx_len, seq)
    cc = jnp.zeros((SUBLANES, d), F32).at[:b].set(c).at[b].set(c_ctx)
    mods = adaln_all(cc, ada_w, ada_b)
    x_all = jnp.concatenate([ctx, x], axis=1)
    for layer in range(depth):
        m = mods[layer]
        mod = tuple(m[:b, None, k * d:(k + 1) * d] for k in range(3)) + \
            tuple(m[b:b + 1, k * d:(k + 1) * d] for k in range(3))
        last = layer == depth - 1
        j = layer // 2
        g = norm_g[layer][None, :]
        if layer % 2 == 0:
            x_all = even_layer(x_all, mod, g, ev_w_in[j], ev_w_out[j], ev_q_g[j], ev_k_g[j], ev_sgu_g[j],
                               ev_ws[j], ev_bs[j], cos_t, sin_t, ctx_len, last)
        else:
            s5p = (s5_lam_re[j], s5_lam_im[j], s5_log_dt[j], s5_b_re[j], s5_b_im[j], s5_c_re[j], s5_c_im[j])
            x_all = odd_layer(x_all, mod, g, od_w_in[j], od_w_out[j], s5p, s5_d[j], s5_glu_w[j], s5_glu_b[j],
                              lru_conv_w[j], lru_conv_b[j], lru_lam[j], lru_wa[j], lru_ba[j], lru_wx[j],
                              lru_bx[j], ctx_len, last)
    return x_all
```

```python
import functools
import math

import jax
import jax.numpy as jnp
from jax import lax
from jax.experimental import pallas as pl
from jax.experimental.pallas import tpu as pltpu

F32 = jnp.float32
BF16 = jnp.bfloat16

EPS = 1e-6
GRID_W = 64
HEAD_DIM = 128
A_Q_HEADS = 12
A_KV_HEADS = 4
A_GROUP = A_Q_HEADS // A_KV_HEADS
B_GROUPS = 4
CHUNK = 128
ROPE_THETA = 10000.0
C_GROUP_DIM = 16
C_STATE = 64
D_BLOCKS = 8
CONV_W = 4
LRU_C = 8.0

LANES = 128
SUBLANES = 8
SUB = 8
PIECE_GROUPS = LANES // C_GROUP_DIM
VMEM_LIMIT = 50 * 1024 * 1024


def _cparams(*sem):
    return pltpu.CompilerParams(dimension_semantics=sem, vmem_limit_bytes=VMEM_LIMIT)


def _silu(x):
    return x * jax.nn.sigmoid(x)


def _pick_tile(rows, target, align=16):
    best = None
    for cand in range(align, min(rows, target) + 1, align):
        if rows % cand == 0:
            best = cand
    assert best is not None, (rows, target)
    return best


def _adaln_kernel(cc_ref, w_ref, b_ref, o_ref):
    s = _silu(cc_ref[...])
    o_ref[0] = jnp.dot(s, w_ref[0], preferred_element_type=F32,
                       precision=lax.Precision.HIGHEST) + b_ref[0]


def adaln_all(cc, ada_w, ada_b):
    depth, d, n3 = ada_w.shape
    tn = 512
    return pl.pallas_call(
        _adaln_kernel,
        grid=(depth, n3 // tn),
        in_specs=[pl.BlockSpec((8, d), lambda l, j: (0, 0)),
                  pl.BlockSpec((1, d, tn), lambda l, j: (l, 0, j)),
                  pl.BlockSpec((1, 1, tn), lambda l, j: (l, 0, j))],
        out_specs=pl.BlockSpec((1, 8, tn), lambda l, j: (l, 0, j)),
        out_shape=jax.ShapeDtypeStruct((depth, 8, n3), F32),
        compiler_params=_cparams("parallel", "parallel"),
        name="adaln",
    )(cc, ada_w, ada_b.reshape(depth, 1, n3))


def _inproj_kernel(x_ref, shl_ref, scl_ref, shc_ref, scc_ref, g_ref, w_ref, o_ref, h_scr, *, tm, ctx_len):
    i = pl.program_id(1)
    first = pl.program_id(2) == 0

    def modulate(rows, sc_ref, sh_ref):
        x = x_ref[0, rows]
        r = lax.rsqrt(jnp.mean(x * x, axis=-1, keepdims=True) + EPS)
        h_scr[rows] = ((x * r) * (g_ref[...] * (1.0 + sc_ref[...])) + sh_ref[...]).astype(BF16)

    scl, shl = scl_ref.at[0], shl_ref.at[0]
    if tm >= ctx_len:
        @pl.when(jnp.logical_and(first, i == 0))
        def _():
            modulate(slice(0, ctx_len), scc_ref, shc_ref)
            if tm > ctx_len:
                modulate(slice(ctx_len, tm), scl, shl)

        @pl.when(jnp.logical_and(first, i > 0))
        def _():
            modulate(slice(0, tm), scl, shl)
    else:
        @pl.when(jnp.logical_and(first, i < ctx_len // tm))
        def _():
            modulate(slice(0, tm), scc_ref, shc_ref)

        @pl.when(jnp.logical_and(first, i >= ctx_len // tm))
        def _():
            modulate(slice(0, tm), scl, shl)

    o_ref[0, 0] = jnp.dot(h_scr[...], w_ref[...], preferred_element_type=F32).astype(o_ref.dtype)


def inproj(x_all, shift_l, scale_l, shift_c, scale_c, g, w, ctx_len, tm, out_dtype):
    b, t, d = x_all.shape
    n_slab, _, tn = w.shape
    n = n_slab * tn
    assert tm % ctx_len == 0 or ctx_len % tm == 0
    return pl.pallas_call(
        functools.partial(_inproj_kernel, tm=tm, ctx_len=ctx_len),
        grid=(b, t // tm, n // tn),
        in_specs=[pl.BlockSpec((1, tm, d), lambda bb, i, j: (bb, i, 0)),
                  pl.BlockSpec((1, 1, d), lambda bb, i, j: (bb, 0, 0)),
                  pl.BlockSpec((1, 1, d), lambda bb, i, j: (bb, 0, 0)),
                  pl.BlockSpec((1, d), lambda bb, i, j: (0, 0)),
                  pl.BlockSpec((1, d), lambda bb, i, j: (0, 0)),
                  pl.BlockSpec((1, d), lambda bb, i, j: (0, 0)),
                  pl.BlockSpec((None, d, tn), lambda bb, i, j: (j, 0, 0))],
        out_specs=pl.BlockSpec((1, 1, tm, tn), lambda bb, i, j: (j, bb, i, 0)),
        out_shape=jax.ShapeDtypeStruct((n // tn, b, t, tn), out_dtype),
        scratch_shapes=[pltpu.VMEM((tm, d), BF16)],
        compiler_params=_cparams("parallel", "parallel", "arbitrary"),
        name="inproj",
    )(x_all, shift_l, scale_l, shift_c, scale_c, g, w)


def _outproj_kernel(ma_ref, mb_ref, w1_ref, w2_ref, x_ref, gl_ref, gc_ref, o_ref, *, ctx_tiles):
    i = pl.program_id(1)
    acc = jnp.dot(ma_ref[0], w1_ref[...], preferred_element_type=F32)
    acc = acc + jnp.dot(mb_ref[0], w2_ref[...], preferred_element_type=F32)
    gate = jnp.where(i < ctx_tiles, gc_ref[...], gl_ref[0])
    o_ref[0] = x_ref[0] + gate * acc


def outproj(ma, mb, w, x_all, gate_l, gate_c, ctx_len, latent_only):
    b, t, d = x_all.shape
    k1, k2 = ma.shape[2], mb.shape[2]
    assert w.shape[0] == k1 + k2 and k1 % k2 == 0
    tm = ctx_len
    off = 1 if latent_only else 0
    t_out = t - ctx_len if latent_only else t
    return pl.pallas_call(
        functools.partial(_outproj_kernel, ctx_tiles=0 if latent_only else 1),
        grid=(b, t_out // tm),
        in_specs=[pl.BlockSpec((1, tm, k1), lambda bb, i: (bb, i + off, 0)),
                  pl.BlockSpec((1, tm, k2), lambda bb, i: (bb, i + off, 0)),
                  pl.BlockSpec((k1, d), lambda bb, i: (0, 0)),
                  pl.BlockSpec((k2, d), lambda bb, i: (k1 // k2, 0)),
                  pl.BlockSpec((1, tm, d), lambda bb, i: (bb, i + off, 0)),
                  pl.BlockSpec((1, 1, d), lambda bb, i: (bb, 0, 0)),
                  pl.BlockSpec((1, d), lambda bb, i: (0, 0))],
        out_specs=pl.BlockSpec((1, tm, d), lambda bb, i: (bb, i, 0)),
        out_shape=jax.ShapeDtypeStruct((b, t_out, d), F32),
        compiler_params=_cparams("parallel", "parallel"),
        name="outproj",
    )(ma, mb, w, w, x_all, gate_l, gate_c)


def _norm_rope(x, gain, cos, sin):
    y = x * lax.rsqrt(jnp.mean(x * x, axis=-1, keepdims=True) + EPS) * gain
    lane = lax.broadcasted_iota(jnp.int32, y.shape, 1)
    partner = jnp.where((lane & 1) == 0, pltpu.roll(y, HEAD_DIM - 1, 1), pltpu.roll(y, 1, 1))
    return y * cos + partner * sin


def _attn_kernel(q0_ref, q1_ref, q2_ref, k_ref, v_ref, ga0_ref, ga1_ref, ga2_ref, qg_ref, kg_ref,
                 cos_ref, sin_ref, o_ref, k_scr, vaug_scr, *, ctx_len, tq):
    iq = pl.program_id(2)

    @pl.when(iq == 0)
    def _():
        k_scr[...] = _norm_rope(k_ref[0, 0].astype(F32), kg_ref[...], cos_ref[...], sin_ref[...]).astype(BF16)
        vaug_scr[:, :HEAD_DIM] = v_ref[0, 0].astype(BF16)
        vaug_scr[:, HEAD_DIM:] = jnp.ones((vaug_scr.shape[0], HEAD_DIM), BF16)

    def attend(n_keys):
        rows = pl.ds(pl.multiple_of(iq * tq, tq), tq)
        cos_q, sin_q = cos_ref[rows, :], sin_ref[rows, :]
        k = k_scr[:n_keys]
        vaug = vaug_scr[:n_keys]
        for g, (q_ref, ga_ref) in enumerate(((q0_ref, ga0_ref), (q1_ref, ga1_ref), (q2_ref, ga2_ref))):
            q = _norm_rope(q_ref[0, 0].astype(F32), qg_ref[...], cos_q, sin_q).astype(BF16)
            s = lax.dot_general(q, k, (((1,), (1,)), ((), ())), preferred_element_type=F32)
            p = jnp.exp2(s - jnp.max(s, axis=-1, keepdims=True)).astype(BF16)
            pv = jnp.dot(p, vaug, preferred_element_type=F32)
            o = pv[:, :HEAD_DIM] / pv[:, HEAD_DIM:]
            o_ref[0, :, g * HEAD_DIM:(g + 1) * HEAD_DIM] = (o * _silu(ga_ref[0, 0].astype(F32))).astype(BF16)

    @pl.when(iq == 0)
    def _():
        attend(ctx_len)

    @pl.when(iq > 0)
    def _():
        attend(k_scr.shape[0])


def attention(slabs, q_gain, k_gain, cos_t, sin_t, ctx_len):
    _, b, t, tn = slabs.shape
    per = tn // HEAD_DIM
    tq = ctx_len
    gw = A_GROUP * HEAD_DIM
    assert A_GROUP == 3

    def head_spec(first_slab, g):
        def head_map(bb, kv, iq):
            h = kv * A_GROUP + g
            return (first_slab + h // per, bb, iq, h % per)
        return pl.BlockSpec((1, 1, tq, HEAD_DIM), head_map)

    vec = pl.BlockSpec((1, HEAD_DIM), lambda bb, kv, iq: (0, 0))
    table = pl.BlockSpec((t, HEAD_DIM), lambda bb, kv, iq: (0, 0))
    return pl.pallas_call(
        functools.partial(_attn_kernel, ctx_len=ctx_len, tq=tq),
        grid=(b, A_KV_HEADS, t // tq),
        in_specs=[head_spec(2, 0), head_spec(2, 1), head_spec(2, 2),
                  pl.BlockSpec((1, 1, t, HEAD_DIM), lambda bb, kv, iq: (0, bb, 0, kv)),
                  pl.BlockSpec((1, 1, t, HEAD_DIM), lambda bb, kv, iq: (1, bb, 0, kv)),
                  head_spec(7, 0), head_spec(7, 1), head_spec(7, 2),
                  vec, vec, table, table],
        out_specs=pl.BlockSpec((1, tq, gw), lambda bb, kv, iq: (bb, iq, kv)),
        out_shape=jax.ShapeDtypeStruct((b, t, A_Q_HEADS * HEAD_DIM), BF16),
        scratch_shapes=[pltpu.VMEM((t, HEAD_DIM), BF16), pltpu.VMEM((t, 2 * HEAD_DIM), BF16)],
        compiler_params=_cparams("parallel", "parallel", "arbitrary"),
        name="attention",
    )(slabs, slabs, slabs, slabs, slabs, slabs, slabs, slabs, q_gain, k_gain, cos_t, sin_t)


def _sgu_kernel(u_ref, v_ref, gb_ref, ws_ref, bs_ref, g_ref, o_ref):
    outs = []
    for grp in range(B_GROUPS):
        cols = slice(grp * LANES, (grp + 1) * LANES)
        v = v_ref[0, 0, :, cols].astype(F32)
        vn = v * lax.rsqrt(jnp.mean(v * v, axis=-1, keepdims=True) + EPS) * g_ref[:, cols]
        mixed = jnp.dot(ws_ref[grp], vn.astype(BF16), preferred_element_type=F32) + bs_ref[grp]
        outs.append(u_ref[0, 0, :, cols].astype(F32) * mixed * _silu(gb_ref[0, 0, :, cols].astype(F32)))
    o_ref[0] = jnp.concatenate(outs, axis=1).astype(BF16)


def sgu(slabs, ws, bs, g):
    _, b, t, tn = slabs.shape
    spec = lambda s: pl.BlockSpec((1, 1, CHUNK, tn), lambda bb, c: (s, bb, c, 0))
    return pl.pallas_call(
        _sgu_kernel,
        grid=(b, t // CHUNK),
        in_specs=[spec(5), spec(6), spec(10),
                  pl.BlockSpec((B_GROUPS, CHUNK, CHUNK), lambda bb, c: (0, 0, 0)),
                  pl.BlockSpec((B_GROUPS, CHUNK, 1), lambda bb, c: (0, 0, 0)),
                  pl.BlockSpec((1, tn), lambda bb, c: (0, 0))],
        out_specs=pl.BlockSpec((1, CHUNK, tn), lambda bb, c: (bb, c, 0)),
        out_shape=jax.ShapeDtypeStruct((b, t, tn), BF16),
        compiler_params=_cparams("parallel", "parallel"),
        name="sgu",
    )(slabs, slabs, slabs, ws, bs, g)


def _s5_fold_rows(u_ref, tm):
    return jnp.concatenate([u_ref[pl.ds(t, tm, stride=SUB), :] for t in range(SUB)], axis=1)


def _s5_in_kernel(u_ref, w_ref, o_ref, *, tm):
    x = _s5_fold_rows(u_ref, tm).astype(BF16)
    o_ref[...] = jnp.dot(x, w_ref[0], preferred_element_type=F32)


def s5_in(slabs2, w_in, tm):
    _, r, _ = slabs2.shape
    m = r // SUB
    n_piece = w_in.shape[0]
    ncol = w_in.shape[2]
    return pl.pallas_call(
        functools.partial(_s5_in_kernel, tm=tm),
        grid=(n_piece, m // tm),
        in_specs=[pl.BlockSpec((None, tm * SUB, LANES), lambda i, rr: (0, rr, i)),
                  pl.BlockSpec((1, w_in.shape[1], ncol), lambda i, rr: (i, 0, 0))],
        out_specs=pl.BlockSpec((tm, ncol), lambda i, rr: (rr, i)),
        out_shape=jax.ShapeDtypeStruct((m, n_piece * ncol), F32),
        compiler_params=_cparams("parallel", "parallel"),
        name="s5_in",
    )(slabs2, w_in)


def _s5_scan_kernel(g_ref, c_ref, o_ref, *, ctx_pairs, all_pairs, half):
    rows = 2 * SUBLANES
    rowi = lax.broadcasted_iota(jnp.int32, (SUBLANES, half), 0)

    def tile(d, g_re, g_im, c_re, c_im):
        bwd = d == 1
        b_re, b_im = g_re, g_im
        for idx, dd in enumerate((1, 2, 4)):
            a_re, a_im = c_ref[d, idx, 0], c_ref[d, idx, 1]
            sh = SUBLANES - dd if bwd else dd
            s_re, s_im = pltpu.roll(b_re, sh, 0), pltpu.roll(b_im, sh, 0)
            b_re, b_im = (b_re + a_re * s_re - a_im * s_im,
                          b_im + a_re * s_im + a_im * s_re)
        a_re, a_im = c_ref[d, 3, 0], c_ref[d, 3, 1]
        h_re = b_re + a_re * c_re - a_im * c_im
        h_im = b_im + a_re * c_im + a_im * c_re
        if bwd:
            p_re = jnp.where(rowi == SUBLANES - 1, c_re, pltpu.roll(h_re, SUBLANES - 1, 0))
            p_im = jnp.where(rowi == SUBLANES - 1, c_im, pltpu.roll(h_im, SUBLANES - 1, 0))
            return p_re, p_im, h_re[0:1], h_im[0:1]
        p_re = jnp.where(rowi == 0, c_re, pltpu.roll(h_re, 1, 0))
        p_im = jnp.where(rowi == 0, c_im, pltpu.roll(h_im, 1, 0))
        return p_re, p_im, h_re[SUBLANES - 1:SUBLANES], h_im[SUBLANES - 1:SUBLANES]

    def pair(d, pi, carry):
        c_re, c_im = carry
        r0 = pl.multiple_of(pi * rows, rows)
        base = d * 2 * half
        order = (1, 0) if d == 1 else (0, 1)
        res = [None, None]
        for which in order:
            rr = pl.ds(r0 + which * SUBLANES, SUBLANES)
            g_re = g_ref[0, rr, pl.ds(base, half)]
            g_im = g_ref[0, rr, pl.ds(base + half, half)]
            p_re, p_im, c_re, c_im = tile(d, g_re, g_im, c_re, c_im)
            res[which] = (p_re, p_im)
        o_ref[0, pl.ds(r0, rows), pl.ds(base, half)] = jnp.concatenate(
            [res[0][0], res[1][0]], axis=0).astype(BF16)
        o_ref[0, pl.ds(r0, rows), pl.ds(base + half, half)] = jnp.concatenate(
            [res[0][1], res[1][1]], axis=0).astype(BF16)
        return c_re, c_im

    zero = (jnp.zeros((1, half), F32), jnp.zeros((1, half), F32))
    lax.fori_loop(0, all_pairs, lambda pi, c: pair(0, pi, c), zero)
    c = lax.fori_loop(0, ctx_pairs, lambda n, c: pair(1, ctx_pairs - 1 - n, c), zero)
    lax.fori_loop(0, all_pairs - ctx_pairs, lambda n, c: pair(1, all_pairs - 1 - n, c), c)


def s5_scan(g3, consts, ctx_rows):
    b, rows, wide = g3.shape
    half = PIECE_GROUPS * C_STATE
    n_piece = wide // (4 * half)
    return pl.pallas_call(
        functools.partial(_s5_scan_kernel, ctx_pairs=ctx_rows // (2 * SUBLANES),
                          all_pairs=rows // (2 * SUBLANES), half=half),
        grid=(b, n_piece),
        in_specs=[pl.BlockSpec((1, rows, 4 * half), lambda bb, i: (bb, 0, i)),
                  pl.BlockSpec((2, 4, 2, SUBLANES, half), lambda bb, i: (0, 0, 0, 0, i))],
        out_specs=pl.BlockSpec((1, rows, 4 * half), lambda bb, i: (bb, 0, i)),
        out_shape=jax.ShapeDtypeStruct((b, rows, wide), BF16),
        compiler_params=_cparams("parallel", "parallel"),
        name="s5_scan",
    )(g3, consts)


def _s5_out_kernel(u_ref, hp_ref, t_ref, w_ref, y_ref, *, tm):
    x = _s5_fold_rows(u_ref, tm).astype(BF16)
    y = jnp.dot(x, t_ref[0], preferred_element_type=F32)
    y = y + jnp.dot(hp_ref[...], w_ref[0], preferred_element_type=F32)
    for t in range(SUB):
        y_ref[pl.ds(t, tm, stride=SUB), :] = y[:, t * LANES:(t + 1) * LANES]


def s5_out(slabs2, hp2, t_intra, w_out, tm):
    _, r, width = slabs2.shape
    m = r // SUB
    n_piece = t_intra.shape[0]
    kh = w_out.shape[1]
    return pl.pallas_call(
        functools.partial(_s5_out_kernel, tm=tm),
        grid=(n_piece, m // tm),
        in_specs=[pl.BlockSpec((None, tm * SUB, LANES), lambda i, rr: (0, rr, i)),
                  pl.BlockSpec((tm, kh), lambda i, rr: (rr, i)),
                  pl.BlockSpec((1,) + t_intra.shape[1:], lambda i, rr: (i, 0, 0)),
                  pl.BlockSpec((1,) + w_out.shape[1:], lambda i, rr: (i, 0, 0))],
        out_specs=pl.BlockSpec((tm * SUB, LANES), lambda i, rr: (rr, i)),
        out_shape=jax.ShapeDtypeStruct((r, width), F32),
        compiler_params=_cparams("parallel", "parallel"),
        name="s5_out",
    )(slabs2, hp2, t_intra, w_out)


def _s5_finish_kernel(y_ref, u_ref, gc_ref, d_ref, w_ref, b_ref, o_ref):
    y = y_ref[...] + d_ref[...] * u_ref[0]
    y = jax.nn.gelu(y)
    z = jnp.dot(y.astype(BF16), w_ref[...], preferred_element_type=F32) + b_ref[...]
    o_ref[...] = (y * jax.nn.sigmoid(z) * _silu(gc_ref[0])).astype(BF16)


def s5_finish(y, slabs2, d_skip, glu_w, glu_b, tm):
    r, w = y.shape
    return pl.pallas_call(
        _s5_finish_kernel,
        grid=(r // tm,),
        in_specs=[pl.BlockSpec((tm, w), lambda i: (i, 0)),
                  pl.BlockSpec((1, tm, w), lambda i: (0, i, 0)),
                  pl.BlockSpec((1, tm, w), lambda i: (2, i, 0)),
                  pl.BlockSpec((1, w), lambda i: (0, 0)),
                  pl.BlockSpec((w, w), lambda i: (0, 0)),
                  pl.BlockSpec((1, w), lambda i: (0, 0))],
        out_specs=pl.BlockSpec((tm, w), lambda i: (i, 0)),
        out_shape=jax.ShapeDtypeStruct((r, w), BF16),
        compiler_params=_cparams("parallel"),
        name="s5_finish",
    )(y, slabs2, slabs2, d_skip, glu_w, glu_b)


def _expand_block_diag(src, unit, row_unit):
    rows, cols = src.shape
    wide = cols * PIECE_GROUPS
    lg = lambda v: int(math.log2(v))
    k = lax.broadcasted_iota(jnp.int32, (cols, wide), 0)
    j = lax.broadcasted_iota(jnp.int32, (cols, wide), 1)
    src_col = ((j >> lg(PIECE_GROUPS * unit)) << lg(unit)) + (j & (unit - 1))
    spread = jnp.where(k == src_col, 1.0, 0.0).astype(BF16)
    out = jnp.dot(src.astype(BF16), spread, preferred_element_type=F32)
    rg = (lax.broadcasted_iota(jnp.int32, (rows, wide), 0) >> lg(row_unit)) & (PIECE_GROUPS - 1)
    ch = (lax.broadcasted_iota(jnp.int32, (rows, wide), 1) >> lg(unit)) & (PIECE_GROUPS - 1)
    return jnp.where(rg == ch, out, 0.0).astype(BF16)


def _s5_expand_kernel(kc_ref, wi_ref, wo_ref, t_ref, win_ref, wout_ref, tsrc_scr, *, p, n):
    lane = lax.broadcasted_iota(jnp.int32, (p, LANES), 1)
    for gl in range(PIECE_GROUPS):
        kf = kc_ref[0, 0, gl]
        kb = kc_ref[1, 0, gl]
        for s in range(SUB):
            shift_b = (LANES - p * (SUB - 1 - s)) % LANES
            f = kf if s == 0 else pltpu.roll(kf, p * s, 1)
            r = kb if shift_b == 0 else pltpu.roll(kb, shift_b, 1)
            blk = jnp.where(lane >= p * s, f, 0.0) + jnp.where(lane < p * (s + 1), r, 0.0)
            tsrc_scr[pl.ds((s * PIECE_GROUPS + gl) * p, p), :] = blk
    t_ref[0] = _expand_block_diag(tsrc_scr[...], p, p)
    win_ref[0] = _expand_block_diag(wi_ref[0], n, p)
    wout_ref[0] = _expand_block_diag(wo_ref[0], p, n)


def s5_expand(kcat, win_src, wout_src, p, n):
    npiece = win_src.shape[0]
    rows_t = SUB * PIECE_GROUPS * p
    spec = lambda a, mult: pl.BlockSpec((1, a.shape[1], a.shape[2] * mult), lambda i: (i, 0, 0))
    shape = lambda a: jax.ShapeDtypeStruct((npiece, a.shape[1], a.shape[2] * PIECE_GROUPS), BF16)
    return pl.pallas_call(
        functools.partial(_s5_expand_kernel, p=p, n=n),
        grid=(npiece,),
        in_specs=[pl.BlockSpec((2, 1, PIECE_GROUPS, p, LANES), lambda i: (0, i, 0, 0, 0)),
                  spec(win_src, 1), spec(wout_src, 1)],
        out_specs=[pl.BlockSpec((1, rows_t, LANES * PIECE_GROUPS), lambda i: (i, 0, 0)),
                   spec(win_src, PIECE_GROUPS), spec(wout_src, PIECE_GROUPS)],
        out_shape=[jax.ShapeDtypeStruct((npiece, rows_t, LANES * PIECE_GROUPS), BF16),
                   shape(win_src), shape(wout_src)],
        scratch_shapes=[pltpu.VMEM((rows_t, LANES), F32)],
        compiler_params=_cparams("parallel"),
        name="s5_expand",
    )(kcat, win_src, wout_src)


def s5_derive(lam_re, lam_im, log_dt, b_re, b_im, c_re, c_im):
    hi = lax.Precision.HIGHEST
    n_dir, g, n = lam_re.shape
    p = b_re.shape[-1]
    npiece = g // PIECE_GROUPS
    assert n_dir == 2 and SUB * p == LANES
    dt = jnp.exp(log_dt)[:, :, None]
    up = list(range(SUB))
    down = up[::-1]

    def powers(exps):
        k = jnp.asarray(exps, F32)[:, :, None, None]
        mag = jnp.exp(k * (lam_re * dt)[:, None])
        ang = k * (lam_im * dt)[:, None]
        return mag * jnp.cos(ang), mag * jnp.sin(ang)

    a_re, a_im = powers([[1], [1]])
    den = lam_re * lam_re + lam_im * lam_im
    nr, ni = a_re[:, 0] - 1.0, a_im[:, 0]
    cf_re = ((nr * lam_re + ni * lam_im) / den)[:, :, None, :]
    cf_im = ((ni * lam_re - nr * lam_im) / den)[:, :, None, :]
    bt_re, bt_im = jnp.swapaxes(b_re, 2, 3), jnp.swapaxes(b_im, 2, 3)
    bb_re = cf_re * bt_re - cf_im * bt_im
    bb_im = cf_re * bt_im + cf_im * bt_re
    ct_re, ct_im = jnp.swapaxes(c_re, 2, 3), jnp.swapaxes(c_im, 2, 3)

    def c_times_powers(exps):
        e_re, e_im = powers(exps)
        e_re = jnp.transpose(e_re, (0, 2, 3, 1))[..., None]
        e_im = jnp.transpose(e_im, (0, 2, 3, 1))[..., None]
        cr, ci = ct_re[:, :, :, None, :], ct_im[:, :, :, None, :]
        return cr * e_re - ci * e_im, cr * e_im + ci * e_re

    ck_re, ck_im = c_times_powers([up, down])
    lhs = jnp.concatenate([bb_re, -bb_im], axis=-1)
    rhs = jnp.concatenate([ck_re, ck_im], axis=2).reshape(n_dir, g, 2 * n, SUB * p)
    kcat = jnp.einsum('dgqk,dgkm->dgqm', lhs, rhs, precision=hi)
    kcat = kcat.reshape(n_dir, npiece, PIECE_GROUPS, p, SUB * p)

    e_re, e_im = powers([down, up])
    pm = lambda x: x.reshape(n_dir, SUB, npiece, PIECE_GROUPS, n).transpose(2, 1, 3, 0, 4)[:, :, :, None]
    bm = lambda x: x.reshape(n_dir, npiece, PIECE_GROUPS, p, n).transpose(1, 2, 3, 0, 4)[:, None]
    v_re = pm(e_re) * bm(bb_re) - pm(e_im) * bm(bb_im)
    v_im = pm(e_re) * bm(bb_im) + pm(e_im) * bm(bb_re)
    win_src = jnp.stack([v_re, v_im], axis=-2).reshape(npiece, SUB * PIECE_GROUPS * p, n_dir * 2 * n)

    co_re, co_im = c_times_powers([[k + 1 for k in up], [SUB - k for k in up]])
    o = jnp.stack([co_re, -co_im], axis=1)
    o = o.reshape(n_dir, 2, npiece, PIECE_GROUPS * n, SUB * p).transpose(2, 0, 1, 3, 4)
    wout_src = o.reshape(npiece, n_dir * 2 * PIECE_GROUPS * n, SUB * p)

    steps = [SUB * m for m in range(1, SUBLANES + 1)]
    q_re, q_im = powers([steps, steps])
    q_re = q_re.reshape(n_dir, SUBLANES, g * n)
    q_im = q_im.reshape(n_dir, SUBLANES, g * n)
    row = jnp.arange(SUBLANES)
    consts = []
    for d in range(n_dir):
        kinds = []
        for dd in (1, 2, 4):
            keep = (row + dd <= SUBLANES - 1) if d == 1 else (row >= dd)
            m = keep.astype(F32)[:, None]
            kinds.append(jnp.stack([m * q_re[d, dd - 1][None], m * q_im[d, dd - 1][None]]))
        sel = (SUBLANES - 1 - row) if d == 1 else row
        kinds.append(jnp.stack([q_re[d][sel], q_im[d][sel]]))
        consts.append(jnp.stack(kinds))
    consts = jnp.stack(consts)
    t_intra, w_in, w_out = s5_expand(kcat, win_src, wout_src, p, n)
    return t_intra, w_in, w_out, consts


def _lru_kernel(*refs, bwd, nch, tc, width):
    if bwd:
        (x_ref, xp_ref, xn_ref, cw_ref, cb_ref, lam_ref, wa_ref, ba_ref, wx_ref, bx_ref,
         hf_ref, gd_ref, o_ref, pad_scr, a_scr, b_scr, carry_scr) = refs
    else:
        (x_ref, xp_ref, xn_ref, cw_ref, cb_ref, lam_ref, wa_ref, ba_ref, wx_ref, bx_ref,
         o_ref, pad_scr, a_scr, b_scr, carry_scr) = refs
    k = pl.program_id(1)
    chunk = jnp.where(k == 0, 0, nch - k) if bwd else k
    prev_ok = jnp.logical_and(chunk != 0, chunk != 1)
    next_ok = jnp.logical_and(chunk != 0, chunk != nch - 1)
    pad_scr[0:SUBLANES] = jnp.where(prev_ok, xp_ref[0, 0], 0.0)
    pad_scr[SUBLANES:SUBLANES + tc] = x_ref[0, 0]
    pad_scr[SUBLANES + tc:2 * SUBLANES + tc] = jnp.where(next_ok, xn_ref[0, 0], 0.0)
    xc = cb_ref[...]
    for tap in range(CONV_W):
        xc = xc + cw_ref[tap:tap + 1] * pad_scr[SUBLANES - 1 + tap:SUBLANES - 1 + tap + tc]

    bw = width // D_BLOCKS
    r_parts, i_parts = [], []
    for blk in range(D_BLOCKS):
        xb = xc[:, blk * bw:(blk + 1) * bw].astype(BF16)
        r_parts.append(jnp.dot(xb, wa_ref[blk], preferred_element_type=F32))
        i_parts.append(jnp.dot(xb, wx_ref[blk], preferred_element_type=F32))
    r = jax.nn.sigmoid(jnp.concatenate(r_parts, axis=1) + ba_ref[...])
    ig = jax.nn.sigmoid(jnp.concatenate(i_parts, axis=1) + bx_ref[...])
    nl = -lam_ref[...]
    e = jnp.exp(-jnp.abs(nl))
    e1 = 1.0 + e
    log1p_e = jnp.where(e1 == 1.0, e, jnp.log(e1) * (e / jnp.where(e1 == 1.0, 1.0, e1 - 1.0)))
    softplus = jnp.maximum(nl, 0.0) + log1p_e
    log_a = (-LRU_C * softplus) * r
    a_scr[...] = jnp.exp(log_a)
    th = jnp.tanh(log_a)
    b_scr[...] = jnp.sqrt(-2.0 * th / (1.0 - th)) * (ig * xc)

    @pl.when(k == 0)
    def _():
        carry_scr[...] = jnp.zeros_like(carry_scr)

    rowi = lax.broadcasted_iota(jnp.int32, (SUBLANES, width), 0)
    n_tiles = tc // SUBLANES

    def body(n, carry):
        ti = n_tiles - 1 - n if bwd else n
        rr = pl.ds(pl.multiple_of(ti * SUBLANES, SUBLANES), SUBLANES)
        a, b = a_scr[rr], b_scr[rr]
        for dd in (1, 2, 4):
            keep = (rowi + dd <= SUBLANES - 1) if bwd else (rowi >= dd)
            sh = SUBLANES - dd if bwd else dd
            b = b + a * jnp.where(keep, pltpu.roll(b, sh, 0), 0.0)
            a = a * jnp.where(keep, pltpu.roll(a, sh, 0), 1.0)
        h = b + a * carry
        b_scr[rr] = h
        return h[0:1] if bwd else h[SUBLANES - 1:SUBLANES]

    carry_scr[...] = lax.fori_loop(0, n_tiles, body, carry_scr[...])
    if bwd:
        o_ref[0] = ((hf_ref[0] + b_scr[...]) * _silu(gd_ref[0, 0])).astype(BF16)
    else:
        o_ref[0] = b_scr[...]


def lru_pass(slabs, conv_w, conv_b, lam, wa, ba, wx, bx, ctx_len, hf=None):
    _, b, t, w = slabs.shape
    bwd = hf is not None
    tc = ctx_len
    nch = t // tc
    hb = tc // SUBLANES
    nhb = t // SUBLANES

    def ch(k):
        return jnp.where(k == 0, 0, nch - k) if bwd else k

    in_specs = [pl.BlockSpec((1, 1, tc, w), lambda bb, k: (1, bb, ch(k), 0)),
                pl.BlockSpec((1, 1, SUBLANES, w), lambda bb, k: (1, bb, jnp.maximum(ch(k) * hb - 1, 0), 0)),
                pl.BlockSpec((1, 1, SUBLANES, w), lambda bb, k: (1, bb, jnp.minimum((ch(k) + 1) * hb, nhb - 1), 0)),
                pl.BlockSpec((CONV_W, w), lambda bb, k: (0, 0)),
                pl.BlockSpec((1, w), lambda bb, k: (0, 0)),
                pl.BlockSpec((1, w), lambda bb, k: (0, 0)),
                pl.BlockSpec(wa.shape, lambda bb, k: (0, 0, 0)),
                pl.BlockSpec((1, w), lambda bb, k: (0, 0)),
                pl.BlockSpec(wx.shape, lambda bb, k: (0, 0, 0)),
                pl.BlockSpec((1, w), lambda bb, k: (0, 0))]
    args = [slabs, slabs, slabs, conv_w, conv_b, lam, wa, ba, wx, bx]
    if bwd:
        in_specs += [pl.BlockSpec((1, tc, w), lambda bb, k: (bb, ch(k), 0)),
                     pl.BlockSpec((1, 1, tc, w), lambda bb, k: (3, bb, ch(k), 0))]
        args += [hf, slabs]
    return pl.pallas_call(
        functools.partial(_lru_kernel, bwd=bwd, nch=nch, tc=tc, width=w),
        grid=(b, nch),
        in_specs=in_specs,
        out_specs=pl.BlockSpec((1, tc, w), lambda bb, k: (bb, ch(k), 0)),
        out_shape=jax.ShapeDtypeStruct((b, t, w), BF16 if bwd else F32),
        scratch_shapes=[pltpu.VMEM((tc + 2 * SUBLANES, w), F32),
                        pltpu.VMEM((tc, w), F32),
                        pltpu.VMEM((tc, w), F32),
                        pltpu.VMEM((1, w), F32)],
        compiler_params=_cparams("parallel", "arbitrary"),
        name="lru_bwd" if bwd else "lru_fwd",
    )(*args)


def rope_tables(ctx_len, seq):
    rows = seq // GRID_W
    axis_dim = HEAD_DIM // 2
    row = jnp.repeat(jnp.arange(rows), GRID_W).astype(F32)
    col = jnp.tile(jnp.arange(GRID_W), rows).astype(F32)
    inv = ROPE_THETA ** (-jnp.arange(0, axis_dim, 2, dtype=F32) / axis_dim)
    ang = jnp.concatenate([row[:, None] * inv, col[:, None] * inv], axis=-1)
    cos, sin = jnp.cos(ang), jnp.sin(ang)
    cos_t = jnp.repeat(cos, 2, axis=-1)
    sin_t = jnp.stack([-sin, sin], axis=-1).reshape(seq, HEAD_DIM)
    cos_t = jnp.concatenate([jnp.ones((ctx_len, HEAD_DIM), F32), cos_t], axis=0)
    sin_t = jnp.concatenate([jnp.zeros((ctx_len, HEAD_DIM), F32), sin_t], axis=0)
    return cos_t, sin_t


def _slab_weights(w, tn):
    d, n = w.shape
    return w.astype(BF16).reshape(d, n // tn, tn).transpose(1, 0, 2)


def even_layer(x_all, mod, g, w_in, w_out, q_g, k_g, sgu_g, ws, bs, cos_t, sin_t, ctx_len, latent_only):
    shift_l, scale_l, gate_l, shift_c, scale_c, gate_c = mod
    slabs = inproj(x_all, shift_l, scale_l, shift_c, scale_c, g, _slab_weights(w_in, 512), ctx_len,
                   tm=_pick_tile(x_all.shape[1], 768), out_dtype=BF16)
    q_gain = (q_g * (HEAD_DIM ** -0.5 * math.log2(math.e)))[None, :]
    mix_a = attention(slabs, q_gain, k_g[None, :], cos_t, sin_t, ctx_len)
    mix_b = sgu(slabs, ws.astype(BF16), bs[:, :, None], sgu_g[None, :])
    return outproj(mix_a, mix_b, w_out.astype(BF16), x_all, gate_l, gate_c, ctx_len, latent_only)


def odd_layer(x_all, mod, g, w_in, w_out, s5p, d_skip, glu_w, glu_b, conv_w, conv_b, lam, wa, ba, wx, bx,
              ctx_len, latent_only):
    shift_l, scale_l, gate_l, shift_c, scale_c, gate_c = mod
    b, t, _ = x_all.shape
    slabs = inproj(x_all, shift_l, scale_l, shift_c, scale_c, g, _slab_weights(w_in, 1024), ctx_len,
                   tm=_pick_tile(t, 768), out_dtype=F32)
    cw = slabs.shape[3]
    rows = b * t // SUB
    t_intra, w_s5in, w_s5out, consts = s5_derive(*s5p)
    slabs2 = slabs.reshape(slabs.shape[0], b * t, cw)
    gmat = s5_in(slabs2, w_s5in, tm=_pick_tile(rows, 576))
    hp = s5_scan(gmat.reshape(b, t // SUB, -1), consts, ctx_len // SUB)
    y = s5_out(slabs2, hp.reshape(rows, -1), t_intra, w_s5out, tm=_pick_tile(rows, 576))
    mix_c = s5_finish(y, slabs2, d_skip[None, :], glu_w.astype(BF16), glu_b[None, :],
                      tm=_pick_tile(b * t, 512))
    hf = lru_pass(slabs, conv_w, conv_b[None, :], lam[0][None, :], wa[0].astype(BF16), ba[0][None, :],
                  wx[0].astype(BF16), bx[0][None, :], ctx_len)
    mix_d = lru_pass(slabs, conv_w, conv_b[None, :], lam[1][None, :], wa[1].astype(BF16), ba[1][None, :],
                     wx[1].astype(BF16), bx[1][None, :], ctx_len, hf=hf)
    return outproj(mix_c.reshape(b, t, cw), mix_d, w_out.astype(BF16), x_all, gate_l, gate_c,
                   ctx_len, latent_only)


def kernel(x, c, ctx, c_ctx, ada_w, ada_b, norm_g, ev_w_in, ev_w_out, ev_q_g, ev_k_g, ev_sgu_g, ev_ws, ev_bs,
           od_w_in, od_w_out, s5_lam_re, s5_lam_im, s5_log_dt, s5_b_re, s5_b_im, s5_c_re, s5_c_im, s5_d,
           s5_glu_w, s5_glu_b, lru_conv_w, lru_conv_b, lru_lam, lru_wa, lru_ba, lru_wx, lru_bx):
    b, seq, d = x.shape
    ctx_len = ctx.shape[1]
    depth = ada_w.shape[0]
    assert b + 1 <= SUBLANES and seq % ctx_len == 0
    cos_t, sin_t = rope_tables(ctx_len, seq)
    cc = jnp.zeros((SUBLANES, d), F32).at[:b].set(c).at[b].set(c_ctx)
    mods = adaln_all(cc, ada_w, ada_b)
    x_all = jnp.concatenate([ctx, x], axis=1)
    for layer in range(depth):
        m = mods[layer]
        mod = tuple(m[:b, None, k * d:(k + 1) * d] for k in range(3)) + \
            tuple(m[b:b + 1, k * d:(k + 1) * d] for k in range(3))
        last = layer == depth - 1
        j = layer // 2
        g = norm_g[layer][None, :]
        if layer % 2 == 0:
            x_all = even_layer(x_all, mod, g, ev_w_in[j], ev_w_out[j], ev_q_g[j], ev_k_g[j], ev_sgu_g[j],
                               ev_ws[j], ev_bs[j], cos_t, sin_t, ctx_len, last)
        else:
            s5p = (s5_lam_re[j], s5_lam_im[j], s5_log_dt[j], s5_b_re[j], s5_b_im[j], s5_c_re[j], s5_c_im[j])
            x_all = odd_layer(x_all, mod, g, od_w_in[j], od_w_out[j], s5p, s5_d[j], s5_glu_w[j], s5_glu_b[j],
                              lru_conv_w[j], lru_conv_b[j], lru_lam[j], lru_wa[j], lru_ba[j], lru_wx[j],
                              lru_bx[j], ctx_len, last)
    return x_all
```

```python
import functools
import math

import jax
import jax.numpy as jnp
from jax import lax
from jax.experimental import pallas as pl
from jax.experimental.pallas import tpu as pltpu

F32 = jnp.float32
BF16 = jnp.bfloat16

EPS = 1e-6
GRID_W = 64
HEAD_DIM = 128
A_Q_HEADS = 12
A_KV_HEADS = 4
A_GROUP = A_Q_HEADS // A_KV_HEADS
B_GROUPS = 4
CHUNK = 128
ROPE_THETA = 10000.0
C_GROUP_DIM = 16
C_STATE = 64
D_BLOCKS = 8
CONV_W = 4
LRU_C = 8.0

LANES = 128
SUBLANES = 8
SUB = 8
PIECE_GROUPS = LANES // C_GROUP_DIM
VMEM_LIMIT = 50 * 1024 * 1024


def _cparams(*sem):
    return pltpu.CompilerParams(dimension_semantics=sem, vmem_limit_bytes=VMEM_LIMIT)


def _sigmoid(x):
    return 0.5 * jnp.tanh(0.5 * x) + 0.5


def _silu(x):
    return x * _sigmoid(x)


def _pick_tile(rows, target, align=16):
    best = None
    for cand in range(align, min(rows, target) + 1, align):
        if rows % cand == 0:
            best = cand
    assert best is not None, (rows, target)
    return best


def _adaln_kernel(cc_ref, w_ref, b_ref, o_ref):
    s = _silu(cc_ref[...])
    o_ref[0] = jnp.dot(s, w_ref[0], preferred_element_type=F32,
                       precision=lax.Precision.HIGHEST) + b_ref[0]


def adaln_all(cc, ada_w, ada_b):
    depth, d, n3 = ada_w.shape
    tn = 512
    return pl.pallas_call(
        _adaln_kernel,
        grid=(depth, n3 // tn),
        in_specs=[pl.BlockSpec((8, d), lambda l, j: (0, 0)),
                  pl.BlockSpec((1, d, tn), lambda l, j: (l, 0, j)),
                  pl.BlockSpec((1, 1, tn), lambda l, j: (l, 0, j))],
        out_specs=pl.BlockSpec((1, 8, tn), lambda l, j: (l, 0, j)),
        out_shape=jax.ShapeDtypeStruct((depth, 8, n3), F32),
        compiler_params=_cparams("parallel", "parallel"),
        name="adaln",
    )(cc, ada_w, ada_b.reshape(depth, 1, n3))


def _inproj_kernel(x_ref, shl_ref, scl_ref, shc_ref, scc_ref, g_ref, w_ref, o_ref, h_scr, *, tm, ctx_len):
    i = pl.program_id(1)
    first = pl.program_id(2) == 0

    def modulate(rows, sc_ref, sh_ref):
        x = x_ref[0, rows]
        r = lax.rsqrt(jnp.mean(x * x, axis=-1, keepdims=True) + EPS)
        h_scr[rows] = ((x * r) * (g_ref[...] * (1.0 + sc_ref[...])) + sh_ref[...]).astype(BF16)

    scl, shl = scl_ref.at[0], shl_ref.at[0]
    if tm >= ctx_len:
        @pl.when(jnp.logical_and(first, i == 0))
        def _():
            modulate(slice(0, ctx_len), scc_ref, shc_ref)
            if tm > ctx_len:
                modulate(slice(ctx_len, tm), scl, shl)

        @pl.when(jnp.logical_and(first, i > 0))
        def _():
            modulate(slice(0, tm), scl, shl)
    else:
        @pl.when(jnp.logical_and(first, i < ctx_len // tm))
        def _():
            modulate(slice(0, tm), scc_ref, shc_ref)

        @pl.when(jnp.logical_and(first, i >= ctx_len // tm))
        def _():
            modulate(slice(0, tm), scl, shl)

    o_ref[0, 0] = jnp.dot(h_scr[...], w_ref[...], preferred_element_type=F32).astype(o_ref.dtype)


def inproj(x_all, shift_l, scale_l, shift_c, scale_c, g, w, ctx_len, tn, tm, out_dtype):
    b, t, d = x_all.shape
    n = w.shape[1]
    assert tm >= ctx_len or ctx_len % tm == 0
    return pl.pallas_call(
        functools.partial(_inproj_kernel, tm=tm, ctx_len=ctx_len),
        grid=(b, t // tm, n // tn),
        in_specs=[pl.BlockSpec((1, tm, d), lambda bb, i, j: (bb, i, 0)),
                  pl.BlockSpec((1, 1, d), lambda bb, i, j: (bb, 0, 0)),
                  pl.BlockSpec((1, 1, d), lambda bb, i, j: (bb, 0, 0)),
                  pl.BlockSpec((1, d), lambda bb, i, j: (0, 0)),
                  pl.BlockSpec((1, d), lambda bb, i, j: (0, 0)),
                  pl.BlockSpec((1, d), lambda bb, i, j: (0, 0)),
                  pl.BlockSpec((d, tn), lambda bb, i, j: (0, j))],
        out_specs=pl.BlockSpec((1, 1, tm, tn), lambda bb, i, j: (j, bb, i, 0)),
        out_shape=jax.ShapeDtypeStruct((n // tn, b, t, tn), out_dtype),
        scratch_shapes=[pltpu.VMEM((tm, d), BF16)],
        compiler_params=_cparams("parallel", "parallel", "arbitrary"),
        name="inproj",
    )(x_all, shift_l, scale_l, shift_c, scale_c, g, w)


def _outproj_kernel(ma_ref, mb_ref, w1_ref, w2_ref, x_ref, gl_ref, gc_ref, o_ref, *, ctx_tiles):
    i = pl.program_id(1)
    acc = jnp.dot(ma_ref[0], w1_ref[...], preferred_element_type=F32)
    acc = acc + jnp.dot(mb_ref[0], w2_ref[...], preferred_element_type=F32)
    gate = jnp.where(i < ctx_tiles, gc_ref[...], gl_ref[0])
    o_ref[0] = x_ref[0] + gate * acc


def outproj(ma, mb, w, x_all, gate_l, gate_c, ctx_len, latent_only):
    b, t, d = x_all.shape
    k1, k2 = ma.shape[2], mb.shape[2]
    assert w.shape[0] == k1 + k2 and k1 % k2 == 0
    tm = ctx_len
    off = 1 if latent_only else 0
    t_out = t - ctx_len if latent_only else t
    return pl.pallas_call(
        functools.partial(_outproj_kernel, ctx_tiles=0 if latent_only else 1),
        grid=(b, t_out // tm),
        in_specs=[pl.BlockSpec((1, tm, k1), lambda bb, i: (bb, i + off, 0)),
                  pl.BlockSpec((1, tm, k2), lambda bb, i: (bb, i + off, 0)),
                  pl.BlockSpec((k1, d), lambda bb, i: (0, 0)),
                  pl.BlockSpec((k2, d), lambda bb, i: (k1 // k2, 0)),
                  pl.BlockSpec((1, tm, d), lambda bb, i: (bb, i + off, 0)),
                  pl.BlockSpec((1, 1, d), lambda bb, i: (bb, 0, 0)),
                  pl.BlockSpec((1, d), lambda bb, i: (0, 0))],
        out_specs=pl.BlockSpec((1, tm, d), lambda bb, i: (bb, i, 0)),
        out_shape=jax.ShapeDtypeStruct((b, t_out, d), F32),
        compiler_params=_cparams("parallel", "parallel"),
        name="outproj",
    )(ma, mb, w, w, x_all, gate_l, gate_c)


def _norm_rope(x, gain, cos, sin):
    y = x * lax.rsqrt(jnp.mean(x * x, axis=-1, keepdims=True) + EPS) * gain
    lane = lax.broadcasted_iota(jnp.int32, y.shape, 1)
    partner = jnp.where((lane & 1) == 0, pltpu.roll(y, HEAD_DIM - 1, 1), pltpu.roll(y, 1, 1))
    return y * cos + partner * sin


def _attn_kernel(q0_ref, q1_ref, q2_ref, k_ref, v_ref, ga0_ref, ga1_ref, ga2_ref, qg_ref, kg_ref,
                 cos_ref, sin_ref, o_ref, k_scr, vaug_scr, *, ctx_len, tq):
    iq = pl.program_id(2)

    @pl.when(iq == 0)
    def _():
        k_scr[...] = _norm_rope(k_ref[0, 0].astype(F32), kg_ref[...], cos_ref[...], sin_ref[...]).astype(BF16)
        vaug_scr[:, :HEAD_DIM] = v_ref[0, 0].astype(BF16)
        vaug_scr[:, HEAD_DIM:] = jnp.ones((vaug_scr.shape[0], HEAD_DIM), BF16)

    def attend(n_keys):
        rows = pl.ds(pl.multiple_of(iq * tq, tq), tq)
        cos_q, sin_q = cos_ref[rows, :], sin_ref[rows, :]
        k = k_scr[:n_keys]
        vaug = vaug_scr[:n_keys]
        for g, (q_ref, ga_ref) in enumerate(((q0_ref, ga0_ref), (q1_ref, ga1_ref), (q2_ref, ga2_ref))):
            q = _norm_rope(q_ref[0, 0].astype(F32), qg_ref[...], cos_q, sin_q).astype(BF16)
            s = lax.dot_general(q, k, (((1,), (1,)), ((), ())), preferred_element_type=F32)
            p = jnp.exp2(s - jnp.max(s, axis=-1, keepdims=True)).astype(BF16)
            pv = jnp.dot(p, vaug, preferred_element_type=F32)
            o = pv[:, :HEAD_DIM] / pv[:, HEAD_DIM:]
            o_ref[0, :, g * HEAD_DIM:(g + 1) * HEAD_DIM] = (o * _silu(ga_ref[0, 0].astype(F32))).astype(BF16)

    @pl.when(iq == 0)
    def _():
        attend(ctx_len)

    @pl.when(iq > 0)
    def _():
        attend(k_scr.shape[0])


def attention(slabs, q_gain, k_gain, cos_t, sin_t, ctx_len):
    _, b, t, tn = slabs.shape
    per = tn // HEAD_DIM
    tq = ctx_len
    gw = A_GROUP * HEAD_DIM
    assert A_GROUP == 3

    def head_spec(first_slab, g):
        def head_map(bb, kv, iq):
            h = kv * A_GROUP + g
            return (first_slab + h // per, bb, iq, h % per)
        return pl.BlockSpec((1, 1, tq, HEAD_DIM), head_map)

    vec = pl.BlockSpec((1, HEAD_DIM), lambda bb, kv, iq: (0, 0))
    table = pl.BlockSpec((t, HEAD_DIM), lambda bb, kv, iq: (0, 0))
    return pl.pallas_call(
        functools.partial(_attn_kernel, ctx_len=ctx_len, tq=tq),
        grid=(b, A_KV_HEADS, t // tq),
        in_specs=[head_spec(2, 0), head_spec(2, 1), head_spec(2, 2),
                  pl.BlockSpec((1, 1, t, HEAD_DIM), lambda bb, kv, iq: (0, bb, 0, kv)),
                  pl.BlockSpec((1, 1, t, HEAD_DIM), lambda bb, kv, iq: (1, bb, 0, kv)),
                  head_spec(7, 0), head_spec(7, 1), head_spec(7, 2),
                  vec, vec, table, table],
        out_specs=pl.BlockSpec((1, tq, gw), lambda bb, kv, iq: (bb, iq, kv)),
        out_shape=jax.ShapeDtypeStruct((b, t, A_Q_HEADS * HEAD_DIM), BF16),
        scratch_shapes=[pltpu.VMEM((t, HEAD_DIM), BF16), pltpu.VMEM((t, 2 * HEAD_DIM), BF16)],
        compiler_params=_cparams("parallel", "parallel", "arbitrary"),
        name="attention",
    )(slabs, slabs, slabs, slabs, slabs, slabs, slabs, slabs, q_gain, k_gain, cos_t, sin_t)


def _sgu_kernel(u_ref, v_ref, gb_ref, ws_ref, bs_ref, g_ref, o_ref):
    outs = []
    for grp in range(B_GROUPS):
        cols = slice(grp * LANES, (grp + 1) * LANES)
        v = v_ref[0, 0, :, cols].astype(F32)
        vn = v * lax.rsqrt(jnp.mean(v * v, axis=-1, keepdims=True) + EPS) * g_ref[:, cols]
        mixed = jnp.dot(ws_ref[grp], vn.astype(BF16), preferred_element_type=F32) + bs_ref[grp]
        outs.append(u_ref[0, 0, :, cols].astype(F32) * mixed * _silu(gb_ref[0, 0, :, cols].astype(F32)))
    o_ref[0] = jnp.concatenate(outs, axis=1).astype(BF16)


def sgu(slabs, ws, bs, g):
    _, b, t, tn = slabs.shape
    spec = lambda s: pl.BlockSpec((1, 1, CHUNK, tn), lambda bb, c: (s, bb, c, 0))
    return pl.pallas_call(
        _sgu_kernel,
        grid=(b, t // CHUNK),
        in_specs=[spec(5), spec(6), spec(10),
                  pl.BlockSpec((B_GROUPS, CHUNK, CHUNK), lambda bb, c: (0, 0, 0)),
                  pl.BlockSpec((B_GROUPS, CHUNK, 1), lambda bb, c: (0, 0, 0)),
                  pl.BlockSpec((1, tn), lambda bb, c: (0, 0))],
        out_specs=pl.BlockSpec((1, CHUNK, tn), lambda bb, c: (bb, c, 0)),
        out_shape=jax.ShapeDtypeStruct((b, t, tn), BF16),
        compiler_params=_cparams("parallel", "parallel"),
        name="sgu",
    )(slabs, slabs, slabs, ws, bs, g)


def _s5_fold_rows(u_ref, tm):
    return jnp.concatenate([u_ref[pl.ds(t, tm, stride=SUB), :] for t in range(SUB)], axis=1)


def _s5_in_kernel(u_ref, w_ref, o_ref, *, tm):
    x = _s5_fold_rows(u_ref, tm).astype(BF16)
    o_ref[...] = jnp.dot(x, w_ref[0], preferred_element_type=F32)


def s5_in(slabs2, w_in, tm):
    _, r, _ = slabs2.shape
    m = r // SUB
    n_piece = w_in.shape[0]
    ncol = w_in.shape[2]
    return pl.pallas_call(
        functools.partial(_s5_in_kernel, tm=tm),
        grid=(n_piece, m // tm),
        in_specs=[pl.BlockSpec((None, tm * SUB, LANES), lambda i, rr: (0, rr, i)),
                  pl.BlockSpec((1, w_in.shape[1], ncol), lambda i, rr: (i, 0, 0))],
        out_specs=pl.BlockSpec((tm, ncol), lambda i, rr: (rr, i)),
        out_shape=jax.ShapeDtypeStruct((m, n_piece * ncol), F32),
        compiler_params=_cparams("parallel", "parallel"),
        name="s5_in",
    )(slabs2, w_in)


def _s5_scan_kernel(g_ref, c_ref, o_ref, *, ctx_pairs, all_pairs, half):
    rows = 2 * SUBLANES
    rowi = lax.broadcasted_iota(jnp.int32, (SUBLANES, half), 0)

    def tile(d, g_re, g_im, c_re, c_im):
        bwd = d == 1
        b_re, b_im = g_re, g_im
        for idx, dd in enumerate((1, 2, 4)):
            a_re, a_im = c_ref[d, idx, 0], c_ref[d, idx, 1]
            sh = SUBLANES - dd if bwd else dd
            s_re, s_im = pltpu.roll(b_re, sh, 0), pltpu.roll(b_im, sh, 0)
            b_re, b_im = (b_re + a_re * s_re - a_im * s_im,
                          b_im + a_re * s_im + a_im * s_re)
        a_re, a_im = c_ref[d, 3, 0], c_ref[d, 3, 1]
        h_re = b_re + a_re * c_re - a_im * c_im
        h_im = b_im + a_re * c_im + a_im * c_re
        if bwd:
            p_re = jnp.where(rowi == SUBLANES - 1, c_re, pltpu.roll(h_re, SUBLANES - 1, 0))
            p_im = jnp.where(rowi == SUBLANES - 1, c_im, pltpu.roll(h_im, SUBLANES - 1, 0))
            return p_re, p_im, h_re[0:1], h_im[0:1]
        p_re = jnp.where(rowi == 0, c_re, pltpu.roll(h_re, 1, 0))
        p_im = jnp.where(rowi == 0, c_im, pltpu.roll(h_im, 1, 0))
        return p_re, p_im, h_re[SUBLANES - 1:SUBLANES], h_im[SUBLANES - 1:SUBLANES]

    def pair(d, pi, carry):
        c_re, c_im = carry
        r0 = pl.multiple_of(pi * rows, rows)
        base = d * 2 * half
        order = (1, 0) if d == 1 else (0, 1)
        res = [None, None]
        for which in order:
            rr = pl.ds(r0 + which * SUBLANES, SUBLANES)
            g_re = g_ref[0, rr, pl.ds(base, half)]
            g_im = g_ref[0, rr, pl.ds(base + half, half)]
            p_re, p_im, c_re, c_im = tile(d, g_re, g_im, c_re, c_im)
            res[which] = (p_re, p_im)
        o_ref[0, pl.ds(r0, rows), pl.ds(base, half)] = jnp.concatenate(
            [res[0][0], res[1][0]], axis=0).astype(BF16)
        o_ref[0, pl.ds(r0, rows), pl.ds(base + half, half)] = jnp.concatenate(
            [res[0][1], res[1][1]], axis=0).astype(BF16)
        return c_re, c_im

    zero = (jnp.zeros((1, half), F32), jnp.zeros((1, half), F32))
    lax.fori_loop(0, all_pairs, lambda pi, c: pair(0, pi, c), zero)
    c = lax.fori_loop(0, ctx_pairs, lambda n, c: pair(1, ctx_pairs - 1 - n, c), zero)
    lax.fori_loop(0, all_pairs - ctx_pairs, lambda n, c: pair(1, all_pairs - 1 - n, c), c)


def s5_scan(g3, consts, ctx_rows):
    b, rows, wide = g3.shape
    half = PIECE_GROUPS * C_STATE
    n_piece = wide // (4 * half)
    return pl.pallas_call(
        functools.partial(_s5_scan_kernel, ctx_pairs=ctx_rows // (2 * SUBLANES),
                          all_pairs=rows // (2 * SUBLANES), half=half),
        grid=(b, n_piece),
        in_specs=[pl.BlockSpec((1, rows, 4 * half), lambda bb, i: (bb, 0, i)),
                  pl.BlockSpec((2, 4, 2, SUBLANES, half), lambda bb, i: (0, 0, 0, 0, i))],
        out_specs=pl.BlockSpec((1, rows, 4 * half), lambda bb, i: (bb, 0, i)),
        out_shape=jax.ShapeDtypeStruct((b, rows, wide), BF16),
        compiler_params=_cparams("parallel", "parallel"),
        name="s5_scan",
    )(g3, consts)


def _s5_out_kernel(u_ref, hp_ref, t_ref, w_ref, y_ref, *, tm):
    x = _s5_fold_rows(u_ref, tm).astype(BF16)
    y = jnp.dot(x, t_ref[0], preferred_element_type=F32)
    y = y + jnp.dot(hp_ref[...], w_ref[0], preferred_element_type=F32)
    for t in range(SUB):
        y_ref[pl.ds(t, tm, stride=SUB), :] = y[:, t * LANES:(t + 1) * LANES]


def s5_out(slabs2, hp2, t_intra, w_out, tm):
    _, r, width = slabs2.shape
    m = r // SUB
    n_piece = t_intra.shape[0]
    kh = w_out.shape[1]
    return pl.pallas_call(
        functools.partial(_s5_out_kernel, tm=tm),
        grid=(n_piece, m // tm),
        in_specs=[pl.BlockSpec((None, tm * SUB, LANES), lambda i, rr: (0, rr, i)),
                  pl.BlockSpec((tm, kh), lambda i, rr: (rr, i)),
                  pl.BlockSpec((1,) + t_intra.shape[1:], lambda i, rr: (i, 0, 0)),
                  pl.BlockSpec((1,) + w_out.shape[1:], lambda i, rr: (i, 0, 0))],
        out_specs=pl.BlockSpec((tm * SUB, LANES), lambda i, rr: (rr, i)),
        out_shape=jax.ShapeDtypeStruct((r, width), F32),
        compiler_params=_cparams("parallel", "parallel"),
        name="s5_out",
    )(slabs2, hp2, t_intra, w_out)


def _s5_finish_kernel(y_ref, u_ref, gc_ref, d_ref, w_ref, b_ref, o_ref):
    y = y_ref[...] + d_ref[...] * u_ref[0]
    y = jax.nn.gelu(y)
    z = jnp.dot(y.astype(BF16), w_ref[...], preferred_element_type=F32) + b_ref[...]
    o_ref[...] = (y * _sigmoid(z) * _silu(gc_ref[0])).astype(BF16)


def s5_finish(y, slabs2, d_skip, glu_w, glu_b, tm):
    r, w = y.shape
    return pl.pallas_call(
        _s5_finish_kernel,
        grid=(r // tm,),
        in_specs=[pl.BlockSpec((tm, w), lambda i: (i, 0)),
                  pl.BlockSpec((1, tm, w), lambda i: (0, i, 0)),
                  pl.BlockSpec((1, tm, w), lambda i: (2, i, 0)),
                  pl.BlockSpec((1, w), lambda i: (0, 0)),
                  pl.BlockSpec((w, w), lambda i: (0, 0)),
                  pl.BlockSpec((1, w), lambda i: (0, 0))],
        out_specs=pl.BlockSpec((tm, w), lambda i: (i, 0)),
        out_shape=jax.ShapeDtypeStruct((r, w), BF16),
        compiler_params=_cparams("parallel"),
        name="s5_finish",
    )(y, slabs2, slabs2, d_skip, glu_w, glu_b)


def _expand_block_diag(src, unit, row_unit):
    rows, cols = src.shape
    wide = cols * PIECE_GROUPS
    lg = lambda v: int(math.log2(v))
    k = lax.broadcasted_iota(jnp.int32, (cols, wide), 0)
    j = lax.broadcasted_iota(jnp.int32, (cols, wide), 1)
    src_col = ((j >> lg(PIECE_GROUPS * unit)) << lg(unit)) + (j & (unit - 1))
    spread = jnp.where(k == src_col, 1.0, 0.0).astype(BF16)
    out = jnp.dot(src.astype(BF16), spread, preferred_element_type=F32)
    rg = (lax.broadcasted_iota(jnp.int32, (rows, wide), 0) >> lg(row_unit)) & (PIECE_GROUPS - 1)
    ch = (lax.broadcasted_iota(jnp.int32, (rows, wide), 1) >> lg(unit)) & (PIECE_GROUPS - 1)
    return jnp.where(rg == ch, out, 0.0).astype(BF16)


def _s5_expand_kernel(kc_ref, wf_ref, wb_ref, wo_ref, t_ref, win_ref, wout_ref, tsrc_scr, *, p, n):
    lane = lax.broadcasted_iota(jnp.int32, (p, LANES), 1)
    for gl in range(PIECE_GROUPS):
        kf = kc_ref[0, 0, gl]
        kb = kc_ref[1, 0, gl]
        for s in range(SUB):
            shift_b = (LANES - p * (SUB - 1 - s)) % LANES
            f = kf if s == 0 else pltpu.roll(kf, p * s, 1)
            r = kb if shift_b == 0 else pltpu.roll(kb, shift_b, 1)
            blk = jnp.where(lane >= p * s, f, 0.0) + jnp.where(lane < p * (s + 1), r, 0.0)
            tsrc_scr[pl.ds((s * PIECE_GROUPS + gl) * p, p), :] = blk
    t_ref[0] = _expand_block_diag(tsrc_scr[...], p, p)
    half = win_ref.shape[2] // 2
    win_ref[0, :, :half] = _expand_block_diag(wf_ref[0], n, p)
    win_ref[0, :, half:] = _expand_block_diag(wb_ref[0], n, p)
    wout_ref[0] = _expand_block_diag(wo_ref[0], p, n)


def s5_expand(kcat, win_f, win_b, wout_src, p, n):
    npiece = win_f.shape[0]
    rows_t = SUB * PIECE_GROUPS * p
    spec = lambda a, mult: pl.BlockSpec((1, a.shape[1], a.shape[2] * mult), lambda i: (i, 0, 0))
    wide = lambda a, mult: jax.ShapeDtypeStruct((npiece, a.shape[1], a.shape[2] * mult), BF16)
    return pl.pallas_call(
        functools.partial(_s5_expand_kernel, p=p, n=n),
        grid=(npiece,),
        in_specs=[pl.BlockSpec((2, 1, PIECE_GROUPS, p, LANES), lambda i: (0, i, 0, 0, 0)),
                  spec(win_f, 1), spec(win_b, 1), spec(wout_src, 1)],
        out_specs=[pl.BlockSpec((1, rows_t, LANES * PIECE_GROUPS), lambda i: (i, 0, 0)),
                   spec(win_f, 2 * PIECE_GROUPS), spec(wout_src, PIECE_GROUPS)],
        out_shape=[jax.ShapeDtypeStruct((npiece, rows_t, LANES * PIECE_GROUPS), BF16),
                   wide(win_f, 2 * PIECE_GROUPS), wide(wout_src, PIECE_GROUPS)],
        scratch_shapes=[pltpu.VMEM((rows_t, LANES), F32)],
        compiler_params=_cparams("parallel"),
        name="s5_expand",
    )(kcat, win_f, win_b, wout_src)


def s5_derive(lam_re, lam_im, log_dt, b_re, b_im, c_re, c_im):
    hi = lax.Precision.HIGHEST
    n_dir, g, n = lam_re.shape
    p = b_re.shape[-1]
    npiece = g // PIECE_GROUPS
    assert n_dir == 2 and SUB * p == LANES
    dt = jnp.exp(log_dt)[:, :, None]
    up = list(range(SUB))
    down = up[::-1]

    def powers(exps):
        k = jnp.asarray(exps, F32)[:, :, None, None]
        mag = jnp.exp(k * (lam_re * dt)[:, None])
        ang = k * (lam_im * dt)[:, None]
        return mag * jnp.cos(ang), mag * jnp.sin(ang)

    a_re, a_im = powers([[1], [1]])
    den = lam_re * lam_re + lam_im * lam_im
    nr, ni = a_re[:, 0] - 1.0, a_im[:, 0]
    cf_re = ((nr * lam_re + ni * lam_im) / den)[:, :, None, :]
    cf_im = ((ni * lam_re - nr * lam_im) / den)[:, :, None, :]
    bt_re, bt_im = jnp.swapaxes(b_re, 2, 3), jnp.swapaxes(b_im, 2, 3)
    bb_re = cf_re * bt_re - cf_im * bt_im
    bb_im = cf_re * bt_im + cf_im * bt_re
    ct_re, ct_im = jnp.swapaxes(c_re, 2, 3), jnp.swapaxes(c_im, 2, 3)

    def c_times_powers(exps):
        e_re, e_im = powers(exps)
        e_re = jnp.transpose(e_re, (0, 2, 3, 1))[..., None]
        e_im = jnp.transpose(e_im, (0, 2, 3, 1))[..., None]
        cr, ci = ct_re[:, :, :, None, :], ct_im[:, :, :, None, :]
        return cr * e_re - ci * e_im, cr * e_im + ci * e_re

    cp_re, cp_im = c_times_powers([up + [k + 1 for k in up], down + [SUB - k for k in up]])
    ck_re, ck_im, co_re, co_im = cp_re[:, :, :, :SUB], cp_im[:, :, :, :SUB], cp_re[:, :, :, SUB:], cp_im[:, :, :, SUB:]

    lhs = jnp.concatenate([bb_re, -bb_im], axis=-1)
    rhs = jnp.concatenate([ck_re, ck_im], axis=2).reshape(n_dir, g, 2 * n, SUB * p)
    kcat = jnp.einsum('dgqk,dgkm->dgqm', lhs, rhs, precision=hi)
    kcat = kcat.reshape(n_dir, npiece, PIECE_GROUPS, p, SUB * p)

    e_re, e_im = powers([down, up])
    e1 = jnp.concatenate([e_re, e_re], axis=-1).reshape(n_dir, SUB, npiece, PIECE_GROUPS, 1, 2 * n)
    e2 = jnp.concatenate([-e_im, e_im], axis=-1).reshape(n_dir, SUB, npiece, PIECE_GROUPS, 1, 2 * n)
    b1 = jnp.concatenate([bb_re, bb_im], axis=-1).reshape(n_dir, 1, npiece, PIECE_GROUPS, p, 2 * n)
    b2 = jnp.concatenate([bb_im, bb_re], axis=-1).reshape(n_dir, 1, npiece, PIECE_GROUPS, p, 2 * n)
    win = (e1 * b1 + e2 * b2).transpose(0, 2, 1, 3, 4, 5)
    win = win.reshape(n_dir, npiece, SUB * PIECE_GROUPS * p, 2 * n)

    o = jnp.stack([co_re, -co_im], axis=1)
    o = o.reshape(n_dir, 2, npiece, PIECE_GROUPS * n, SUB * p).transpose(2, 0, 1, 3, 4)
    wout_src = o.reshape(npiece, n_dir * 2 * PIECE_GROUPS * n, SUB * p)

    steps = [SUB * m for m in range(1, SUBLANES + 1)]
    q_re, q_im = powers([steps, steps])
    q_re = q_re.reshape(n_dir, SUBLANES, g * n)
    q_im = q_im.reshape(n_dir, SUBLANES, g * n)
    row = jnp.arange(SUBLANES)
    consts = []
    for d in range(n_dir):
        kinds = []
        for dd in (1, 2, 4):
            keep = (row + dd <= SUBLANES - 1) if d == 1 else (row >= dd)
            m = keep.astype(F32)[:, None]
            kinds.append(jnp.stack([m * q_re[d, dd - 1][None], m * q_im[d, dd - 1][None]]))
        sel = (SUBLANES - 1 - row) if d == 1 else row
        kinds.append(jnp.stack([q_re[d][sel], q_im[d][sel]]))
        consts.append(jnp.stack(kinds))
    consts = jnp.stack(consts)
    return kcat, win[0], win[1], wout_src, consts


def _lru_kernel(*refs, bwd, nch, tc, width):
    if bwd:
        (x_ref, xp_ref, xn_ref, cw_ref, cb_ref, lam_ref, wa_ref, ba_ref, wx_ref, bx_ref,
         hf_ref, gd_ref, o_ref, pad_scr, a_scr, b_scr, carry_scr) = refs
    else:
        (x_ref, xp_ref, xn_ref, cw_ref, cb_ref, lam_ref, wa_ref, ba_ref, wx_ref, bx_ref,
         o_ref, pad_scr, a_scr, b_scr, carry_scr) = refs
    k = pl.program_id(1)
    chunk = jnp.where(k == 0, 0, nch - k) if bwd else k
    prev_ok = jnp.logical_and(chunk != 0, chunk != 1)
    next_ok = jnp.logical_and(chunk != 0, chunk != nch - 1)
    pad_scr[0:SUBLANES] = jnp.where(prev_ok, xp_ref[0, 0], 0.0)
    pad_scr[SUBLANES:SUBLANES + tc] = x_ref[0, 0]
    pad_scr[SUBLANES + tc:2 * SUBLANES + tc] = jnp.where(next_ok, xn_ref[0, 0], 0.0)
    xc = cb_ref[...]
    for tap in range(CONV_W):
        xc = xc + cw_ref[tap:tap + 1] * pad_scr[SUBLANES - 1 + tap:SUBLANES - 1 + tap + tc]

    bw = width // D_BLOCKS
    r_parts, i_parts = [], []
    for blk in range(D_BLOCKS):
        xb = xc[:, blk * bw:(blk + 1) * bw].astype(BF16)
        r_parts.append(jnp.dot(xb, wa_ref[blk], preferred_element_type=F32))
        i_parts.append(jnp.dot(xb, wx_ref[blk], preferred_element_type=F32))
    r = _sigmoid(jnp.concatenate(r_parts, axis=1) + ba_ref[...])
    ig = _sigmoid(jnp.concatenate(i_parts, axis=1) + bx_ref[...])
    nl = -lam_ref[...]
    e = jnp.exp(-jnp.abs(nl))
    e1 = 1.0 + e
    log1p_e = jnp.where(e1 == 1.0, e, jnp.log(e1) * (e / jnp.where(e1 == 1.0, 1.0, e1 - 1.0)))
    softplus = jnp.maximum(nl, 0.0) + log1p_e
    log_a = (-LRU_C * softplus) * r
    a = jnp.exp(log_a)
    a_scr[...] = a
    b_scr[...] = jnp.sqrt(-jnp.tanh(log_a) * (1.0 + a * a)) * (ig * xc)

    @pl.when(k == 0)
    def _():
        carry_scr[...] = jnp.zeros_like(carry_scr)

    rowi = lax.broadcasted_iota(jnp.int32, (SUBLANES, width), 0)
    n_tiles = tc // SUBLANES

    def body(n, carry):
        ti = n_tiles - 1 - n if bwd else n
        rr = pl.ds(pl.multiple_of(ti * SUBLANES, SUBLANES), SUBLANES)
        a, b = a_scr[rr], b_scr[rr]
        for dd in (1, 2, 4):
            keep = (rowi + dd <= SUBLANES - 1) if bwd else (rowi >= dd)
            sh = SUBLANES - dd if bwd else dd
            b = b + a * jnp.where(keep, pltpu.roll(b, sh, 0), 0.0)
            a = a * jnp.where(keep, pltpu.roll(a, sh, 0), 1.0)
        h = b + a * carry
        b_scr[rr] = h
        return h[0:1] if bwd else h[SUBLANES - 1:SUBLANES]

    carry_scr[...] = lax.fori_loop(0, n_tiles, body, carry_scr[...])
    if bwd:
        o_ref[0] = ((hf_ref[0] + b_scr[...]) * _silu(gd_ref[0, 0])).astype(BF16)
    else:
        o_ref[0] = b_scr[...]


def lru_pass(slabs, conv_w, conv_b, lam, wa, ba, wx, bx, ctx_len, hf=None):
    _, b, t, w = slabs.shape
    bwd = hf is not None
    tc = ctx_len
    nch = t // tc
    hb = tc // SUBLANES
    nhb = t // SUBLANES

    def ch(k):
        return jnp.where(k == 0, 0, nch - k) if bwd else k

    in_specs = [pl.BlockSpec((1, 1, tc, w), lambda bb, k: (1, bb, ch(k), 0)),
                pl.BlockSpec((1, 1, SUBLANES, w), lambda bb, k: (1, bb, jnp.maximum(ch(k) * hb - 1, 0), 0)),
                pl.BlockSpec((1, 1, SUBLANES, w), lambda bb, k: (1, bb, jnp.minimum((ch(k) + 1) * hb, nhb - 1), 0)),
                pl.BlockSpec((CONV_W, w), lambda bb, k: (0, 0)),
                pl.BlockSpec((1, w), lambda bb, k: (0, 0)),
                pl.BlockSpec((1, w), lambda bb, k: (0, 0)),
                pl.BlockSpec(wa.shape, lambda bb, k: (0, 0, 0)),
                pl.BlockSpec((1, w), lambda bb, k: (0, 0)),
                pl.BlockSpec(wx.shape, lambda bb, k: (0, 0, 0)),
                pl.BlockSpec((1, w), lambda bb, k: (0, 0))]
    args = [slabs, slabs, slabs, conv_w, conv_b, lam, wa, ba, wx, bx]
    if bwd:
        in_specs += [pl.BlockSpec((1, tc, w), lambda bb, k: (bb, ch(k), 0)),
                     pl.BlockSpec((1, 1, tc, w), lambda bb, k: (3, bb, ch(k), 0))]
        args += [hf, slabs]
    return pl.pallas_call(
        functools.partial(_lru_kernel, bwd=bwd, nch=nch, tc=tc, width=w),
        grid=(b, nch),
        in_specs=in_specs,
        out_specs=pl.BlockSpec((1, tc, w), lambda bb, k: (bb, ch(k), 0)),
        out_shape=jax.ShapeDtypeStruct((b, t, w), BF16 if bwd else F32),
        scratch_shapes=[pltpu.VMEM((tc + 2 * SUBLANES, w), F32),
                        pltpu.VMEM((tc, w), F32),
                        pltpu.VMEM((tc, w), F32),
                        pltpu.VMEM((1, w), F32)],
        compiler_params=_cparams("parallel", "arbitrary"),
        name="lru_bwd" if bwd else "lru_fwd",
    )(*args)


def rope_tables(ctx_len, seq):
    rows = seq // GRID_W
    axis_dim = HEAD_DIM // 2
    row = jnp.repeat(jnp.arange(rows), GRID_W).astype(F32)
    col = jnp.tile(jnp.arange(GRID_W), rows).astype(F32)
    inv = ROPE_THETA ** (-jnp.arange(0, axis_dim, 2, dtype=F32) / axis_dim)
    ang = jnp.concatenate([row[:, None] * inv, col[:, None] * inv], axis=-1)
    cos, sin = jnp.cos(ang), jnp.sin(ang)
    cos_t = jnp.repeat(cos, 2, axis=-1)
    sin_t = jnp.stack([-sin, sin], axis=-1).reshape(seq, HEAD_DIM)
    cos_t = jnp.concatenate([jnp.ones((ctx_len, HEAD_DIM), F32), cos_t], axis=0)
    sin_t = jnp.concatenate([jnp.zeros((ctx_len, HEAD_DIM), F32), sin_t], axis=0)
    return cos_t, sin_t


def even_layer(x_all, mod, g, w_in, w_out, q_g, k_g, sgu_g, ws, bs, cos_t, sin_t, ctx_len, latent_only):
    shift_l, scale_l, gate_l, shift_c, scale_c, gate_c = mod
    slabs = inproj(x_all, shift_l, scale_l, shift_c, scale_c, g, w_in.astype(BF16), ctx_len, tn=512,
                   tm=_pick_tile(x_all.shape[1], 1152), out_dtype=BF16)
    q_gain = (q_g * (HEAD_DIM ** -0.5 * math.log2(math.e)))[None, :]
    mix_a = attention(slabs, q_gain, k_g[None, :], cos_t, sin_t, ctx_len)
    mix_b = sgu(slabs, ws.astype(BF16), bs[:, :, None], sgu_g[None, :])
    return outproj(mix_a, mix_b, w_out.astype(BF16), x_all, gate_l, gate_c, ctx_len, latent_only)


def odd_layer(x_all, mod, g, w_in, w_out, s5_ops, d_skip, glu_w, glu_b, conv_w, conv_b, lam, wa, ba, wx, bx,
              ctx_len, latent_only):
    shift_l, scale_l, gate_l, shift_c, scale_c, gate_c = mod
    b, t, _ = x_all.shape
    slabs = inproj(x_all, shift_l, scale_l, shift_c, scale_c, g, w_in.astype(BF16), ctx_len, tn=1024,
                   tm=_pick_tile(t, 1152), out_dtype=F32)
    cw = slabs.shape[3]
    rows = b * t // SUB
    kcat, win_f, win_b, wout_src, consts = s5_ops
    t_intra, w_s5in, w_s5out = s5_expand(kcat, win_f, win_b, wout_src, kcat.shape[-2], C_STATE)
    slabs2 = slabs.reshape(slabs.shape[0], b * t, cw)
    gmat = s5_in(slabs2, w_s5in, tm=_pick_tile(rows, 576))
    hp = s5_scan(gmat.reshape(b, t // SUB, -1), consts, ctx_len // SUB)
    y = s5_out(slabs2, hp.reshape(rows, -1), t_intra, w_s5out, tm=_pick_tile(rows, 576))
    mix_c = s5_finish(y, slabs2, d_skip[None, :], glu_w.astype(BF16), glu_b[None, :],
                      tm=_pick_tile(b * t, 512))
    hf = lru_pass(slabs, conv_w, conv_b[None, :], lam[0][None, :], wa[0].astype(BF16), ba[0][None, :],
                  wx[0].astype(BF16), bx[0][None, :], ctx_len)
    mix_d = lru_pass(slabs, conv_w, conv_b[None, :], lam[1][None, :], wa[1].astype(BF16), ba[1][None, :],
                     wx[1].astype(BF16), bx[1][None, :], ctx_len, hf=hf)
    return outproj(mix_c.reshape(b, t, cw), mix_d, w_out.astype(BF16), x_all, gate_l, gate_c,
                   ctx_len, latent_only)


def kernel(x, c, ctx, c_ctx, ada_w, ada_b, norm_g, ev_w_in, ev_w_out, ev_q_g, ev_k_g, ev_sgu_g, ev_ws, ev_bs,
           od_w_in, od_w_out, s5_lam_re, s5_lam_im, s5_log_dt, s5_b_re, s5_b_im, s5_c_re, s5_c_im, s5_d,
           s5_glu_w, s5_glu_b, lru_conv_w, lru_conv_b, lru_lam, lru_wa, lru_ba, lru_wx, lru_bx):
    b, seq, d = x.shape
    ctx_len = ctx.shape[1]
    depth = ada_w.shape[0]
    assert b + 1 <= SUBLANES and seq % ctx_len == 0
    cos_t, sin_t = rope_tables(ctx_len, seq)
    cc = jnp.zeros((SUBLANES, d), F32).at[:b].set(c).at[b].set(c_ctx)
    mods = adaln_all(cc, ada_w, ada_b)
    x_all = jnp.concatenate([ctx, x], axis=1)
    s5_ops = jax.vmap(s5_derive)(s5_lam_re, s5_lam_im, s5_log_dt, s5_b_re, s5_b_im, s5_c_re, s5_c_im)
    for layer in range(depth):
        m = mods[layer]
        mod = tuple(m[:b, None, k * d:(k + 1) * d] for k in range(3)) + \
            tuple(m[b:b + 1, k * d:(k + 1) * d] for k in range(3))
        last = layer == depth - 1
        j = layer // 2
        g = norm_g[layer][None, :]
        if layer % 2 == 0:
            x_all = even_layer(x_all, mod, g, ev_w_in[j], ev_w_out[j], ev_q_g[j], ev_k_g[j], ev_sgu_g[j],
                               ev_ws[j], ev_bs[j], cos_t, sin_t, ctx_len, last)
        else:
            x_all = odd_layer(x_all, mod, g, od_w_in[j], od_w_out[j], tuple(a[j] for a in s5_ops),
                              s5_d[j], s5_glu_w[j], s5_glu_b[j],
                              lru_conv_w[j], lru_conv_b[j], lru_lam[j], lru_wa[j], lru_ba[j], lru_wx[j],
                              lru_bx[j], ctx_len, last)
    return x_all
```

```python
import functools
import math

import jax
import jax.numpy as jnp
from jax import lax
from jax.experimental import pallas as pl
from jax.experimental.pallas import tpu as pltpu

F32 = jnp.float32
BF16 = jnp.bfloat16

EPS = 1e-6
GRID_W = 64
HEAD_DIM = 128
A_Q_HEADS = 12
A_KV_HEADS = 4
A_GROUP = A_Q_HEADS // A_KV_HEADS
B_GROUPS = 4
CHUNK = 128
ROPE_THETA = 10000.0
C_GROUP_DIM = 16
C_STATE = 64
D_BLOCKS = 8
CONV_W = 4
LRU_C = 8.0

LANES = 128
SUBLANES = 8
SUB = 8
PIECE_GROUPS = LANES // C_GROUP_DIM
VMEM_LIMIT = 50 * 1024 * 1024


def _cparams(*sem):
    return pltpu.CompilerParams(dimension_semantics=sem, vmem_limit_bytes=VMEM_LIMIT)


def _sigmoid(x):
    return 0.5 * jnp.tanh(0.5 * x) + 0.5


def _silu(x):
    return x * _sigmoid(x)


def _pick_tile(rows, target, align=16):
    best = None
    for cand in range(align, min(rows, target) + 1, align):
        if rows % cand == 0:
            best = cand
    assert best is not None, (rows, target)
    return best


def _split_bf16(x, parts):
    out = []
    for _ in range(parts):
        hi = x.astype(BF16)
        out.append(hi)
        x = x - hi.astype(F32)
    return out


def _adaln_kernel(cc_ref, w_ref, b_ref, o_ref):
    s_parts = _split_bf16(_silu(cc_ref[...]), 3)
    w_hi, w_lo = _split_bf16(w_ref[0], 2)
    acc = jnp.dot(s_parts[2], w_hi, preferred_element_type=F32)
    for s_p, w_p in ((s_parts[1], w_lo), (s_parts[0], w_lo), (s_parts[1], w_hi), (s_parts[0], w_hi)):
        acc = acc + jnp.dot(s_p, w_p, preferred_element_type=F32)
    o_ref[0] = acc + b_ref[0]


def adaln_all(cc, ada_w, ada_b):
    depth, d, n3 = ada_w.shape
    rows = cc.shape[0]
    tn = 512
    return pl.pallas_call(
        _adaln_kernel,
        grid=(depth, n3 // tn),
        in_specs=[pl.BlockSpec((rows, d), lambda l, j: (0, 0)),
                  pl.BlockSpec((1, d, tn), lambda l, j: (l, 0, j)),
                  pl.BlockSpec((1, 1, tn), lambda l, j: (l, 0, j))],
        out_specs=pl.BlockSpec((1, rows, tn), lambda l, j: (l, 0, j)),
        out_shape=jax.ShapeDtypeStruct((depth, rows, n3), F32),
        compiler_params=_cparams("parallel", "parallel"),
        name="adaln",
    )(cc, ada_w, ada_b.reshape(depth, 1, n3))


def _inproj_kernel(x_ref, shl_ref, scl_ref, shc_ref, scc_ref, g_ref, w_ref, o_ref, h_scr, *, tm, ctx_len):
    i = pl.program_id(1)
    first = pl.program_id(2) == 0

    def modulate(rows, sc_ref, sh_ref):
        x = x_ref[0, rows]
        r = lax.rsqrt(jnp.mean(x * x, axis=-1, keepdims=True) + EPS)
        h_scr[rows] = ((x * r) * (g_ref[...] * (1.0 + sc_ref[...])) + sh_ref[...]).astype(BF16)

    scl, shl = scl_ref.at[0], shl_ref.at[0]
    if tm >= ctx_len:
        @pl.when(jnp.logical_and(first, i == 0))
        def _():
            modulate(slice(0, ctx_len), scc_ref, shc_ref)
            if tm > ctx_len:
                modulate(slice(ctx_len, tm), scl, shl)

        @pl.when(jnp.logical_and(first, i > 0))
        def _():
            modulate(slice(0, tm), scl, shl)
    else:
        @pl.when(jnp.logical_and(first, i < ctx_len // tm))
        def _():
            modulate(slice(0, tm), scc_ref, shc_ref)

        @pl.when(jnp.logical_and(first, i >= ctx_len // tm))
        def _():
            modulate(slice(0, tm), scl, shl)

    o_ref[0, 0] = jnp.dot(h_scr[...], w_ref[...], preferred_element_type=F32).astype(o_ref.dtype)


def inproj(x_all, shift_l, scale_l, shift_c, scale_c, g, w, layer, ctx_len, tn, tm, out_dtype):
    b, t, d = x_all.shape
    n = w.shape[2]
    assert tm >= ctx_len or ctx_len % tm == 0
    return pl.pallas_call(
        functools.partial(_inproj_kernel, tm=tm, ctx_len=ctx_len),
        grid=(b, t // tm, n // tn),
        in_specs=[pl.BlockSpec((1, tm, d), lambda bb, i, j: (bb, i, 0)),
                  pl.BlockSpec((1, 1, d), lambda bb, i, j: (bb, 0, 0)),
                  pl.BlockSpec((1, 1, d), lambda bb, i, j: (bb, 0, 0)),
                  pl.BlockSpec((1, d), lambda bb, i, j: (0, 0)),
                  pl.BlockSpec((1, d), lambda bb, i, j: (0, 0)),
                  pl.BlockSpec((1, d), lambda bb, i, j: (0, 0)),
                  pl.BlockSpec((None, d, tn), lambda bb, i, j: (layer, 0, j))],
        out_specs=pl.BlockSpec((1, 1, tm, tn), lambda bb, i, j: (j, bb, i, 0)),
        out_shape=jax.ShapeDtypeStruct((n // tn, b, t, tn), out_dtype),
        scratch_shapes=[pltpu.VMEM((tm, d), BF16)],
        compiler_params=_cparams("parallel", "parallel", "arbitrary"),
        name="inproj",
    )(x_all, shift_l, scale_l, shift_c, scale_c, g, w)


def _outproj_kernel(ma_ref, mb_ref, w1_ref, w2_ref, x_ref, gl_ref, gc_ref, o_ref, *, ctx_tiles):
    i = pl.program_id(1)
    acc = jnp.dot(ma_ref[0], w1_ref[...], preferred_element_type=F32)
    acc = acc + jnp.dot(mb_ref[0], w2_ref[...], preferred_element_type=F32)
    gate = jnp.where(i < ctx_tiles, gc_ref[...], gl_ref[0])
    o_ref[0] = x_ref[0] + gate * acc


def outproj(ma, mb, w, layer, x_all, gate_l, gate_c, ctx_len, latent_only):
    b, t, d = x_all.shape
    k1, k2 = ma.shape[2], mb.shape[2]
    assert w.shape[1] == k1 + k2 and k1 % k2 == 0
    tm = ctx_len
    off = 1 if latent_only else 0
    t_out = t - ctx_len if latent_only else t
    return pl.pallas_call(
        functools.partial(_outproj_kernel, ctx_tiles=0 if latent_only else 1),
        grid=(b, t_out // tm),
        in_specs=[pl.BlockSpec((1, tm, k1), lambda bb, i: (bb, i + off, 0)),
                  pl.BlockSpec((1, tm, k2), lambda bb, i: (bb, i + off, 0)),
                  pl.BlockSpec((None, k1, d), lambda bb, i: (layer, 0, 0)),
                  pl.BlockSpec((None, k2, d), lambda bb, i: (layer, k1 // k2, 0)),
                  pl.BlockSpec((1, tm, d), lambda bb, i: (bb, i + off, 0)),
                  pl.BlockSpec((1, 1, d), lambda bb, i: (bb, 0, 0)),
                  pl.BlockSpec((1, d), lambda bb, i: (0, 0))],
        out_specs=pl.BlockSpec((1, tm, d), lambda bb, i: (bb, i, 0)),
        out_shape=jax.ShapeDtypeStruct((b, t_out, d), F32),
        compiler_params=_cparams("parallel", "parallel"),
        name="outproj",
    )(ma, mb, w, w, x_all, gate_l, gate_c)


def _norm_rope(x, gain, cos, sin):
    y = x * lax.rsqrt(jnp.mean(x * x, axis=-1, keepdims=True) + EPS) * gain
    lane = lax.broadcasted_iota(jnp.int32, y.shape, 1)
    partner = jnp.where((lane & 1) == 0, pltpu.roll(y, HEAD_DIM - 1, 1), pltpu.roll(y, 1, 1))
    return y * cos + partner * sin


def _attn_kernel(q0_ref, q1_ref, q2_ref, k_ref, v_ref, ga0_ref, ga1_ref, ga2_ref, qg_ref, kg_ref,
                 cos_ref, sin_ref, o_ref, k_scr, vaug_scr, *, ctx_len, tq):
    iq = pl.program_id(2)

    @pl.when(iq == 0)
    def _():
        k_scr[...] = _norm_rope(k_ref[0, 0].astype(F32), kg_ref[...], cos_ref[...], sin_ref[...]).astype(BF16)
        vaug_scr[:, :HEAD_DIM] = v_ref[0, 0].astype(BF16)
        vaug_scr[:, HEAD_DIM:] = jnp.ones((vaug_scr.shape[0], HEAD_DIM), BF16)

    def attend(n_keys):
        rows = pl.ds(pl.multiple_of(iq * tq, tq), tq)
        cos_q, sin_q = cos_ref[rows, :], sin_ref[rows, :]
        k = k_scr[:n_keys]
        vaug = vaug_scr[:n_keys]
        for g, (q_ref, ga_ref) in enumerate(((q0_ref, ga0_ref), (q1_ref, ga1_ref), (q2_ref, ga2_ref))):
            q = _norm_rope(q_ref[0, 0].astype(F32), qg_ref[...], cos_q, sin_q).astype(BF16)
            s = lax.dot_general(q, k, (((1,), (1,)), ((), ())), preferred_element_type=F32)
            p = jnp.exp2(s - jnp.max(s, axis=-1, keepdims=True)).astype(BF16)
            pv = jnp.dot(p, vaug, preferred_element_type=F32)
            o = pv[:, :HEAD_DIM] / pv[:, HEAD_DIM:]
            o_ref[0, :, g * HEAD_DIM:(g + 1) * HEAD_DIM] = (o * _silu(ga_ref[0, 0].astype(F32))).astype(BF16)

    @pl.when(iq == 0)
    def _():
        attend(ctx_len)

    @pl.when(iq > 0)
    def _():
        attend(k_scr.shape[0])


def attention(slabs, q_gain, k_gain, cos_t, sin_t, ctx_len):
    _, b, t, tn = slabs.shape
    per = tn // HEAD_DIM
    tq = ctx_len
    gw = A_GROUP * HEAD_DIM
    assert A_GROUP == 3

    def head_spec(first_slab, g):
        def head_map(bb, kv, iq):
            h = kv * A_GROUP + g
            return (first_slab + h // per, bb, iq, h % per)
        return pl.BlockSpec((1, 1, tq, HEAD_DIM), head_map)

    vec = pl.BlockSpec((1, HEAD_DIM), lambda bb, kv, iq: (0, 0))
    table = pl.BlockSpec((t, HEAD_DIM), lambda bb, kv, iq: (0, 0))
    return pl.pallas_call(
        functools.partial(_attn_kernel, ctx_len=ctx_len, tq=tq),
        grid=(b, A_KV_HEADS, t // tq),
        in_specs=[head_spec(2, 0), head_spec(2, 1), head_spec(2, 2),
                  pl.BlockSpec((1, 1, t, HEAD_DIM), lambda bb, kv, iq: (0, bb, 0, kv)),
                  pl.BlockSpec((1, 1, t, HEAD_DIM), lambda bb, kv, iq: (1, bb, 0, kv)),
                  head_spec(7, 0), head_spec(7, 1), head_spec(7, 2),
                  vec, vec, table, table],
        out_specs=pl.BlockSpec((1, tq, gw), lambda bb, kv, iq: (bb, iq, kv)),
        out_shape=jax.ShapeDtypeStruct((b, t, A_Q_HEADS * HEAD_DIM), BF16),
        scratch_shapes=[pltpu.VMEM((t, HEAD_DIM), BF16), pltpu.VMEM((t, 2 * HEAD_DIM), BF16)],
        compiler_params=_cparams("parallel", "parallel", "arbitrary"),
        name="attention",
    )(slabs, slabs, slabs, slabs, slabs, slabs, slabs, slabs, q_gain, k_gain, cos_t, sin_t)


def _sgu_kernel(u_ref, v_ref, gb_ref, ws_ref, bs_ref, g_ref, o_ref):
    outs = []
    for grp in range(B_GROUPS):
        cols = slice(grp * LANES, (grp + 1) * LANES)
        v = v_ref[0, 0, :, cols].astype(F32)
        vn = v * lax.rsqrt(jnp.mean(v * v, axis=-1, keepdims=True) + EPS) * g_ref[:, cols]
        mixed = jnp.dot(ws_ref[grp], vn.astype(BF16), preferred_element_type=F32) + bs_ref[grp]
        outs.append(u_ref[0, 0, :, cols].astype(F32) * mixed * _silu(gb_ref[0, 0, :, cols].astype(F32)))
    o_ref[0] = jnp.concatenate(outs, axis=1).astype(BF16)


def sgu(slabs, ws, bs, g):
    _, b, t, tn = slabs.shape
    spec = lambda s: pl.BlockSpec((1, 1, CHUNK, tn), lambda bb, c: (s, bb, c, 0))
    return pl.pallas_call(
        _sgu_kernel,
        grid=(b, t // CHUNK),
        in_specs=[spec(5), spec(6), spec(10),
                  pl.BlockSpec((B_GROUPS, CHUNK, CHUNK), lambda bb, c: (0, 0, 0)),
                  pl.BlockSpec((B_GROUPS, CHUNK, 1), lambda bb, c: (0, 0, 0)),
                  pl.BlockSpec((1, tn), lambda bb, c: (0, 0))],
        out_specs=pl.BlockSpec((1, CHUNK, tn), lambda bb, c: (bb, c, 0)),
        out_shape=jax.ShapeDtypeStruct((b, t, tn), BF16),
        compiler_params=_cparams("parallel", "parallel"),
        name="sgu",
    )(slabs, slabs, slabs, ws, bs, g)


def _s5_fold_rows(u_ref, tm):
    return jnp.concatenate([u_ref[pl.ds(t, tm, stride=SUB), :] for t in range(SUB)], axis=1)


def _s5_scan(g_ref, c_ref, o_ref, *, ctx_pairs, all_pairs, half):
    rows = 2 * SUBLANES
    rowi = lax.broadcasted_iota(jnp.int32, (SUBLANES, half), 0)

    def tile(d, g_re, g_im, c_re, c_im):
        bwd = d == 1
        b_re, b_im = g_re, g_im
        for idx, dd in enumerate((1, 2, 4)):
            a_re, a_im = c_ref[d, idx, 0], c_ref[d, idx, 1]
            sh = SUBLANES - dd if bwd else dd
            s_re, s_im = pltpu.roll(b_re, sh, 0), pltpu.roll(b_im, sh, 0)
            b_re, b_im = (b_re + a_re * s_re - a_im * s_im,
                          b_im + a_re * s_im + a_im * s_re)
        a_re, a_im = c_ref[d, 3, 0], c_ref[d, 3, 1]
        h_re = b_re + a_re * c_re - a_im * c_im
        h_im = b_im + a_re * c_im + a_im * c_re
        if bwd:
            p_re = jnp.where(rowi == SUBLANES - 1, c_re, pltpu.roll(h_re, SUBLANES - 1, 0))
            p_im = jnp.where(rowi == SUBLANES - 1, c_im, pltpu.roll(h_im, SUBLANES - 1, 0))
            return p_re, p_im, h_re[0:1], h_im[0:1]
        p_re = jnp.where(rowi == 0, c_re, pltpu.roll(h_re, 1, 0))
        p_im = jnp.where(rowi == 0, c_im, pltpu.roll(h_im, 1, 0))
        return p_re, p_im, h_re[SUBLANES - 1:SUBLANES], h_im[SUBLANES - 1:SUBLANES]

    def pair(d, pi, carry):
        c_re, c_im = carry
        r0 = pl.multiple_of(pi * rows, rows)
        base = d * 2 * half
        order = (1, 0) if d == 1 else (0, 1)
        res = [None, None]
        for which in order:
            rr = pl.ds(r0 + which * SUBLANES, SUBLANES)
            g_re = g_ref[rr, pl.ds(base, half)]
            g_im = g_ref[rr, pl.ds(base + half, half)]
            p_re, p_im, c_re, c_im = tile(d, g_re, g_im, c_re, c_im)
            res[which] = (p_re, p_im)
        o_ref[pl.ds(r0, rows), pl.ds(base, half)] = jnp.concatenate(
            [res[0][0], res[1][0]], axis=0).astype(BF16)
        o_ref[pl.ds(r0, rows), pl.ds(base + half, half)] = jnp.concatenate(
            [res[0][1], res[1][1]], axis=0).astype(BF16)
        return c_re, c_im

    zero = (jnp.zeros((1, half), F32), jnp.zeros((1, half), F32))
    lax.fori_loop(0, all_pairs, lambda pi, c: pair(0, pi, c), zero)
    c = lax.fori_loop(0, ctx_pairs, lambda n, c: pair(1, ctx_pairs - 1 - n, c), zero)
    lax.fori_loop(0, all_pairs - ctx_pairs, lambda n, c: pair(1, all_pairs - 1 - n, c), c)


def _s5_core_kernel(u_ref, win_ref, t_ref, wout_ref, c_ref, y_ref, g_scr, hp_scr, *, tm, ctx_pairs, half):
    x = _s5_fold_rows(u_ref, tm).astype(BF16)
    g_scr[...] = jnp.dot(x, win_ref[0], preferred_element_type=F32)
    _s5_scan(g_scr, c_ref, hp_scr, ctx_pairs=ctx_pairs, all_pairs=tm // (2 * SUBLANES), half=half)
    y = jnp.dot(x, t_ref[0], preferred_element_type=F32)
    y = y + jnp.dot(hp_scr[...], wout_ref[0], preferred_element_type=F32)
    for t in range(SUB):
        y_ref[pl.ds(t, tm, stride=SUB), :] = y[:, t * LANES:(t + 1) * LANES]


def s5_core(slabs2, w_in, t_intra, w_out, consts, batch, ctx_len):
    _, r, width = slabs2.shape
    t = r // batch
    tm = t // SUB
    n_piece = t_intra.shape[0]
    half = PIECE_GROUPS * C_STATE
    assert w_in.shape[2] == 4 * half and tm % (2 * SUBLANES) == 0 and ctx_len % (2 * SUBLANES * SUB) == 0
    return pl.pallas_call(
        functools.partial(_s5_core_kernel, tm=tm, ctx_pairs=ctx_len // SUB // (2 * SUBLANES), half=half),
        grid=(n_piece, batch),
        in_specs=[pl.BlockSpec((None, t, LANES), lambda i, bb: (0, bb, i)),
                  pl.BlockSpec((1,) + w_in.shape[1:], lambda i, bb: (i, 0, 0)),
                  pl.BlockSpec((1,) + t_intra.shape[1:], lambda i, bb: (i, 0, 0)),
                  pl.BlockSpec((1,) + w_out.shape[1:], lambda i, bb: (i, 0, 0)),
                  pl.BlockSpec((2, 4, 2, SUBLANES, half), lambda i, bb: (0, 0, 0, 0, i))],
        out_specs=pl.BlockSpec((t, LANES), lambda i, bb: (bb, i)),
        out_shape=jax.ShapeDtypeStruct((r, width), F32),
        scratch_shapes=[pltpu.VMEM((tm, 4 * half), F32), pltpu.VMEM((tm, 4 * half), BF16)],
        compiler_params=_cparams("parallel", "parallel"),
        name="s5_core",
    )(slabs2, w_in, t_intra, w_out, consts)


def _s5_finish_kernel(y_ref, u_ref, gc_ref, d_ref, w_ref, b_ref, o_ref):
    y = y_ref[...] + d_ref[...] * u_ref[0]
    y = jax.nn.gelu(y)
    z = jnp.dot(y.astype(BF16), w_ref[...], preferred_element_type=F32) + b_ref[...]
    o_ref[...] = (y * _sigmoid(z) * _silu(gc_ref[0])).astype(BF16)


def s5_finish(y, slabs2, d_skip, glu_w, glu_b, tm):
    r, w = y.shape
    return pl.pallas_call(
        _s5_finish_kernel,
        grid=(r // tm,),
        in_specs=[pl.BlockSpec((tm, w), lambda i: (i, 0)),
                  pl.BlockSpec((1, tm, w), lambda i: (0, i, 0)),
                  pl.BlockSpec((1, tm, w), lambda i: (2, i, 0)),
                  pl.BlockSpec((1, w), lambda i: (0, 0)),
                  pl.BlockSpec((w, w), lambda i: (0, 0)),
                  pl.BlockSpec((1, w), lambda i: (0, 0))],
        out_specs=pl.BlockSpec((tm, w), lambda i: (i, 0)),
        out_shape=jax.ShapeDtypeStruct((r, w), BF16),
        compiler_params=_cparams("parallel"),
        name="s5_finish",
    )(y, slabs2, slabs2, d_skip, glu_w, glu_b)


def _expand_block_diag(src, unit, row_unit):
    rows, cols = src.shape
    wide = cols * PIECE_GROUPS
    lg = lambda v: int(math.log2(v))
    k = lax.broadcasted_iota(jnp.int32, (cols, wide), 0)
    j = lax.broadcasted_iota(jnp.int32, (cols, wide), 1)
    src_col = ((j >> lg(PIECE_GROUPS * unit)) << lg(unit)) + (j & (unit - 1))
    spread = jnp.where(k == src_col, 1.0, 0.0).astype(BF16)
    out = jnp.dot(src.astype(BF16), spread, preferred_element_type=F32)
    rg = (lax.broadcasted_iota(jnp.int32, (rows, wide), 0) >> lg(row_unit)) & (PIECE_GROUPS - 1)
    ch = (lax.broadcasted_iota(jnp.int32, (rows, wide), 1) >> lg(unit)) & (PIECE_GROUPS - 1)
    return jnp.where(rg == ch, out, 0.0).astype(BF16)


def _s5_expand_kernel(kc_ref, wf_ref, wb_ref, wo_ref, t_ref, win_ref, wout_ref, tsrc_scr, *, p, n):
    lane = lax.broadcasted_iota(jnp.int32, (p, LANES), 1)
    for gl in range(PIECE_GROUPS):
        kf = kc_ref[0, 0, gl]
        kb = kc_ref[1, 0, gl]
        for s in range(SUB):
            shift_b = (LANES - p * (SUB - 1 - s)) % LANES
            f = kf if s == 0 else pltpu.roll(kf, p * s, 1)
            r = kb if shift_b == 0 else pltpu.roll(kb, shift_b, 1)
            blk = jnp.where(lane >= p * s, f, 0.0) + jnp.where(lane < p * (s + 1), r, 0.0)
            tsrc_scr[pl.ds((s * PIECE_GROUPS + gl) * p, p), :] = blk
    t_ref[0] = _expand_block_diag(tsrc_scr[...], p, p)
    half = win_ref.shape[2] // 2
    win_ref[0, :, :half] = _expand_block_diag(wf_ref[0], n, p)
    win_ref[0, :, half:] = _expand_block_diag(wb_ref[0], n, p)
    wout_ref[0] = _expand_block_diag(wo_ref[0], p, n)


def s5_expand(kcat, win_f, win_b, wout_src, p, n):
    npiece = win_f.shape[0]
    rows_t = SUB * PIECE_GROUPS * p
    spec = lambda a, mult: pl.BlockSpec((1, a.shape[1], a.shape[2] * mult), lambda i: (i, 0, 0))
    wide = lambda a, mult: jax.ShapeDtypeStruct((npiece, a.shape[1], a.shape[2] * mult), BF16)
    return pl.pallas_call(
        functools.partial(_s5_expand_kernel, p=p, n=n),
        grid=(npiece,),
        in_specs=[pl.BlockSpec((2, 1, PIECE_GROUPS, p, LANES), lambda i: (0, i, 0, 0, 0)),
                  spec(win_f, 1), spec(win_b, 1), spec(wout_src, 1)],
        out_specs=[pl.BlockSpec((1, rows_t, LANES * PIECE_GROUPS), lambda i: (i, 0, 0)),
                   spec(win_f, 2 * PIECE_GROUPS), spec(wout_src, PIECE_GROUPS)],
        out_shape=[jax.ShapeDtypeStruct((npiece, rows_t, LANES * PIECE_GROUPS), BF16),
                   wide(win_f, 2 * PIECE_GROUPS), wide(wout_src, PIECE_GROUPS)],
        scratch_shapes=[pltpu.VMEM((rows_t, LANES), F32)],
        compiler_params=_cparams("parallel"),
        name="s5_expand",
    )(kcat, win_f, win_b, wout_src)


def s5_derive(lam_re, lam_im, log_dt, b_re, b_im, c_re, c_im):
    hi = lax.Precision.HIGHEST
    n_dir, g, n = lam_re.shape
    p = b_re.shape[-1]
    npiece = g // PIECE_GROUPS
    assert n_dir == 2 and SUB * p == LANES
    dt = jnp.exp(log_dt)[:, :, None]
    up = list(range(SUB))
    down = up[::-1]

    def powers(exps):
        k = jnp.asarray(exps, F32)[:, :, None, None]
        mag = jnp.exp(k * (lam_re * dt)[:, None])
        ang = k * (lam_im * dt)[:, None]
        return mag * jnp.cos(ang), mag * jnp.sin(ang)

    a_re, a_im = powers([[1], [1]])
    den = lam_re * lam_re + lam_im * lam_im
    nr, ni = a_re[:, 0] - 1.0, a_im[:, 0]
    cf_re = ((nr * lam_re + ni * lam_im) / den)[:, :, None, :]
    cf_im = ((ni * lam_re - nr * lam_im) / den)[:, :, None, :]
    bt_re, bt_im = jnp.swapaxes(b_re, 2, 3), jnp.swapaxes(b_im, 2, 3)
    bb_re = cf_re * bt_re - cf_im * bt_im
    bb_im = cf_re * bt_im + cf_im * bt_re
    ct_re, ct_im = jnp.swapaxes(c_re, 2, 3), jnp.swapaxes(c_im, 2, 3)

    def c_times_powers(exps):
        e_re, e_im = powers(exps)
        e_re = jnp.transpose(e_re, (0, 2, 3, 1))[..., None]
        e_im = jnp.transpose(e_im, (0, 2, 3, 1))[..., None]
        cr, ci = ct_re[:, :, :, None, :], ct_im[:, :, :, None, :]
        return cr * e_re - ci * e_im, cr * e_im + ci * e_re

    cp_re, cp_im = c_times_powers([up + [k + 1 for k in up], down + [SUB - k for k in up]])
    ck_re, ck_im, co_re, co_im = cp_re[:, :, :, :SUB], cp_im[:, :, :, :SUB], cp_re[:, :, :, SUB:], cp_im[:, :, :, SUB:]

    lhs = jnp.concatenate([bb_re, -bb_im], axis=-1)
    rhs = jnp.concatenate([ck_re, ck_im], axis=2).reshape(n_dir, g, 2 * n, SUB * p)
    kcat = jnp.einsum('dgqk,dgkm->dgqm', lhs, rhs, precision=hi)
    kcat = kcat.reshape(n_dir, npiece, PIECE_GROUPS, p, SUB * p)

    e_re, e_im = powers([down, up])
    e1 = jnp.concatenate([e_re, e_re], axis=-1).reshape(n_dir, SUB, npiece, PIECE_GROUPS, 1, 2 * n)
    e2 = jnp.concatenate([-e_im, e_im], axis=-1).reshape(n_dir, SUB, npiece, PIECE_GROUPS, 1, 2 * n)
    b1 = jnp.concatenate([bb_re, bb_im], axis=-1).reshape(n_dir, 1, npiece, PIECE_GROUPS, p, 2 * n)
    b2 = jnp.concatenate([bb_im, bb_re], axis=-1).reshape(n_dir, 1, npiece, PIECE_GROUPS, p, 2 * n)
    win = (e1 * b1 + e2 * b2).transpose(0, 2, 1, 3, 4, 5)
    win = win.reshape(n_dir, npiece, SUB * PIECE_GROUPS * p, 2 * n)

    o = jnp.stack([co_re, -co_im], axis=1)
    o = o.reshape(n_dir, 2, npiece, PIECE_GROUPS * n, SUB * p).transpose(2, 0, 1, 3, 4)
    wout_src = o.reshape(npiece, n_dir * 2 * PIECE_GROUPS * n, SUB * p)

    steps = [SUB * m for m in range(1, SUBLANES + 1)]
    q_re, q_im = powers([steps, steps])
    q_re = q_re.reshape(n_dir, SUBLANES, g * n)
    q_im = q_im.reshape(n_dir, SUBLANES, g * n)
    row = jnp.arange(SUBLANES)
    consts = []
    for d in range(n_dir):
        kinds = []
        for dd in (1, 2, 4):
            keep = (row + dd <= SUBLANES - 1) if d == 1 else (row >= dd)
            m = keep.astype(F32)[:, None]
            kinds.append(jnp.stack([m * q_re[d, dd - 1][None], m * q_im[d, dd - 1][None]]))
        sel = (SUBLANES - 1 - row) if d == 1 else row
        kinds.append(jnp.stack([q_re[d][sel], q_im[d][sel]]))
        consts.append(jnp.stack(kinds))
    consts = jnp.stack(consts)
    return kcat, win[0], win[1], wout_src, consts


def _lru_kernel(*refs, bwd, nch, tc, width):
    if bwd:
        (x_ref, xp_ref, xn_ref, cw_ref, cb_ref, lam_ref, wa_ref, ba_ref, wx_ref, bx_ref,
         hf_ref, gd_ref, o_ref, pad_scr, a_scr, b_scr, carry_scr) = refs
    else:
        (x_ref, xp_ref, xn_ref, cw_ref, cb_ref, lam_ref, wa_ref, ba_ref, wx_ref, bx_ref,
         o_ref, pad_scr, a_scr, b_scr, carry_scr) = refs
    k = pl.program_id(1)
    chunk = jnp.where(k == 0, 0, nch - k) if bwd else k
    prev_ok = jnp.logical_and(chunk != 0, chunk != 1)
    next_ok = jnp.logical_and(chunk != 0, chunk != nch - 1)
    pad_scr[0:SUBLANES] = jnp.where(prev_ok, xp_ref[0, 0], 0.0)
    pad_scr[SUBLANES:SUBLANES + tc] = x_ref[0, 0]
    pad_scr[SUBLANES + tc:2 * SUBLANES + tc] = jnp.where(next_ok, xn_ref[0, 0], 0.0)
    xc = cb_ref[...]
    for tap in range(CONV_W):
        xc = xc + cw_ref[tap:tap + 1] * pad_scr[SUBLANES - 1 + tap:SUBLANES - 1 + tap + tc]

    bw = width // D_BLOCKS
    r_parts, i_parts = [], []
    for blk in range(D_BLOCKS):
        xb = xc[:, blk * bw:(blk + 1) * bw].astype(BF16)
        r_parts.append(jnp.dot(xb, wa_ref[blk], preferred_element_type=F32))
        i_parts.append(jnp.dot(xb, wx_ref[blk], preferred_element_type=F32))
    r = _sigmoid(jnp.concatenate(r_parts, axis=1) + ba_ref[...])
    ig = _sigmoid(jnp.concatenate(i_parts, axis=1) + bx_ref[...])
    nl = -lam_ref[...]
    e = jnp.exp(-jnp.abs(nl))
    e1 = 1.0 + e
    log1p_e = jnp.where(e1 == 1.0, e, jnp.log(e1) * (e / jnp.where(e1 == 1.0, 1.0, e1 - 1.0)))
    softplus = jnp.maximum(nl, 0.0) + log1p_e
    log_a = (-LRU_C * softplus) * r
    a = jnp.exp(log_a)
    a_scr[...] = a
    b_scr[...] = jnp.sqrt(-jnp.tanh(log_a) * (1.0 + a * a)) * (ig * xc)

    @pl.when(k == 0)
    def _():
        carry_scr[...] = jnp.zeros_like(carry_scr)

    rowi = lax.broadcasted_iota(jnp.int32, (SUBLANES, width), 0)
    n_tiles = tc // SUBLANES

    def body(n, carry):
        ti = n_tiles - 1 - n if bwd else n
        rr = pl.ds(pl.multiple_of(ti * SUBLANES, SUBLANES), SUBLANES)
        a, b = a_scr[rr], b_scr[rr]
        for dd in (1, 2, 4):
            keep = (rowi + dd <= SUBLANES - 1) if bwd else (rowi >= dd)
            sh = SUBLANES - dd if bwd else dd
            b = b + a * jnp.where(keep, pltpu.roll(b, sh, 0), 0.0)
            a = a * jnp.where(keep, pltpu.roll(a, sh, 0), 1.0)
        h = b + a * carry
        b_scr[rr] = h
        return h[0:1] if bwd else h[SUBLANES - 1:SUBLANES]

    carry_scr[...] = lax.fori_loop(0, n_tiles, body, carry_scr[...])
    if bwd:
        o_ref[0] = ((hf_ref[0] + b_scr[...]) * _silu(gd_ref[0, 0])).astype(BF16)
    else:
        o_ref[0] = b_scr[...]


def lru_pass(slabs, conv_w, conv_b, lam, wa, ba, wx, bx, ctx_len, hf=None):
    _, b, t, w = slabs.shape
    bwd = hf is not None
    tc = ctx_len
    nch = t // tc
    hb = tc // SUBLANES
    nhb = t // SUBLANES

    def ch(k):
        return jnp.where(k == 0, 0, nch - k) if bwd else k

    in_specs = [pl.BlockSpec((1, 1, tc, w), lambda bb, k: (1, bb, ch(k), 0)),
                pl.BlockSpec((1, 1, SUBLANES, w), lambda bb, k: (1, bb, jnp.maximum(ch(k) * hb - 1, 0), 0)),
                pl.BlockSpec((1, 1, SUBLANES, w), lambda bb, k: (1, bb, jnp.minimum((ch(k) + 1) * hb, nhb - 1), 0)),
                pl.BlockSpec((CONV_W, w), lambda bb, k: (0, 0)),
                pl.BlockSpec((1, w), lambda bb, k: (0, 0)),
                pl.BlockSpec((1, w), lambda bb, k: (0, 0)),
                pl.BlockSpec(wa.shape, lambda bb, k: (0, 0, 0)),
                pl.BlockSpec((1, w), lambda bb, k: (0, 0)),
                pl.BlockSpec(wx.shape, lambda bb, k: (0, 0, 0)),
                pl.BlockSpec((1, w), lambda bb, k: (0, 0))]
    args = [slabs, slabs, slabs, conv_w, conv_b, lam, wa, ba, wx, bx]
    if bwd:
        in_specs += [pl.BlockSpec((1, tc, w), lambda bb, k: (bb, ch(k), 0)),
                     pl.BlockSpec((1, 1, tc, w), lambda bb, k: (3, bb, ch(k), 0))]
        args += [hf, slabs]
    return pl.pallas_call(
        functools.partial(_lru_kernel, bwd=bwd, nch=nch, tc=tc, width=w),
        grid=(b, nch),
        in_specs=in_specs,
        out_specs=pl.BlockSpec((1, tc, w), lambda bb, k: (bb, ch(k), 0)),
        out_shape=jax.ShapeDtypeStruct((b, t, w), BF16 if bwd else F32),
        scratch_shapes=[pltpu.VMEM((tc + 2 * SUBLANES, w), F32),
                        pltpu.VMEM((tc, w), F32),
                        pltpu.VMEM((tc, w), F32),
                        pltpu.VMEM((1, w), F32)],
        compiler_params=_cparams("parallel", "arbitrary"),
        name="lru_bwd" if bwd else "lru_fwd",
    )(*args)


def rope_tables(ctx_len, seq):
    rows = seq // GRID_W
    axis_dim = HEAD_DIM // 2
    row = jnp.repeat(jnp.arange(rows), GRID_W).astype(F32)
    col = jnp.tile(jnp.arange(GRID_W), rows).astype(F32)
    inv = ROPE_THETA ** (-jnp.arange(0, axis_dim, 2, dtype=F32) / axis_dim)
    ang = jnp.concatenate([row[:, None] * inv, col[:, None] * inv], axis=-1)
    cos, sin = jnp.cos(ang), jnp.sin(ang)
    cos_t = jnp.repeat(cos, 2, axis=-1)
    sin_t = jnp.stack([-sin, sin], axis=-1).reshape(seq, HEAD_DIM)
    cos_t = jnp.concatenate([jnp.ones((ctx_len, HEAD_DIM), F32), cos_t], axis=0)
    sin_t = jnp.concatenate([jnp.zeros((ctx_len, HEAD_DIM), F32), sin_t], axis=0)
    return cos_t, sin_t


def even_layer(x_all, mod, g, w_in, w_out, j, q_g, k_g, sgu_g, ws, bs, cos_t, sin_t, ctx_len, latent_only):
    shift_l, scale_l, gate_l, shift_c, scale_c, gate_c = mod
    slabs = inproj(x_all, shift_l, scale_l, shift_c, scale_c, g, w_in, j, ctx_len, tn=512,
                   tm=_pick_tile(x_all.shape[1], 1152), out_dtype=BF16)
    q_gain = (q_g * (HEAD_DIM ** -0.5 * math.log2(math.e)))[None, :]
    mix_a = attention(slabs, q_gain, k_g[None, :], cos_t, sin_t, ctx_len)
    mix_b = sgu(slabs, ws.astype(BF16), bs[:, :, None], sgu_g[None, :])
    return outproj(mix_a, mix_b, w_out, j, x_all, gate_l, gate_c, ctx_len, latent_only)


def odd_layer(x_all, mod, g, w_in, w_out, j, s5_ops, d_skip, glu_w, glu_b, conv_w, conv_b, lam, wa, ba, wx, bx,
              ctx_len, latent_only):
    shift_l, scale_l, gate_l, shift_c, scale_c, gate_c = mod
    b, t, _ = x_all.shape
    slabs = inproj(x_all, shift_l, scale_l, shift_c, scale_c, g, w_in, j, ctx_len, tn=1024,
                   tm=_pick_tile(t, 1152), out_dtype=F32)
    cw = slabs.shape[3]
    kcat, win_f, win_b, wout_src, consts = s5_ops
    t_intra, w_s5in, w_s5out = s5_expand(kcat, win_f, win_b, wout_src, kcat.shape[-2], C_STATE)
    slabs2 = slabs.reshape(slabs.shape[0], b * t, cw)
    y = s5_core(slabs2, w_s5in, t_intra, w_s5out, consts, b, ctx_len)
    mix_c = s5_finish(y, slabs2, d_skip[None, :], glu_w.astype(BF16), glu_b[None, :],
                      tm=_pick_tile(b * t, 512))
    hf = lru_pass(slabs, conv_w, conv_b[None, :], lam[0][None, :], wa[0].astype(BF16), ba[0][None, :],
                  wx[0].astype(BF16), bx[0][None, :], ctx_len)
    mix_d = lru_pass(slabs, conv_w, conv_b[None, :], lam[1][None, :], wa[1].astype(BF16), ba[1][None, :],
                     wx[1].astype(BF16), bx[1][None, :], ctx_len, hf=hf)
    return outproj(mix_c.reshape(b, t, cw), mix_d, w_out, j, x_all, gate_l, gate_c, ctx_len, latent_only)


def kernel(x, c, ctx, c_ctx, ada_w, ada_b, norm_g, ev_w_in, ev_w_out, ev_q_g, ev_k_g, ev_sgu_g, ev_ws, ev_bs,
           od_w_in, od_w_out, s5_lam_re, s5_lam_im, s5_log_dt, s5_b_re, s5_b_im, s5_c_re, s5_c_im, s5_d,
           s5_glu_w, s5_glu_b, lru_conv_w, lru_conv_b, lru_lam, lru_wa, lru_ba, lru_wx, lru_bx):
    b, seq, d = x.shape
    ctx_len = ctx.shape[1]
    depth = ada_w.shape[0]
    assert b + 1 <= SUBLANES and seq % ctx_len == 0
    cos_t, sin_t = rope_tables(ctx_len, seq)
    cc = jnp.zeros((2 * SUBLANES, d), F32).at[:b].set(c).at[b].set(c_ctx)
    mods = adaln_all(cc, ada_w, ada_b)
    x_all = jnp.concatenate([ctx, x], axis=1)
    s5_ops = jax.vmap(s5_derive)(s5_lam_re, s5_lam_im, s5_log_dt, s5_b_re, s5_b_im, s5_c_re, s5_c_im)
    ev_w_in, ev_w_out = ev_w_in.astype(BF16), ev_w_out.astype(BF16)
    od_w_in, od_w_out = od_w_in.astype(BF16), od_w_out.astype(BF16)
    for layer in range(depth):
        m = mods[layer]
        mod = tuple(m[:b, None, k * d:(k + 1) * d] for k in range(3)) + \
            tuple(m[b:b + 1, k * d:(k + 1) * d] for k in range(3))
        last = layer == depth - 1
        j = layer // 2
        g = norm_g[layer][None, :]
        if layer % 2 == 0:
            x_all = even_layer(x_all, mod, g, ev_w_in, ev_w_out, j, ev_q_g[j], ev_k_g[j], ev_sgu_g[j],
                               ev_ws[j], ev_bs[j], cos_t, sin_t, ctx_len, last)
        else:
            x_all = odd_layer(x_all, mod, g, od_w_in, od_w_out, j, tuple(a[j] for a in s5_ops),
                              s5_d[j], s5_glu_w[j], s5_glu_b[j],
                              lru_conv_w[j], lru_conv_b[j], lru_lam[j], lru_wa[j], lru_ba[j], lru_wx[j],
                              lru_bx[j], ctx_len, last)
    return x_all
```

```python
import functools
import math

import jax
import jax.numpy as jnp
from jax import lax
from jax.experimental import pallas as pl
from jax.experimental.pallas import tpu as pltpu

F32 = jnp.float32
BF16 = jnp.bfloat16

EPS = 1e-6
GRID_W = 64
HEAD_DIM = 128
A_Q_HEADS = 12
A_KV_HEADS = 4
A_GROUP = A_Q_HEADS // A_KV_HEADS
B_GROUPS = 4
CHUNK = 128
ROPE_THETA = 10000.0
C_GROUP_DIM = 16
C_STATE = 64
D_BLOCKS = 8
CONV_W = 4
LRU_C = 8.0

LANES = 128
SUBLANES = 8
SUB = 8
PIECE_GROUPS = LANES // C_GROUP_DIM
KEY_CHUNKS = 3
VMEM_LIMIT = 50 * 1024 * 1024


def _cparams(*sem):
    return pltpu.CompilerParams(dimension_semantics=sem, vmem_limit_bytes=VMEM_LIMIT)


def _sigmoid(x):
    return 0.5 * jnp.tanh(0.5 * x) + 0.5


def _silu(x):
    return x * _sigmoid(x)


def _pick_tile(rows, target, align=16):
    best = None
    for cand in range(align, min(rows, target) + 1, align):
        if rows % cand == 0:
            best = cand
    assert best is not None, (rows, target)
    return best


def _split_bf16(x, parts):
    out = []
    for _ in range(parts):
        hi = x.astype(BF16)
        out.append(hi)
        x = x - hi.astype(F32)
    return out


def _adaln_kernel(cc_ref, w_ref, b_ref, o_ref):
    s_parts = _split_bf16(_silu(cc_ref[...]), 3)
    w_hi, w_lo = _split_bf16(w_ref[0], 2)
    acc = jnp.dot(s_parts[2], w_hi, preferred_element_type=F32)
    for s_p, w_p in ((s_parts[1], w_lo), (s_parts[0], w_lo), (s_parts[1], w_hi), (s_parts[0], w_hi)):
        acc = acc + jnp.dot(s_p, w_p, preferred_element_type=F32)
    o_ref[0] = acc + b_ref[0]


def adaln_all(cc, ada_w, ada_b):
    depth, d, n3 = ada_w.shape
    rows = cc.shape[0]
    tn = _pick_tile(n3, 1536, align=LANES)
    return pl.pallas_call(
        _adaln_kernel,
        grid=(depth, n3 // tn),
        in_specs=[pl.BlockSpec((rows, d), lambda l, j: (0, 0)),
                  pl.BlockSpec((1, d, tn), lambda l, j: (l, 0, j)),
                  pl.BlockSpec((1, 1, tn), lambda l, j: (l, 0, j))],
        out_specs=pl.BlockSpec((1, rows, tn), lambda l, j: (l, 0, j)),
        out_shape=jax.ShapeDtypeStruct((depth, rows, n3), F32),
        compiler_params=_cparams("parallel", "parallel"),
        name="adaln",
    )(cc, ada_w, ada_b.reshape(depth, 1, n3))


def _inproj_kernel(x_ref, shl_ref, scl_ref, shc_ref, scc_ref, g_ref, w_ref, o_ref, h_scr, *, tm, ctx_len):
    i = pl.program_id(1)
    first = pl.program_id(2) == 0

    def modulate(rows, sc_ref, sh_ref):
        x = x_ref[0, rows]
        r = lax.rsqrt(jnp.mean(x * x, axis=-1, keepdims=True) + EPS)
        h_scr[rows] = ((x * r) * (g_ref[...] * (1.0 + sc_ref[...])) + sh_ref[...]).astype(BF16)

    scl, shl = scl_ref.at[0], shl_ref.at[0]
    if tm >= ctx_len:
        @pl.when(jnp.logical_and(first, i == 0))
        def _():
            modulate(slice(0, ctx_len), scc_ref, shc_ref)
            if tm > ctx_len:
                modulate(slice(ctx_len, tm), scl, shl)

        @pl.when(jnp.logical_and(first, i > 0))
        def _():
            modulate(slice(0, tm), scl, shl)
    else:
        @pl.when(jnp.logical_and(first, i < ctx_len // tm))
        def _():
            modulate(slice(0, tm), scc_ref, shc_ref)

        @pl.when(jnp.logical_and(first, i >= ctx_len // tm))
        def _():
            modulate(slice(0, tm), scl, shl)

    o_ref[0, 0] = jnp.dot(h_scr[...], w_ref[...], preferred_element_type=F32).astype(o_ref.dtype)


def inproj(x_all, shift_l, scale_l, shift_c, scale_c, g, w, layer, ctx_len, tn, tm, out_dtype):
    b, t, d = x_all.shape
    n = w.shape[2]
    assert tm >= ctx_len or ctx_len % tm == 0
    return pl.pallas_call(
        functools.partial(_inproj_kernel, tm=tm, ctx_len=ctx_len),
        grid=(b, t // tm, n // tn),
        in_specs=[pl.BlockSpec((1, tm, d), lambda bb, i, j: (bb, i, 0)),
                  pl.BlockSpec((1, 1, d), lambda bb, i, j: (bb, 0, 0)),
                  pl.BlockSpec((1, 1, d), lambda bb, i, j: (bb, 0, 0)),
                  pl.BlockSpec((1, d), lambda bb, i, j: (0, 0)),
                  pl.BlockSpec((1, d), lambda bb, i, j: (0, 0)),
                  pl.BlockSpec((1, d), lambda bb, i, j: (0, 0)),
                  pl.BlockSpec((None, d, tn), lambda bb, i, j: (layer, 0, j))],
        out_specs=pl.BlockSpec((1, 1, tm, tn), lambda bb, i, j: (j, bb, i, 0)),
        out_shape=jax.ShapeDtypeStruct((n // tn, b, t, tn), out_dtype),
        scratch_shapes=[pltpu.VMEM((tm, d), BF16)],
        compiler_params=_cparams("parallel", "parallel", "arbitrary"),
        name="inproj",
    )(x_all, shift_l, scale_l, shift_c, scale_c, g, w)


def _outproj_kernel(ma_ref, mb_ref, w1_ref, w2_ref, x_ref, gl_ref, gc_ref, o_ref, *, ctx_tiles):
    i = pl.program_id(1)
    acc = jnp.dot(ma_ref[0], w1_ref[...], preferred_element_type=F32)
    acc = acc + jnp.dot(mb_ref[0], w2_ref[...], preferred_element_type=F32)
    gate = jnp.where(i < ctx_tiles, gc_ref[...], gl_ref[0])
    o_ref[0] = x_ref[0] + gate * acc


def outproj(ma, mb, w, layer, x_all, gate_l, gate_c, ctx_len, latent_only):
    b, t, d = x_all.shape
    k1, k2 = ma.shape[2], mb.shape[2]
    assert w.shape[1] == k1 + k2 and k1 % k2 == 0
    tm = ctx_len
    off = 1 if latent_only else 0
    t_out = t - ctx_len if latent_only else t
    return pl.pallas_call(
        functools.partial(_outproj_kernel, ctx_tiles=0 if latent_only else 1),
        grid=(b, t_out // tm),
        in_specs=[pl.BlockSpec((1, tm, k1), lambda bb, i: (bb, i + off, 0)),
                  pl.BlockSpec((1, tm, k2), lambda bb, i: (bb, i + off, 0)),
                  pl.BlockSpec((None, k1, d), lambda bb, i: (layer, 0, 0)),
                  pl.BlockSpec((None, k2, d), lambda bb, i: (layer, k1 // k2, 0)),
                  pl.BlockSpec((1, tm, d), lambda bb, i: (bb, i + off, 0)),
                  pl.BlockSpec((1, 1, d), lambda bb, i: (bb, 0, 0)),
                  pl.BlockSpec((1, d), lambda bb, i: (0, 0))],
        out_specs=pl.BlockSpec((1, tm, d), lambda bb, i: (bb, i, 0)),
        out_shape=jax.ShapeDtypeStruct((b, t_out, d), F32),
        compiler_params=_cparams("parallel", "parallel"),
        name="outproj",
    )(ma, mb, w, w, x_all, gate_l, gate_c)


def _norm_rope(x, gain, cos, sin):
    y = x * lax.rsqrt(jnp.mean(x * x, axis=-1, keepdims=True) + EPS) * gain
    lane = lax.broadcasted_iota(jnp.int32, y.shape, 1)
    partner = jnp.where((lane & 1) == 0, pltpu.roll(y, HEAD_DIM - 1, 1), pltpu.roll(y, 1, 1))
    return y * cos + partner * sin


def _attn_kernel(q0_ref, q1_ref, q2_ref, k_ref, v_ref, ga0_ref, ga1_ref, ga2_ref, qg_ref, kg_ref,
                 cos_ref, sin_ref, o_ref, k_scr, vaug_scr, *, ctx_len, tq):
    iq = pl.program_id(2)

    @pl.when(iq == 0)
    def _():
        k_scr[...] = _norm_rope(k_ref[0, 0].astype(F32), kg_ref[...], cos_ref[...], sin_ref[...]).astype(BF16)
        vaug_scr[:, :HEAD_DIM] = v_ref[0, 0].astype(BF16)
        vaug_scr[:, HEAD_DIM:] = jnp.ones((vaug_scr.shape[0], HEAD_DIM), BF16)

    def attend(n_keys):
        rows = pl.ds(pl.multiple_of(iq * tq, tq), tq)
        cos_q, sin_q = cos_ref[rows, :], sin_ref[rows, :]
        kc = n_keys // KEY_CHUNKS if n_keys % (KEY_CHUNKS * 2 * HEAD_DIM) == 0 else n_keys
        for g, (q_ref, ga_ref) in enumerate(((q0_ref, ga0_ref), (q1_ref, ga1_ref), (q2_ref, ga2_ref))):
            q = _norm_rope(q_ref[0, 0].astype(F32), qg_ref[...], cos_q, sin_q).astype(BF16)
            parts = []
            for c0 in range(0, n_keys, kc):
                s = lax.dot_general(q, k_scr[c0:c0 + kc], (((1,), (1,)), ((), ())), preferred_element_type=F32)
                m = jnp.max(s, axis=-1, keepdims=True)
                p = jnp.exp2(s - m).astype(BF16)
                parts.append((m, jnp.dot(p, vaug_scr[c0:c0 + kc], preferred_element_type=F32)))
            m_all = functools.reduce(jnp.maximum, [m for m, _ in parts])
            pv = functools.reduce(lambda a, b: a + b, [acc * jnp.exp2(m - m_all) for m, acc in parts])
            o = pv[:, :HEAD_DIM] / pv[:, HEAD_DIM:]
            o_ref[0, :, g * HEAD_DIM:(g + 1) * HEAD_DIM] = (o * _silu(ga_ref[0, 0].astype(F32))).astype(BF16)

    @pl.when(iq == 0)
    def _():
        attend(ctx_len)

    @pl.when(iq > 0)
    def _():
        attend(k_scr.shape[0])


def attention(slabs, q_gain, k_gain, cos_t, sin_t, ctx_len):
    _, b, t, tn = slabs.shape
    per = tn // HEAD_DIM
    tq = ctx_len
    gw = A_GROUP * HEAD_DIM
    assert A_GROUP == 3

    def head_spec(first_slab, g):
        def head_map(bb, kv, iq):
            h = kv * A_GROUP + g
            return (first_slab + h // per, bb, iq, h % per)
        return pl.BlockSpec((1, 1, tq, HEAD_DIM), head_map)

    vec = pl.BlockSpec((1, HEAD_DIM), lambda bb, kv, iq: (0, 0))
    table = pl.BlockSpec((t, HEAD_DIM), lambda bb, kv, iq: (0, 0))
    return pl.pallas_call(
        functools.partial(_attn_kernel, ctx_len=ctx_len, tq=tq),
        grid=(b, A_KV_HEADS, t // tq),
        in_specs=[head_spec(2, 0), head_spec(2, 1), head_spec(2, 2),
                  pl.BlockSpec((1, 1, t, HEAD_DIM), lambda bb, kv, iq: (0, bb, 0, kv)),
                  pl.BlockSpec((1, 1, t, HEAD_DIM), lambda bb, kv, iq: (1, bb, 0, kv)),
                  head_spec(7, 0), head_spec(7, 1), head_spec(7, 2),
                  vec, vec, table, table],
        out_specs=pl.BlockSpec((1, tq, gw), lambda bb, kv, iq: (bb, iq, kv)),
        out_shape=jax.ShapeDtypeStruct((b, t, A_Q_HEADS * HEAD_DIM), BF16),
        scratch_shapes=[pltpu.VMEM((t, HEAD_DIM), BF16), pltpu.VMEM((t, 2 * HEAD_DIM), BF16)],
        compiler_params=_cparams("parallel", "parallel", "arbitrary"),
        name="attention",
    )(slabs, slabs, slabs, slabs, slabs, slabs, slabs, slabs, q_gain, k_gain, cos_t, sin_t)


def _sgu_kernel(u_ref, v_ref, gb_ref, ws_ref, bs_ref, g_ref, o_ref):
    outs = []
    for grp in range(B_GROUPS):
        cols = slice(grp * LANES, (grp + 1) * LANES)
        v = v_ref[0, 0, :, cols].astype(F32)
        vn = v * lax.rsqrt(jnp.mean(v * v, axis=-1, keepdims=True) + EPS) * g_ref[:, cols]
        mixed = jnp.dot(ws_ref[grp], vn.astype(BF16), preferred_element_type=F32) + bs_ref[grp]
        outs.append(u_ref[0, 0, :, cols].astype(F32) * mixed * _silu(gb_ref[0, 0, :, cols].astype(F32)))
    o_ref[0] = jnp.concatenate(outs, axis=1).astype(BF16)


def sgu(slabs, ws, bs, g):
    _, b, t, tn = slabs.shape
    spec = lambda s: pl.BlockSpec((1, 1, CHUNK, tn), lambda bb, c: (s, bb, c, 0))
    return pl.pallas_call(
        _sgu_kernel,
        grid=(b, t // CHUNK),
        in_specs=[spec(5), spec(6), spec(10),
                  pl.BlockSpec((B_GROUPS, CHUNK, CHUNK), lambda bb, c: (0, 0, 0)),
                  pl.BlockSpec((B_GROUPS, CHUNK, 1), lambda bb, c: (0, 0, 0)),
                  pl.BlockSpec((1, tn), lambda bb, c: (0, 0))],
        out_specs=pl.BlockSpec((1, CHUNK, tn), lambda bb, c: (bb, c, 0)),
        out_shape=jax.ShapeDtypeStruct((b, t, tn), BF16),
        compiler_params=_cparams("parallel", "parallel"),
        name="sgu",
    )(slabs, slabs, slabs, ws, bs, g)


def _s5_fold_rows(u_ref, tm):
    return jnp.concatenate([u_ref[pl.ds(t, tm, stride=SUB), :] for t in range(SUB)], axis=1)


def _s5_scan(g_ref, c_ref, o_ref, *, ctx_pairs, all_pairs, half):
    rows = 2 * SUBLANES
    rowi = lax.broadcasted_iota(jnp.int32, (SUBLANES, half), 0)

    def tile(d, g_re, g_im, c_re, c_im):
        bwd = d == 1
        b_re, b_im = g_re, g_im
        for idx, dd in enumerate((1, 2, 4)):
            a_re, a_im = c_ref[d, idx, 0], c_ref[d, idx, 1]
            sh = SUBLANES - dd if bwd else dd
            s_re, s_im = pltpu.roll(b_re, sh, 0), pltpu.roll(b_im, sh, 0)
            b_re, b_im = (b_re + a_re * s_re - a_im * s_im,
                          b_im + a_re * s_im + a_im * s_re)
        a_re, a_im = c_ref[d, 3, 0], c_ref[d, 3, 1]
        h_re = b_re + a_re * c_re - a_im * c_im
        h_im = b_im + a_re * c_im + a_im * c_re
        if bwd:
            p_re = jnp.where(rowi == SUBLANES - 1, c_re, pltpu.roll(h_re, SUBLANES - 1, 0))
            p_im = jnp.where(rowi == SUBLANES - 1, c_im, pltpu.roll(h_im, SUBLANES - 1, 0))
            return p_re, p_im, h_re[0:1], h_im[0:1]
        p_re = jnp.where(rowi == 0, c_re, pltpu.roll(h_re, 1, 0))
        p_im = jnp.where(rowi == 0, c_im, pltpu.roll(h_im, 1, 0))
        return p_re, p_im, h_re[SUBLANES - 1:SUBLANES], h_im[SUBLANES - 1:SUBLANES]

    def pair(d, pi, carry):
        c_re, c_im = carry
        r0 = pl.multiple_of(pi * rows, rows)
        base = d * 2 * half
        order = (1, 0) if d == 1 else (0, 1)
        res = [None, None]
        for which in order:
            rr = pl.ds(r0 + which * SUBLANES, SUBLANES)
            g_re = g_ref[rr, pl.ds(base, half)]
            g_im = g_ref[rr, pl.ds(base + half, half)]
            p_re, p_im, c_re, c_im = tile(d, g_re, g_im, c_re, c_im)
            res[which] = (p_re, p_im)
        o_ref[pl.ds(r0, rows), pl.ds(base, half)] = jnp.concatenate(
            [res[0][0], res[1][0]], axis=0).astype(BF16)
        o_ref[pl.ds(r0, rows), pl.ds(base + half, half)] = jnp.concatenate(
            [res[0][1], res[1][1]], axis=0).astype(BF16)
        return c_re, c_im

    zero = (jnp.zeros((1, half), F32), jnp.zeros((1, half), F32))
    lax.fori_loop(0, all_pairs, lambda pi, c: pair(0, pi, c), zero)
    c = lax.fori_loop(0, ctx_pairs, lambda n, c: pair(1, ctx_pairs - 1 - n, c), zero)
    lax.fori_loop(0, all_pairs - ctx_pairs, lambda n, c: pair(1, all_pairs - 1 - n, c), c)


def _s5_core_kernel(u_ref, win_ref, t_ref, wout_ref, c_ref, y_ref, g_scr, hp_scr, *, tm, ctx_pairs, half):
    x = _s5_fold_rows(u_ref, tm).astype(BF16)
    g_scr[...] = jnp.dot(x, win_ref[0], preferred_element_type=F32)
    _s5_scan(g_scr, c_ref, hp_scr, ctx_pairs=ctx_pairs, all_pairs=tm // (2 * SUBLANES), half=half)
    y = jnp.dot(x, t_ref[0], preferred_element_type=F32)
    y = y + jnp.dot(hp_scr[...], wout_ref[0], preferred_element_type=F32)
    for t in range(SUB):
        y_ref[pl.ds(t, tm, stride=SUB), :] = y[:, t * LANES:(t + 1) * LANES]


def s5_core(slabs2, w_in, t_intra, w_out, consts, layer, batch, ctx_len):
    _, r, width = slabs2.shape
    t = r // batch
    tm = t // SUB
    n_piece = t_intra.shape[0]
    half = PIECE_GROUPS * C_STATE
    assert w_in.shape[2] == 4 * half and tm % (2 * SUBLANES) == 0 and ctx_len % (2 * SUBLANES * SUB) == 0
    return pl.pallas_call(
        functools.partial(_s5_core_kernel, tm=tm, ctx_pairs=ctx_len // SUB // (2 * SUBLANES), half=half),
        grid=(n_piece, batch),
        in_specs=[pl.BlockSpec((None, t, LANES), lambda i, bb: (0, bb, i)),
                  pl.BlockSpec((1,) + w_in.shape[1:], lambda i, bb: (i, 0, 0)),
                  pl.BlockSpec((1,) + t_intra.shape[1:], lambda i, bb: (i, 0, 0)),
                  pl.BlockSpec((1,) + w_out.shape[1:], lambda i, bb: (i, 0, 0)),
                  pl.BlockSpec((None, 2, 4, 2, SUBLANES, half), lambda i, bb: (layer, 0, 0, 0, 0, i))],
        out_specs=pl.BlockSpec((t, LANES), lambda i, bb: (bb, i)),
        out_shape=jax.ShapeDtypeStruct((r, width), F32),
        scratch_shapes=[pltpu.VMEM((tm, 4 * half), F32), pltpu.VMEM((tm, 4 * half), BF16)],
        compiler_params=_cparams("parallel", "parallel"),
        name="s5_core",
    )(slabs2, w_in, t_intra, w_out, consts)


def _s5_finish_kernel(y_ref, u_ref, gc_ref, d_ref, w_ref, b_ref, o_ref):
    y = y_ref[...] + d_ref[...] * u_ref[0]
    y = jax.nn.gelu(y)
    z = jnp.dot(y.astype(BF16), w_ref[...], preferred_element_type=F32) + b_ref[...]
    o_ref[...] = (y * _sigmoid(z) * _silu(gc_ref[0])).astype(BF16)


def s5_finish(y, slabs2, d_skip, glu_w, glu_b, tm):
    r, w = y.shape
    return pl.pallas_call(
        _s5_finish_kernel,
        grid=(r // tm,),
        in_specs=[pl.BlockSpec((tm, w), lambda i: (i, 0)),
                  pl.BlockSpec((1, tm, w), lambda i: (0, i, 0)),
                  pl.BlockSpec((1, tm, w), lambda i: (2, i, 0)),
                  pl.BlockSpec((1, w), lambda i: (0, 0)),
                  pl.BlockSpec((w, w), lambda i: (0, 0)),
                  pl.BlockSpec((1, w), lambda i: (0, 0))],
        out_specs=pl.BlockSpec((tm, w), lambda i: (i, 0)),
        out_shape=jax.ShapeDtypeStruct((r, w), BF16),
        compiler_params=_cparams("parallel"),
        name="s5_finish",
    )(y, slabs2, slabs2, d_skip, glu_w, glu_b)


def _expand_block_diag(src, unit, row_unit):
    rows, cols = src.shape
    wide = cols * PIECE_GROUPS
    lg = lambda v: int(math.log2(v))
    k = lax.broadcasted_iota(jnp.int32, (cols, wide), 0)
    j = lax.broadcasted_iota(jnp.int32, (cols, wide), 1)
    src_col = ((j >> lg(PIECE_GROUPS * unit)) << lg(unit)) + (j & (unit - 1))
    spread = jnp.where(k == src_col, 1.0, 0.0).astype(BF16)
    out = jnp.dot(src.astype(BF16), spread, preferred_element_type=F32)
    rg = (lax.broadcasted_iota(jnp.int32, (rows, wide), 0) >> lg(row_unit)) & (PIECE_GROUPS - 1)
    ch = (lax.broadcasted_iota(jnp.int32, (rows, wide), 1) >> lg(unit)) & (PIECE_GROUPS - 1)
    return jnp.where(rg == ch, out, 0.0).astype(BF16)


def _s5_expand_kernel(kc_ref, wi_ref, wo_ref, t_ref, win_ref, wout_ref, tsrc_scr, *, p, n):
    lane = lax.broadcasted_iota(jnp.int32, (p, LANES), 1)
    for gl in range(PIECE_GROUPS):
        kf = kc_ref[0, 0, gl]
        kb = kc_ref[1, 0, gl]
        for s in range(SUB):
            shift_b = (LANES - p * (SUB - 1 - s)) % LANES
            f = kf if s == 0 else pltpu.roll(kf, p * s, 1)
            r = kb if shift_b == 0 else pltpu.roll(kb, shift_b, 1)
            blk = jnp.where(lane >= p * s, f, 0.0) + jnp.where(lane < p * (s + 1), r, 0.0)
            tsrc_scr[pl.ds((s * PIECE_GROUPS + gl) * p, p), :] = blk
    t_ref[0] = _expand_block_diag(tsrc_scr[...], p, p)
    rows_t = tsrc_scr.shape[0]
    half = win_ref.shape[2] // 2
    win_ref[0, :, :half] = _expand_block_diag(wi_ref[0].reshape(rows_t, LANES), n, p)
    win_ref[0, :, half:] = _expand_block_diag(wi_ref[1].reshape(rows_t, LANES), n, p)
    wout_ref[0] = _expand_block_diag(wo_ref[...].reshape(wout_ref.shape[1], LANES), p, n)


def s5_expand(kcat, win, wout_src, layer, p, n):
    npiece = kcat.shape[2]
    rows_t = SUB * PIECE_GROUPS * p
    rows_o = 2 * 2 * PIECE_GROUPS * n
    return pl.pallas_call(
        functools.partial(_s5_expand_kernel, p=p, n=n),
        grid=(npiece,),
        in_specs=[pl.BlockSpec((None, 2, 1, PIECE_GROUPS, p, LANES), lambda i: (layer, 0, i, 0, 0, 0)),
                  pl.BlockSpec((None, 2, SUB, 1, PIECE_GROUPS * p, LANES), lambda i: (layer, 0, 0, i, 0, 0)),
                  pl.BlockSpec((None, 2, 2, 1, PIECE_GROUPS * n, LANES), lambda i: (layer, 0, 0, i, 0, 0))],
        out_specs=[pl.BlockSpec((1, rows_t, LANES * PIECE_GROUPS), lambda i: (i, 0, 0)),
                   pl.BlockSpec((1, rows_t, 2 * LANES * PIECE_GROUPS), lambda i: (i, 0, 0)),
                   pl.BlockSpec((1, rows_o, LANES * PIECE_GROUPS), lambda i: (i, 0, 0))],
        out_shape=[jax.ShapeDtypeStruct((npiece, rows_t, LANES * PIECE_GROUPS), BF16),
                   jax.ShapeDtypeStruct((npiece, rows_t, 2 * LANES * PIECE_GROUPS), BF16),
                   jax.ShapeDtypeStruct((npiece, rows_o, LANES * PIECE_GROUPS), BF16)],
        scratch_shapes=[pltpu.VMEM((rows_t, LANES), F32)],
        compiler_params=_cparams("parallel"),
        name="s5_expand",
    )(kcat, win, wout_src)


def s5_derive(lam_re, lam_im, log_dt, b_re, b_im, c_re, c_im):
    hi = lax.Precision.HIGHEST
    n_dir, g, n = lam_re.shape
    p = b_re.shape[-1]
    npiece = g // PIECE_GROUPS
    assert n_dir == 2 and SUB * p == LANES
    dt = jnp.exp(log_dt)[:, :, None]
    up = list(range(SUB))
    down = up[::-1]

    def powers(exps):
        k = jnp.asarray(exps, F32)[:, :, None, None]
        mag = jnp.exp(k * (lam_re * dt)[:, None])
        ang = k * (lam_im * dt)[:, None]
        return mag * jnp.cos(ang), mag * jnp.sin(ang)

    a_re, a_im = powers([[1], [1]])
    den = lam_re * lam_re + lam_im * lam_im
    nr, ni = a_re[:, 0] - 1.0, a_im[:, 0]
    cf_re = ((nr * lam_re + ni * lam_im) / den)[:, :, None, :]
    cf_im = ((ni * lam_re - nr * lam_im) / den)[:, :, None, :]
    bt_re, bt_im = jnp.swapaxes(b_re, 2, 3), jnp.swapaxes(b_im, 2, 3)
    bb_re = cf_re * bt_re - cf_im * bt_im
    bb_im = cf_re * bt_im + cf_im * bt_re
    ct_re, ct_im = jnp.swapaxes(c_re, 2, 3), jnp.swapaxes(c_im, 2, 3)

    lane = jnp.arange(SUB * p)
    spread_k = (lane[None, :] // p == jnp.arange(SUB)[:, None]).astype(F32)
    spread_p = (lane[None, :] % p == jnp.arange(p)[:, None]).astype(F32)
    cw_re = jnp.dot(ct_re, spread_p, precision=hi)
    cw_im = jnp.dot(ct_im, spread_p, precision=hi)

    def c_times_powers(exps):
        e_re, e_im = powers(exps)
        e_re = jnp.einsum('dkgn,km->dgnm', e_re, spread_k, precision=hi)
        e_im = jnp.einsum('dkgn,km->dgnm', e_im, spread_k, precision=hi)
        return cw_re * e_re - cw_im * e_im, cw_re * e_im + cw_im * e_re

    ck_re, ck_im = c_times_powers([up, down])
    lhs = jnp.concatenate([bb_re, -bb_im], axis=-1)
    rhs = jnp.concatenate([ck_re, ck_im], axis=2)
    kcat = jnp.einsum('dgqk,dgkm->dgqm', lhs, rhs, precision=hi)
    kcat = kcat.reshape(n_dir, npiece, PIECE_GROUPS, p, SUB * p)

    e_re, e_im = powers([down, up])
    e1 = jnp.concatenate([e_re, e_re], axis=-1).reshape(n_dir, SUB, npiece, PIECE_GROUPS, 1, 2 * n)
    e2 = jnp.concatenate([-e_im, e_im], axis=-1).reshape(n_dir, SUB, npiece, PIECE_GROUPS, 1, 2 * n)
    b1 = jnp.concatenate([bb_re, bb_im], axis=-1).reshape(n_dir, 1, npiece, PIECE_GROUPS, p, 2 * n)
    b2 = jnp.concatenate([bb_im, bb_re], axis=-1).reshape(n_dir, 1, npiece, PIECE_GROUPS, p, 2 * n)
    win = (e1 * b1 + e2 * b2).reshape(n_dir, SUB, npiece, PIECE_GROUPS * p, 2 * n)

    co_re, co_im = c_times_powers([[k + 1 for k in up], [SUB - k for k in up]])
    wout_src = jnp.stack([co_re, -co_im], axis=1).reshape(n_dir, 2, npiece, PIECE_GROUPS * n, SUB * p)

    steps = [SUB * m for m in range(1, SUBLANES + 1)]
    q_re, q_im = powers([steps, steps])
    q_re = q_re.reshape(n_dir, SUBLANES, g * n)
    q_im = q_im.reshape(n_dir, SUBLANES, g * n)
    row = jnp.arange(SUBLANES)
    consts = []
    for d in range(n_dir):
        kinds = []
        for dd in (1, 2, 4):
            keep = (row + dd <= SUBLANES - 1) if d == 1 else (row >= dd)
            m = keep.astype(F32)[:, None]
            kinds.append(jnp.stack([m * q_re[d, dd - 1][None], m * q_im[d, dd - 1][None]]))
        sel = (SUBLANES - 1 - row) if d == 1 else row
        kinds.append(jnp.stack([q_re[d][sel], q_im[d][sel]]))
        consts.append(jnp.stack(kinds))
    consts = jnp.stack(consts)
    return kcat, win, wout_src, consts


def _lru_kernel(*refs, bwd, nch, tc, width):
    if bwd:
        (x_ref, xp_ref, xn_ref, cw_ref, cb_ref, lam_ref, wa_ref, ba_ref, wx_ref, bx_ref,
         hf_ref, gd_ref, o_ref, pad_scr, a_scr, b_scr, carry_scr) = refs
    else:
        (x_ref, xp_ref, xn_ref, cw_ref, cb_ref, lam_ref, wa_ref, ba_ref, wx_ref, bx_ref,
         o_ref, pad_scr, a_scr, b_scr, carry_scr) = refs
    k = pl.program_id(1)
    chunk = jnp.where(k == 0, 0, nch - k) if bwd else k
    prev_ok = jnp.logical_and(chunk != 0, chunk != 1)
    next_ok = jnp.logical_and(chunk != 0, chunk != nch - 1)
    pad_scr[0:SUBLANES] = jnp.where(prev_ok, xp_ref[0, 0], 0.0)
    pad_scr[SUBLANES:SUBLANES + tc] = x_ref[0, 0]
    pad_scr[SUBLANES + tc:2 * SUBLANES + tc] = jnp.where(next_ok, xn_ref[0, 0], 0.0)
    xc = cb_ref[...]
    for tap in range(CONV_W):
        xc = xc + cw_ref[tap:tap + 1] * pad_scr[SUBLANES - 1 + tap:SUBLANES - 1 + tap + tc]

    bw = width // D_BLOCKS
    r_parts, i_parts = [], []
    for blk in range(D_BLOCKS):
        xb = xc[:, blk * bw:(blk + 1) * bw].astype(BF16)
        r_parts.append(jnp.dot(xb, wa_ref[blk], preferred_element_type=F32))
        i_parts.append(jnp.dot(xb, wx_ref[blk], preferred_element_type=F32))
    r = _sigmoid(jnp.concatenate(r_parts, axis=1) + ba_ref[...])
    ig = _sigmoid(jnp.concatenate(i_parts, axis=1) + bx_ref[...])
    nl = -lam_ref[...]
    e = jnp.exp(-jnp.abs(nl))
    e1 = 1.0 + e
    log1p_e = jnp.where(e1 == 1.0, e, jnp.log(e1) * (e / jnp.where(e1 == 1.0, 1.0, e1 - 1.0)))
    softplus = jnp.maximum(nl, 0.0) + log1p_e
    log_a = (-LRU_C * softplus) * r
    a = jnp.exp(log_a)
    a_scr[...] = a
    b_scr[...] = jnp.sqrt(-jnp.tanh(log_a) * (1.0 + a * a)) * (ig * xc)

    @pl.when(k == 0)
    def _():
        carry_scr[...] = jnp.zeros_like(carry_scr)

    rowi = lax.broadcasted_iota(jnp.int32, (SUBLANES, width), 0)
    n_tiles = tc // SUBLANES

    def body(n, carry):
        ti = n_tiles - 1 - n if bwd else n
        rr = pl.ds(pl.multiple_of(ti * SUBLANES, SUBLANES), SUBLANES)
        a, b = a_scr[rr], b_scr[rr]
        for dd in (1, 2, 4):
            keep = (rowi + dd <= SUBLANES - 1) if bwd else (rowi >= dd)
            sh = SUBLANES - dd if bwd else dd
            b = b + a * jnp.where(keep, pltpu.roll(b, sh, 0), 0.0)
            a = a * jnp.where(keep, pltpu.roll(a, sh, 0), 1.0)
        h = b + a * carry
        b_scr[rr] = h
        return h[0:1] if bwd else h[SUBLANES - 1:SUBLANES]

    carry_scr[...] = lax.fori_loop(0, n_tiles, body, carry_scr[...])
    if bwd:
        o_ref[0] = ((hf_ref[0] + b_scr[...]) * _silu(gd_ref[0, 0])).astype(BF16)
    else:
        o_ref[0] = b_scr[...]


def lru_pass(slabs, conv_w, conv_b, lam, wa, ba, wx, bx, ctx_len, hf=None):
    _, b, t, w = slabs.shape
    bwd = hf is not None
    tc = ctx_len
    nch = t // tc
    hb = tc // SUBLANES
    nhb = t // SUBLANES

    def ch(k):
        return jnp.where(k == 0, 0, nch - k) if bwd else k

    in_specs = [pl.BlockSpec((1, 1, tc, w), lambda bb, k: (1, bb, ch(k), 0)),
                pl.BlockSpec((1, 1, SUBLANES, w), lambda bb, k: (1, bb, jnp.maximum(ch(k) * hb - 1, 0), 0)),
                pl.BlockSpec((1, 1, SUBLANES, w), lambda bb, k: (1, bb, jnp.minimum((ch(k) + 1) * hb, nhb - 1), 0)),
                pl.BlockSpec((CONV_W, w), lambda bb, k: (0, 0)),
                pl.BlockSpec((1, w), lambda bb, k: (0, 0)),
                pl.BlockSpec((1, w), lambda bb, k: (0, 0)),
                pl.BlockSpec(wa.shape, lambda bb, k: (0, 0, 0)),
                pl.BlockSpec((1, w), lambda bb, k: (0, 0)),
                pl.BlockSpec(wx.shape, lambda bb, k: (0, 0, 0)),
                pl.BlockSpec((1, w), lambda bb, k: (0, 0))]
    args = [slabs, slabs, slabs, conv_w, conv_b, lam, wa, ba, wx, bx]
    if bwd:
        in_specs += [pl.BlockSpec((1, tc, w), lambda bb, k: (bb, ch(k), 0)),
                     pl.BlockSpec((1, 1, tc, w), lambda bb, k: (3, bb, ch(k), 0))]
        args += [hf, slabs]
    return pl.pallas_call(
        functools.partial(_lru_kernel, bwd=bwd, nch=nch, tc=tc, width=w),
        grid=(b, nch),
        in_specs=in_specs,
        out_specs=pl.BlockSpec((1, tc, w), lambda bb, k: (bb, ch(k), 0)),
        out_shape=jax.ShapeDtypeStruct((b, t, w), BF16 if bwd else F32),
        scratch_shapes=[pltpu.VMEM((tc + 2 * SUBLANES, w), F32),
                        pltpu.VMEM((tc, w), F32),
                        pltpu.VMEM((tc, w), F32),
                        pltpu.VMEM((1, w), F32)],
        compiler_params=_cparams("parallel", "arbitrary"),
        name="lru_bwd" if bwd else "lru_fwd",
    )(*args)


def rope_tables(ctx_len, seq):
    rows = seq // GRID_W
    axis_dim = HEAD_DIM // 2
    row = jnp.repeat(jnp.arange(rows), GRID_W).astype(F32)
    col = jnp.tile(jnp.arange(GRID_W), rows).astype(F32)
    inv = ROPE_THETA ** (-jnp.arange(0, axis_dim, 2, dtype=F32) / axis_dim)
    ang = jnp.concatenate([row[:, None] * inv, col[:, None] * inv], axis=-1)
    cos, sin = jnp.cos(ang), jnp.sin(ang)
    cos_t = jnp.repeat(cos, 2, axis=-1)
    sin_t = jnp.stack([-sin, sin], axis=-1).reshape(seq, HEAD_DIM)
    cos_t = jnp.concatenate([jnp.ones((ctx_len, HEAD_DIM), F32), cos_t], axis=0)
    sin_t = jnp.concatenate([jnp.zeros((ctx_len, HEAD_DIM), F32), sin_t], axis=0)
    return cos_t, sin_t


def even_layer(x_all, mod, g, w_in, w_out, j, q_g, k_g, sgu_g, ws, bs, cos_t, sin_t, ctx_len, latent_only):
    shift_l, scale_l, gate_l, shift_c, scale_c, gate_c = mod
    slabs = inproj(x_all, shift_l, scale_l, shift_c, scale_c, g, w_in, j, ctx_len, tn=512,
                   tm=_pick_tile(x_all.shape[1], 1152), out_dtype=BF16)
    q_gain = (q_g * (HEAD_DIM ** -0.5 * math.log2(math.e)))[None, :]
    mix_a = attention(slabs, q_gain, k_g[None, :], cos_t, sin_t, ctx_len)
    mix_b = sgu(slabs, ws.astype(BF16), bs[:, :, None], sgu_g[None, :])
    return outproj(mix_a, mix_b, w_out, j, x_all, gate_l, gate_c, ctx_len, latent_only)


def odd_layer(x_all, mod, g, w_in, w_out, j, s5_ops, d_skip, glu_w, glu_b, conv_w, conv_b, lam, wa, ba, wx, bx,
              ctx_len, latent_only):
    shift_l, scale_l, gate_l, shift_c, scale_c, gate_c = mod
    b, t, _ = x_all.shape
    slabs = inproj(x_all, shift_l, scale_l, shift_c, scale_c, g, w_in, j, ctx_len, tn=1024,
                   tm=_pick_tile(t, 1152), out_dtype=F32)
    cw = slabs.shape[3]
    kcat, win, wout_src, consts = s5_ops
    t_intra, w_s5in, w_s5out = s5_expand(kcat, win, wout_src, j, kcat.shape[-2], C_STATE)
    slabs2 = slabs.reshape(slabs.shape[0], b * t, cw)
    y = s5_core(slabs2, w_s5in, t_intra, w_s5out, consts, j, b, ctx_len)
    mix_c = s5_finish(y, slabs2, d_skip[None, :], glu_w.astype(BF16), glu_b[None, :],
                      tm=_pick_tile(b * t, 512))
    hf = lru_pass(slabs, conv_w, conv_b[None, :], lam[0][None, :], wa[0].astype(BF16), ba[0][None, :],
                  wx[0].astype(BF16), bx[0][None, :], ctx_len)
    mix_d = lru_pass(slabs, conv_w, conv_b[None, :], lam[1][None, :], wa[1].astype(BF16), ba[1][None, :],
                     wx[1].astype(BF16), bx[1][None, :], ctx_len, hf=hf)
    return outproj(mix_c.reshape(b, t, cw), mix_d, w_out, j, x_all, gate_l, gate_c, ctx_len, latent_only)


def kernel(x, c, ctx, c_ctx, ada_w, ada_b, norm_g, ev_w_in, ev_w_out, ev_q_g, ev_k_g, ev_sgu_g, ev_ws, ev_bs,
           od_w_in, od_w_out, s5_lam_re, s5_lam_im, s5_log_dt, s5_b_re, s5_b_im, s5_c_re, s5_c_im, s5_d,
           s5_glu_w, s5_glu_b, lru_conv_w, lru_conv_b, lru_lam, lru_wa, lru_ba, lru_wx, lru_bx):
    b, seq, d = x.shape
    ctx_len = ctx.shape[1]
    depth = ada_w.shape[0]
    assert b + 1 <= SUBLANES and seq % ctx_len == 0
    cos_t, sin_t = rope_tables(ctx_len, seq)
    cc = jnp.zeros((2 * SUBLANES, d), F32).at[:b].set(c).at[b].set(c_ctx)
    mods = adaln_all(cc, ada_w, ada_b)
    x_all = jnp.concatenate([ctx, x], axis=1)
    s5_ops = jax.vmap(s5_derive)(s5_lam_re, s5_lam_im, s5_log_dt, s5_b_re, s5_b_im, s5_c_re, s5_c_im)
    ev_w_in, ev_w_out = ev_w_in.astype(BF16), ev_w_out.astype(BF16)
    od_w_in, od_w_out = od_w_in.astype(BF16), od_w_out.astype(BF16)
    for layer in range(depth):
        m = mods[layer]
        mod = tuple(m[:b, None, k * d:(k + 1) * d] for k in range(3)) + \
            tuple(m[b:b + 1, k * d:(k + 1) * d] for k in range(3))
        last = layer == depth - 1
        j = layer // 2
        g = norm_g[layer][None, :]
        if layer % 2 == 0:
            x_all = even_layer(x_all, mod, g, ev_w_in, ev_w_out, j, ev_q_g[j], ev_k_g[j], ev_sgu_g[j],
                               ev_ws[j], ev_bs[j], cos_t, sin_t, ctx_len, last)
        else:
            x_all = odd_layer(x_all, mod, g, od_w_in, od_w_out, j, s5_ops,
                              s5_d[j], s5_glu_w[j], s5_glu_b[j],
                              lru_conv_w[j], lru_conv_b[j], lru_lam[j], lru_wa[j], lru_ba[j], lru_wx[j],
                              lru_bx[j], ctx_len, last)
    return x_all
```

```python
import functools
import math

import jax
import jax.numpy as jnp
from jax import lax
from jax.experimental import pallas as pl
from jax.experimental.pallas import tpu as pltpu

F32 = jnp.float32
BF16 = jnp.bfloat16

EPS = 1e-6
GRID_W = 64
HEAD_DIM = 128
A_Q_HEADS = 12
A_KV_HEADS = 4
A_GROUP = A_Q_HEADS // A_KV_HEADS
B_GROUPS = 4
CHUNK = 128
ROPE_THETA = 10000.0
C_GROUP_DIM = 16
C_STATE = 64
D_BLOCKS = 8
CONV_W = 4
LRU_C = 8.0

LANES = 128
SUBLANES = 8
SUB = 8
PIECE_GROUPS = LANES // C_GROUP_DIM
KEY_CHUNKS = 3
VMEM_LIMIT = 50 * 1024 * 1024


def _cparams(*sem):
    return pltpu.CompilerParams(dimension_semantics=sem, vmem_limit_bytes=VMEM_LIMIT)


def _sigmoid(x):
    return 0.5 * jnp.tanh(0.5 * x) + 0.5


def _silu(x):
    return x * _sigmoid(x)


def _pick_tile(rows, target, align=16):
    best = None
    for cand in range(align, min(rows, target) + 1, align):
        if rows % cand == 0:
            best = cand
    assert best is not None, (rows, target)
    return best


def _split_bf16(x, parts):
    out = []
    for _ in range(parts):
        hi = x.astype(BF16)
        out.append(hi)
        x = x - hi.astype(F32)
    return out


def _adaln_kernel(cc_ref, w_ref, b_ref, o_ref):
    s_parts = _split_bf16(_silu(cc_ref[...]), 3)
    w_hi, w_lo = _split_bf16(w_ref[0], 2)
    acc = jnp.dot(s_parts[2], w_hi, preferred_element_type=F32)
    for s_p, w_p in ((s_parts[1], w_lo), (s_parts[0], w_lo), (s_parts[1], w_hi), (s_parts[0], w_hi)):
        acc = acc + jnp.dot(s_p, w_p, preferred_element_type=F32)
    o_ref[0] = acc + b_ref[0]


def adaln_all(cc, ada_w, ada_b):
    depth, d, n3 = ada_w.shape
    rows = cc.shape[0]
    tn = _pick_tile(n3, 1536, align=LANES)
    return pl.pallas_call(
        _adaln_kernel,
        grid=(depth, n3 // tn),
        in_specs=[pl.BlockSpec((rows, d), lambda l, j: (0, 0)),
                  pl.BlockSpec((1, d, tn), lambda l, j: (l, 0, j)),
                  pl.BlockSpec((1, 1, tn), lambda l, j: (l, 0, j))],
        out_specs=pl.BlockSpec((1, rows, tn), lambda l, j: (l, 0, j)),
        out_shape=jax.ShapeDtypeStruct((depth, rows, n3), F32),
        compiler_params=_cparams("parallel", "parallel"),
        name="adaln",
    )(cc, ada_w, ada_b.reshape(depth, 1, n3))


def _inproj_kernel(x_ref, shl_ref, scl_ref, shc_ref, scc_ref, g_ref, w_ref, *rest, tm, ctx_len, f32_slabs):
    if f32_slabs:
        of_ref, o_ref, h_scr = rest
    else:
        o_ref, h_scr = rest
    i = pl.program_id(1)
    j = pl.program_id(2)
    first = j == 0

    def modulate(rows, sc_ref, sh_ref):
        x = x_ref[0, rows]
        r = lax.rsqrt(jnp.mean(x * x, axis=-1, keepdims=True) + EPS)
        h_scr[rows] = ((x * r) * (g_ref[...] * (1.0 + sc_ref[...])) + sh_ref[...]).astype(BF16)

    scl, shl = scl_ref.at[0], shl_ref.at[0]
    if tm >= ctx_len:
        @pl.when(jnp.logical_and(first, i == 0))
        def _():
            modulate(slice(0, ctx_len), scc_ref, shc_ref)
            if tm > ctx_len:
                modulate(slice(ctx_len, tm), scl, shl)

        @pl.when(jnp.logical_and(first, i > 0))
        def _():
            modulate(slice(0, tm), scl, shl)
    else:
        @pl.when(jnp.logical_and(first, i < ctx_len // tm))
        def _():
            modulate(slice(0, tm), scc_ref, shc_ref)

        @pl.when(jnp.logical_and(first, i >= ctx_len // tm))
        def _():
            modulate(slice(0, tm), scl, shl)

    res = jnp.dot(h_scr[...], w_ref[...], preferred_element_type=F32)
    if f32_slabs:
        @pl.when(j < f32_slabs)
        def _():
            of_ref[0, 0] = res

        @pl.when(j >= f32_slabs)
        def _():
            o_ref[0, 0] = res.astype(BF16)
    else:
        o_ref[0, 0] = res.astype(BF16)


def inproj(x_all, shift_l, scale_l, shift_c, scale_c, g, w, layer, ctx_len, tn, tm, f32_slabs=0):
    b, t, d = x_all.shape
    n = w.shape[2]
    n_slab = n // tn
    assert tm >= ctx_len or ctx_len % tm == 0
    block = (1, 1, tm, tn)
    if f32_slabs:
        out_specs = [pl.BlockSpec(block, lambda bb, i, j: (jnp.minimum(j, f32_slabs - 1), bb, i, 0)),
                     pl.BlockSpec(block, lambda bb, i, j: (jnp.maximum(j - f32_slabs, 0), bb, i, 0))]
        out_shape = [jax.ShapeDtypeStruct((f32_slabs, b, t, tn), F32),
                     jax.ShapeDtypeStruct((n_slab - f32_slabs, b, t, tn), BF16)]
    else:
        out_specs = pl.BlockSpec(block, lambda bb, i, j: (j, bb, i, 0))
        out_shape = jax.ShapeDtypeStruct((n_slab, b, t, tn), BF16)
    return pl.pallas_call(
        functools.partial(_inproj_kernel, tm=tm, ctx_len=ctx_len, f32_slabs=f32_slabs),
        grid=(b, t // tm, n_slab),
        in_specs=[pl.BlockSpec((1, tm, d), lambda bb, i, j: (bb, i, 0)),
                  pl.BlockSpec((1, 1, d), lambda bb, i, j: (bb, 0, 0)),
                  pl.BlockSpec((1, 1, d), lambda bb, i, j: (bb, 0, 0)),
                  pl.BlockSpec((1, d), lambda bb, i, j: (0, 0)),
                  pl.BlockSpec((1, d), lambda bb, i, j: (0, 0)),
                  pl.BlockSpec((1, d), lambda bb, i, j: (0, 0)),
                  pl.BlockSpec((None, d, tn), lambda bb, i, j: (layer, 0, j))],
        out_specs=out_specs,
        out_shape=out_shape,
        scratch_shapes=[pltpu.VMEM((tm, d), BF16)],
        compiler_params=_cparams("parallel", "parallel", "arbitrary"),
        name="inproj",
    )(x_all, shift_l, scale_l, shift_c, scale_c, g, w)


def _outproj_kernel(ma_ref, mb_ref, w1_ref, w2_ref, x_ref, gl_ref, gc_ref, o_ref, *, ctx_tiles):
    i = pl.program_id(1)
    acc = jnp.dot(ma_ref[0], w1_ref[...], preferred_element_type=F32)
    acc = acc + jnp.dot(mb_ref[0], w2_ref[...], preferred_element_type=F32)
    gate = jnp.where(i < ctx_tiles, gc_ref[...], gl_ref[0])
    o_ref[0] = x_ref[0] + gate * acc


def outproj(ma, mb, w, layer, x_all, gate_l, gate_c, ctx_len, latent_only):
    b, t, d = x_all.shape
    k1, k2 = ma.shape[2], mb.shape[2]
    assert w.shape[1] == k1 + k2 and k1 % k2 == 0
    tm = ctx_len
    off = 1 if latent_only else 0
    t_out = t - ctx_len if latent_only else t
    return pl.pallas_call(
        functools.partial(_outproj_kernel, ctx_tiles=0 if latent_only else 1),
        grid=(b, t_out // tm),
        in_specs=[pl.BlockSpec((1, tm, k1), lambda bb, i: (bb, i + off, 0)),
                  pl.BlockSpec((1, tm, k2), lambda bb, i: (bb, i + off, 0)),
                  pl.BlockSpec((None, k1, d), lambda bb, i: (layer, 0, 0)),
                  pl.BlockSpec((None, k2, d), lambda bb, i: (layer, k1 // k2, 0)),
                  pl.BlockSpec((1, tm, d), lambda bb, i: (bb, i + off, 0)),
                  pl.BlockSpec((1, 1, d), lambda bb, i: (bb, 0, 0)),
                  pl.BlockSpec((1, d), lambda bb, i: (0, 0))],
        out_specs=pl.BlockSpec((1, tm, d), lambda bb, i: (bb, i, 0)),
        out_shape=jax.ShapeDtypeStruct((b, t_out, d), F32),
        compiler_params=_cparams("parallel", "parallel"),
        name="outproj",
    )(ma, mb, w, w, x_all, gate_l, gate_c)


def _norm_rope(x, gain, cos, sin):
    y = x * lax.rsqrt(jnp.mean(x * x, axis=-1, keepdims=True) + EPS) * gain
    lane = lax.broadcasted_iota(jnp.int32, y.shape, 1)
    partner = jnp.where((lane & 1) == 0, pltpu.roll(y, HEAD_DIM - 1, 1), pltpu.roll(y, 1, 1))
    return y * cos + partner * sin


def _attn_kernel(q0_ref, q1_ref, q2_ref, k_ref, v_ref, ga0_ref, ga1_ref, ga2_ref, qg_ref, kg_ref,
                 cos_ref, sin_ref, o_ref, k_scr, vaug_scr, *, ctx_len, tq):
    iq = pl.program_id(2)

    @pl.when(iq == 0)
    def _():
        k_scr[...] = _norm_rope(k_ref[0, 0].astype(F32), kg_ref[...], cos_ref[...], sin_ref[...]).astype(BF16)
        vaug_scr[:, :HEAD_DIM] = v_ref[0, 0].astype(BF16)
        vaug_scr[:, HEAD_DIM:] = jnp.ones((vaug_scr.shape[0], HEAD_DIM), BF16)

    def attend(n_keys):
        rows = pl.ds(pl.multiple_of(iq * tq, tq), tq)
        cos_q, sin_q = cos_ref[rows, :], sin_ref[rows, :]
        kc = n_keys // KEY_CHUNKS if n_keys % (KEY_CHUNKS * 2 * HEAD_DIM) == 0 else n_keys
        for g, (q_ref, ga_ref) in enumerate(((q0_ref, ga0_ref), (q1_ref, ga1_ref), (q2_ref, ga2_ref))):
            q = _norm_rope(q_ref[0, 0].astype(F32), qg_ref[...], cos_q, sin_q).astype(BF16)
            parts = []
            for c0 in range(0, n_keys, kc):
                s = lax.dot_general(q, k_scr[c0:c0 + kc], (((1,), (1,)), ((), ())), preferred_element_type=F32)
                m = jnp.max(s, axis=-1, keepdims=True)
                p = jnp.exp2(s - m).astype(BF16)
                parts.append((m, jnp.dot(p, vaug_scr[c0:c0 + kc], preferred_element_type=F32)))
            m_all = functools.reduce(jnp.maximum, [m for m, _ in parts])
            pv = functools.reduce(lambda a, b: a + b, [acc * jnp.exp2(m - m_all) for m, acc in parts])
            o = pv[:, :HEAD_DIM] / pv[:, HEAD_DIM:]
            o_ref[0, :, g * HEAD_DIM:(g + 1) * HEAD_DIM] = (o * _silu(ga_ref[0, 0].astype(F32))).astype(BF16)

    @pl.when(iq == 0)
    def _():
        attend(ctx_len)

    @pl.when(iq > 0)
    def _():
        attend(k_scr.shape[0])


def attention(slabs, q_gain, k_gain, cos_t, sin_t, ctx_len):
    _, b, t, tn = slabs.shape
    per = tn // HEAD_DIM
    tq = ctx_len
    gw = A_GROUP * HEAD_DIM
    assert A_GROUP == 3

    def head_spec(first_slab, g):
        def head_map(bb, kv, iq):
            h = kv * A_GROUP + g
            return (first_slab + h // per, bb, iq, h % per)
        return pl.BlockSpec((1, 1, tq, HEAD_DIM), head_map)

    vec = pl.BlockSpec((1, HEAD_DIM), lambda bb, kv, iq: (0, 0))
    table = pl.BlockSpec((t, HEAD_DIM), lambda bb, kv, iq: (0, 0))
    return pl.pallas_call(
        functools.partial(_attn_kernel, ctx_len=ctx_len, tq=tq),
        grid=(b, A_KV_HEADS, t // tq),
        in_specs=[head_spec(2, 0), head_spec(2, 1), head_spec(2, 2),
                  pl.BlockSpec((1, 1, t, HEAD_DIM), lambda bb, kv, iq: (0, bb, 0, kv)),
                  pl.BlockSpec((1, 1, t, HEAD_DIM), lambda bb, kv, iq: (1, bb, 0, kv)),
                  head_spec(7, 0), head_spec(7, 1), head_spec(7, 2),
                  vec, vec, table, table],
        out_specs=pl.BlockSpec((1, tq, gw), lambda bb, kv, iq: (bb, iq, kv)),
        out_shape=jax.ShapeDtypeStruct((b, t, A_Q_HEADS * HEAD_DIM), BF16),
        scratch_shapes=[pltpu.VMEM((t, HEAD_DIM), BF16), pltpu.VMEM((t, 2 * HEAD_DIM), BF16)],
        compiler_params=_cparams("parallel", "parallel", "arbitrary"),
        name="attention",
    )(slabs, slabs, slabs, slabs, slabs, slabs, slabs, slabs, q_gain, k_gain, cos_t, sin_t)


def _sgu_kernel(u_ref, v_ref, gb_ref, ws_ref, bs_ref, g_ref, o_ref, *, n_chunks):
    def chunk(c, carry):
        rows = pl.ds(pl.multiple_of(c * CHUNK, CHUNK), CHUNK)
        outs = []
        for grp in range(B_GROUPS):
            cols = slice(grp * LANES, (grp + 1) * LANES)
            v = v_ref[0, 0, rows, cols].astype(F32)
            vn = v * lax.rsqrt(jnp.mean(v * v, axis=-1, keepdims=True) + EPS) * g_ref[:, cols]
            mixed = jnp.dot(ws_ref[grp], vn.astype(BF16), preferred_element_type=F32) + bs_ref[grp]
            outs.append(u_ref[0, 0, rows, cols].astype(F32) * mixed
                        * _silu(gb_ref[0, 0, rows, cols].astype(F32)))
        o_ref[0, rows, :] = jnp.concatenate(outs, axis=1).astype(BF16)
        return carry

    lax.fori_loop(0, n_chunks, chunk, 0)


def sgu(slabs, ws, bs, g):
    _, b, t, tn = slabs.shape
    spec = lambda s: pl.BlockSpec((1, 1, t, tn), lambda bb: (s, bb, 0, 0))
    return pl.pallas_call(
        functools.partial(_sgu_kernel, n_chunks=t // CHUNK),
        grid=(b,),
        in_specs=[spec(5), spec(6), spec(10),
                  pl.BlockSpec((B_GROUPS, CHUNK, CHUNK), lambda bb: (0, 0, 0)),
                  pl.BlockSpec((B_GROUPS, CHUNK, 1), lambda bb: (0, 0, 0)),
                  pl.BlockSpec((1, tn), lambda bb: (0, 0))],
        out_specs=pl.BlockSpec((1, t, tn), lambda bb: (bb, 0, 0)),
        out_shape=jax.ShapeDtypeStruct((b, t, tn), BF16),
        compiler_params=_cparams("parallel"),
        name="sgu",
    )(slabs, slabs, slabs, ws, bs, g)


def _s5_fold_rows(u_ref, tm):
    return jnp.concatenate([u_ref[pl.ds(t, tm, stride=SUB), :] for t in range(SUB)], axis=1)


def _s5_scan(g_ref, c_ref, o_ref, *, ctx_pairs, all_pairs, half):
    rows = 2 * SUBLANES
    rowi = lax.broadcasted_iota(jnp.int32, (SUBLANES, half), 0)

    def tile(d, g_re, g_im, c_re, c_im):
        bwd = d == 1
        b_re, b_im = g_re, g_im
        for idx, dd in enumerate((1, 2, 4)):
            a_re, a_im = c_ref[d, idx, 0], c_ref[d, idx, 1]
            sh = SUBLANES - dd if bwd else dd
            s_re, s_im = pltpu.roll(b_re, sh, 0), pltpu.roll(b_im, sh, 0)
            b_re, b_im = (b_re + a_re * s_re - a_im * s_im,
                          b_im + a_re * s_im + a_im * s_re)
        a_re, a_im = c_ref[d, 3, 0], c_ref[d, 3, 1]
        h_re = b_re + a_re * c_re - a_im * c_im
        h_im = b_im + a_re * c_im + a_im * c_re
        if bwd:
            p_re = jnp.where(rowi == SUBLANES - 1, c_re, pltpu.roll(h_re, SUBLANES - 1, 0))
            p_im = jnp.where(rowi == SUBLANES - 1, c_im, pltpu.roll(h_im, SUBLANES - 1, 0))
            return p_re, p_im, h_re[0:1], h_im[0:1]
        p_re = jnp.where(rowi == 0, c_re, pltpu.roll(h_re, 1, 0))
        p_im = jnp.where(rowi == 0, c_im, pltpu.roll(h_im, 1, 0))
        return p_re, p_im, h_re[SUBLANES - 1:SUBLANES], h_im[SUBLANES - 1:SUBLANES]

    def pair(d, pi, carry):
        c_re, c_im = carry
        r0 = pl.multiple_of(pi * rows, rows)
        base = d * 2 * half
        order = (1, 0) if d == 1 else (0, 1)
        res = [None, None]
        for which in order:
            rr = pl.ds(r0 + which * SUBLANES, SUBLANES)
            g_re = g_ref[rr, pl.ds(base, half)]
            g_im = g_ref[rr, pl.ds(base + half, half)]
            p_re, p_im, c_re, c_im = tile(d, g_re, g_im, c_re, c_im)
            res[which] = (p_re, p_im)
        o_ref[pl.ds(r0, rows), pl.ds(base, half)] = jnp.concatenate(
            [res[0][0], res[1][0]], axis=0).astype(BF16)
        o_ref[pl.ds(r0, rows), pl.ds(base + half, half)] = jnp.concatenate(
            [res[0][1], res[1][1]], axis=0).astype(BF16)
        return c_re, c_im

    zero = (jnp.zeros((1, half), F32), jnp.zeros((1, half), F32))
    lax.fori_loop(0, all_pairs, lambda pi, c: pair(0, pi, c), zero)
    c = lax.fori_loop(0, ctx_pairs, lambda n, c: pair(1, ctx_pairs - 1 - n, c), zero)
    lax.fori_loop(0, all_pairs - ctx_pairs, lambda n, c: pair(1, all_pairs - 1 - n, c), c)


def _s5_core_kernel(u_ref, win_ref, t_ref, wout_ref, c_ref, y_ref, g_scr, hp_scr, *, tm, ctx_pairs, half):
    x = _s5_fold_rows(u_ref, tm).astype(BF16)
    g_scr[...] = jnp.dot(x, win_ref[0], preferred_element_type=F32)
    _s5_scan(g_scr, c_ref, hp_scr, ctx_pairs=ctx_pairs, all_pairs=tm // (2 * SUBLANES), half=half)
    y = jnp.dot(x, t_ref[0], preferred_element_type=F32)
    y = y + jnp.dot(hp_scr[...], wout_ref[0], preferred_element_type=F32)
    for t in range(SUB):
        y_ref[pl.ds(t, tm, stride=SUB), :] = y[:, t * LANES:(t + 1) * LANES]


def s5_core(slabs2, w_in, t_intra, w_out, consts, layer, batch, ctx_len):
    _, r, width = slabs2.shape
    t = r // batch
    tm = t // SUB
    n_piece = t_intra.shape[0]
    half = PIECE_GROUPS * C_STATE
    assert w_in.shape[2] == 4 * half and tm % (2 * SUBLANES) == 0 and ctx_len % (2 * SUBLANES * SUB) == 0
    return pl.pallas_call(
        functools.partial(_s5_core_kernel, tm=tm, ctx_pairs=ctx_len // SUB // (2 * SUBLANES), half=half),
        grid=(n_piece, batch),
        in_specs=[pl.BlockSpec((None, t, LANES), lambda i, bb: (0, bb, i)),
                  pl.BlockSpec((1,) + w_in.shape[1:], lambda i, bb: (i, 0, 0)),
                  pl.BlockSpec((1,) + t_intra.shape[1:], lambda i, bb: (i, 0, 0)),
                  pl.BlockSpec((1,) + w_out.shape[1:], lambda i, bb: (i, 0, 0)),
                  pl.BlockSpec((None, 2, 4, 2, SUBLANES, half), lambda i, bb: (layer, 0, 0, 0, 0, i))],
        out_specs=pl.BlockSpec((t, LANES), lambda i, bb: (bb, i)),
        out_shape=jax.ShapeDtypeStruct((r, width), F32),
        scratch_shapes=[pltpu.VMEM((tm, 4 * half), F32), pltpu.VMEM((tm, 4 * half), BF16)],
        compiler_params=_cparams("parallel", "parallel"),
        name="s5_core",
    )(slabs2, w_in, t_intra, w_out, consts)


def _s5_finish_kernel(y_ref, u_ref, gc_ref, d_ref, w_ref, b_ref, o_ref):
    y = y_ref[...] + d_ref[...] * u_ref[0]
    y = jax.nn.gelu(y)
    z = jnp.dot(y.astype(BF16), w_ref[...], preferred_element_type=F32) + b_ref[...]
    o_ref[...] = (y * _sigmoid(z) * _silu(gc_ref[0].astype(F32))).astype(BF16)


def s5_finish(y, u_slabs, gate_slabs, d_skip, glu_w, glu_b, tm):
    r, w = y.shape
    return pl.pallas_call(
        _s5_finish_kernel,
        grid=(r // tm,),
        in_specs=[pl.BlockSpec((tm, w), lambda i: (i, 0)),
                  pl.BlockSpec((1, tm, w), lambda i: (0, i, 0)),
                  pl.BlockSpec((1, tm, w), lambda i: (1, i, 0)),
                  pl.BlockSpec((1, w), lambda i: (0, 0)),
                  pl.BlockSpec((w, w), lambda i: (0, 0)),
                  pl.BlockSpec((1, w), lambda i: (0, 0))],
        out_specs=pl.BlockSpec((tm, w), lambda i: (i, 0)),
        out_shape=jax.ShapeDtypeStruct((r, w), BF16),
        compiler_params=_cparams("parallel"),
        name="s5_finish",
    )(y, u_slabs, gate_slabs, d_skip, glu_w, glu_b)


def _expand_block_diag(src, unit, row_unit):
    rows, cols = src.shape
    wide = cols * PIECE_GROUPS
    lg = lambda v: int(math.log2(v))
    k = lax.broadcasted_iota(jnp.int32, (cols, wide), 0)
    j = lax.broadcasted_iota(jnp.int32, (cols, wide), 1)
    src_col = ((j >> lg(PIECE_GROUPS * unit)) << lg(unit)) + (j & (unit - 1))
    spread = jnp.where(k == src_col, 1.0, 0.0).astype(BF16)
    out = jnp.dot(src.astype(BF16), spread, preferred_element_type=F32)
    rg = (lax.broadcasted_iota(jnp.int32, (rows, wide), 0) >> lg(row_unit)) & (PIECE_GROUPS - 1)
    ch = (lax.broadcasted_iota(jnp.int32, (rows, wide), 1) >> lg(unit)) & (PIECE_GROUPS - 1)
    return jnp.where(rg == ch, out, 0.0).astype(BF16)


def _s5_expand_kernel(kc_ref, wi_ref, wo_ref, t_ref, win_ref, wout_ref, tsrc_scr, *, p, n):
    lane = lax.broadcasted_iota(jnp.int32, (p, LANES), 1)
    for gl in range(PIECE_GROUPS):
        kf = kc_ref[0, 0, gl]
        kb = kc_ref[1, 0, gl]
        for s in range(SUB):
            shift_b = (LANES - p * (SUB - 1 - s)) % LANES
            f = kf if s == 0 else pltpu.roll(kf, p * s, 1)
            r = kb if shift_b == 0 else pltpu.roll(kb, shift_b, 1)
            blk = jnp.where(lane >= p * s, f, 0.0) + jnp.where(lane < p * (s + 1), r, 0.0)
            tsrc_scr[pl.ds((s * PIECE_GROUPS + gl) * p, p), :] = blk
    t_ref[0] = _expand_block_diag(tsrc_scr[...], p, p)
    rows_t = tsrc_scr.shape[0]
    half = win_ref.shape[2] // 2
    win_ref[0, :, :half] = _expand_block_diag(wi_ref[0].reshape(rows_t, LANES), n, p)
    win_ref[0, :, half:] = _expand_block_diag(wi_ref[1].reshape(rows_t, LANES), n, p)
    wout_ref[0] = _expand_block_diag(wo_ref[...].reshape(wout_ref.shape[1], LANES), p, n)


def s5_expand(kcat, win, wout_src, layer, p, n):
    npiece = kcat.shape[2]
    rows_t = SUB * PIECE_GROUPS * p
    rows_o = 2 * 2 * PIECE_GROUPS * n
    return pl.pallas_call(
        functools.partial(_s5_expand_kernel, p=p, n=n),
        grid=(npiece,),
        in_specs=[pl.BlockSpec((None, 2, 1, PIECE_GROUPS, p, LANES), lambda i: (layer, 0, i, 0, 0, 0)),
                  pl.BlockSpec((None, 2, SUB, 1, PIECE_GROUPS * p, LANES), lambda i: (layer, 0, 0, i, 0, 0)),
                  pl.BlockSpec((None, 2, 2, 1, PIECE_GROUPS * n, LANES), lambda i: (layer, 0, 0, i, 0, 0))],
        out_specs=[pl.BlockSpec((1, rows_t, LANES * PIECE_GROUPS), lambda i: (i, 0, 0)),
                   pl.BlockSpec((1, rows_t, 2 * LANES * PIECE_GROUPS), lambda i: (i, 0, 0)),
                   pl.BlockSpec((1, rows_o, LANES * PIECE_GROUPS), lambda i: (i, 0, 0))],
        out_shape=[jax.ShapeDtypeStruct((npiece, rows_t, LANES * PIECE_GROUPS), BF16),
                   jax.ShapeDtypeStruct((npiece, rows_t, 2 * LANES * PIECE_GROUPS), BF16),
                   jax.ShapeDtypeStruct((npiece, rows_o, LANES * PIECE_GROUPS), BF16)],
        scratch_shapes=[pltpu.VMEM((rows_t, LANES), F32)],
        compiler_params=_cparams("parallel"),
        name="s5_expand",
    )(kcat, win, wout_src)


def s5_derive(lam_re, lam_im, log_dt, b_re, b_im, c_re, c_im):
    hi = lax.Precision.HIGHEST
    n_dir, g, n = lam_re.shape
    p = b_re.shape[-1]
    npiece = g // PIECE_GROUPS
    assert n_dir == 2 and SUB * p == LANES
    dt = jnp.exp(log_dt)[:, :, None]
    up = list(range(SUB))
    down = up[::-1]

    def powers(exps):
        k = jnp.asarray(exps, F32)[:, :, None, None]
        mag = jnp.exp(k * (lam_re * dt)[:, None])
        ang = k * (lam_im * dt)[:, None]
        return mag * jnp.cos(ang), mag * jnp.sin(ang)

    a_re, a_im = powers([[1], [1]])
    den = lam_re * lam_re + lam_im * lam_im
    nr, ni = a_re[:, 0] - 1.0, a_im[:, 0]
    cf_re = ((nr * lam_re + ni * lam_im) / den)[:, :, None, :]
    cf_im = ((ni * lam_re - nr * lam_im) / den)[:, :, None, :]
    bt_re, bt_im = jnp.swapaxes(b_re, 2, 3), jnp.swapaxes(b_im, 2, 3)
    bb_re = cf_re * bt_re - cf_im * bt_im
    bb_im = cf_re * bt_im + cf_im * bt_re
    ct_re, ct_im = jnp.swapaxes(c_re, 2, 3), jnp.swapaxes(c_im, 2, 3)

    lane = jnp.arange(SUB * p)
    spread_k = (lane[None, :] // p == jnp.arange(SUB)[:, None]).astype(F32)
    spread_p = (lane[None, :] % p == jnp.arange(p)[:, None]).astype(F32)
    cw_re = jnp.dot(ct_re, spread_p, precision=hi)
    cw_im = jnp.dot(ct_im, spread_p, precision=hi)

    def c_times_powers(exps):
        e_re, e_im = powers(exps)
        e_re = jnp.einsum('dkgn,km->dgnm', e_re, spread_k, precision=hi)
        e_im = jnp.einsum('dkgn,km->dgnm', e_im, spread_k, precision=hi)
        return cw_re * e_re - cw_im * e_im, cw_re * e_im + cw_im * e_re

    ck_re, ck_im = c_times_powers([up, down])
    lhs = jnp.concatenate([bb_re, -bb_im], axis=-1)
    rhs = jnp.concatenate([ck_re, ck_im], axis=2)
    kcat = jnp.einsum('dgqk,dgkm->dgqm', lhs, rhs, precision=hi)
    kcat = kcat.reshape(n_dir, npiece, PIECE_GROUPS, p, SUB * p)

    e_re, e_im = powers([down, up])
    e1 = jnp.concatenate([e_re, e_re], axis=-1).reshape(n_dir, SUB, npiece, PIECE_GROUPS, 1, 2 * n)
    e2 = jnp.concatenate([-e_im, e_im], axis=-1).reshape(n_dir, SUB, npiece, PIECE_GROUPS, 1, 2 * n)
    b1 = jnp.concatenate([bb_re, bb_im], axis=-1).reshape(n_dir, 1, npiece, PIECE_GROUPS, p, 2 * n)
    b2 = jnp.concatenate([bb_im, bb_re], axis=-1).reshape(n_dir, 1, npiece, PIECE_GROUPS, p, 2 * n)
    win = (e1 * b1 + e2 * b2).reshape(n_dir, SUB, npiece, PIECE_GROUPS * p, 2 * n)

    co_re, co_im = c_times_powers([[k + 1 for k in up], [SUB - k for k in up]])
    wout_src = jnp.stack([co_re, -co_im], axis=1).reshape(n_dir, 2, npiece, PIECE_GROUPS * n, SUB * p)

    steps = [SUB * m for m in range(1, SUBLANES + 1)]
    q_re, q_im = powers([steps, steps])
    q_re = q_re.reshape(n_dir, SUBLANES, g * n)
    q_im = q_im.reshape(n_dir, SUBLANES, g * n)
    row = jnp.arange(SUBLANES)
    consts = []
    for d in range(n_dir):
        kinds = []
        for dd in (1, 2, 4):
            keep = (row + dd <= SUBLANES - 1) if d == 1 else (row >= dd)
            m = keep.astype(F32)[:, None]
            kinds.append(jnp.stack([m * q_re[d, dd - 1][None], m * q_im[d, dd - 1][None]]))
        sel = (SUBLANES - 1 - row) if d == 1 else row
        kinds.append(jnp.stack([q_re[d][sel], q_im[d][sel]]))
        consts.append(jnp.stack(kinds))
    consts = jnp.stack(consts)
    return kcat, win, wout_src, consts


def _lru_kernel(*refs, bwd, nch, tc, width):
    if bwd:
        (x_ref, xp_ref, xn_ref, cw_ref, cb_ref, lam_ref, wa_ref, ba_ref, wx_ref, bx_ref,
         hf_ref, gd_ref, o_ref, pad_scr, a_scr, b_scr, carry_scr) = refs
    else:
        (x_ref, xp_ref, xn_ref, cw_ref, cb_ref, lam_ref, wa_ref, ba_ref, wx_ref, bx_ref,
         o_ref, pad_scr, a_scr, b_scr, carry_scr) = refs
    k = pl.program_id(1)
    chunk = jnp.where(k == 0, 0, nch - k) if bwd else k
    prev_ok = jnp.logical_and(chunk != 0, chunk != 1)
    next_ok = jnp.logical_and(chunk != 0, chunk != nch - 1)
    pad_scr[0:SUBLANES] = jnp.where(prev_ok, xp_ref[0, 0, SUBLANES:].astype(F32), 0.0)
    pad_scr[SUBLANES:SUBLANES + tc] = x_ref[0, 0].astype(F32)
    pad_scr[SUBLANES + tc:2 * SUBLANES + tc] = jnp.where(next_ok, xn_ref[0, 0, :SUBLANES].astype(F32), 0.0)
    xc = cb_ref[...]
    for tap in range(CONV_W):
        xc = xc + cw_ref[tap:tap + 1] * pad_scr[SUBLANES - 1 + tap:SUBLANES - 1 + tap + tc]

    bw = width // D_BLOCKS
    r_parts, i_parts = [], []
    for blk in range(D_BLOCKS):
        xb = xc[:, blk * bw:(blk + 1) * bw].astype(BF16)
        r_parts.append(jnp.dot(xb, wa_ref[blk], preferred_element_type=F32))
        i_parts.append(jnp.dot(xb, wx_ref[blk], preferred_element_type=F32))
    r = _sigmoid(jnp.concatenate(r_parts, axis=1) + ba_ref[...])
    ig = _sigmoid(jnp.concatenate(i_parts, axis=1) + bx_ref[...])
    nl = -lam_ref[...]
    e = jnp.exp(-jnp.abs(nl))
    e1 = 1.0 + e
    log1p_e = jnp.where(e1 == 1.0, e, jnp.log(e1) * (e / jnp.where(e1 == 1.0, 1.0, e1 - 1.0)))
    softplus = jnp.maximum(nl, 0.0) + log1p_e
    log_a = (-LRU_C * softplus) * r
    a = jnp.exp(log_a)
    a_scr[...] = a
    b_scr[...] = jnp.sqrt(-jnp.tanh(log_a) * (1.0 + a * a)) * (ig * xc)

    @pl.when(k == 0)
    def _():
        carry_scr[...] = jnp.zeros_like(carry_scr)

    rowi = lax.broadcasted_iota(jnp.int32, (SUBLANES, width), 0)
    n_tiles = tc // SUBLANES

    def body(n, carry):
        ti = n_tiles - 1 - n if bwd else n
        rr = pl.ds(pl.multiple_of(ti * SUBLANES, SUBLANES), SUBLANES)
        a, b = a_scr[rr], b_scr[rr]
        for dd in (1, 2, 4):
            keep = (rowi + dd <= SUBLANES - 1) if bwd else (rowi >= dd)
            sh = SUBLANES - dd if bwd else dd
            b = b + a * jnp.where(keep, pltpu.roll(b, sh, 0), 0.0)
            a = a * jnp.where(keep, pltpu.roll(a, sh, 0), 1.0)
        h = b + a * carry
        b_scr[rr] = h
        return h[0:1] if bwd else h[SUBLANES - 1:SUBLANES]

    carry_scr[...] = lax.fori_loop(0, n_tiles, body, carry_scr[...])
    if bwd:
        o_ref[0] = ((hf_ref[0] + b_scr[...]) * _silu(gd_ref[0, 0].astype(F32))).astype(BF16)
    else:
        o_ref[0] = b_scr[...]


def lru_pass(slabs, conv_w, conv_b, lam, wa, ba, wx, bx, ctx_len, hf=None):
    _, b, t, w = slabs.shape
    bwd = hf is not None
    tc = ctx_len
    nch = t // tc
    halo = 2 * SUBLANES
    hb = tc // halo
    nhb = t // halo

    def ch(k):
        return jnp.where(k == 0, 0, nch - k) if bwd else k

    in_specs = [pl.BlockSpec((1, 1, tc, w), lambda bb, k: (0, bb, ch(k), 0)),
                pl.BlockSpec((1, 1, halo, w), lambda bb, k: (0, bb, jnp.maximum(ch(k) * hb - 1, 0), 0)),
                pl.BlockSpec((1, 1, halo, w), lambda bb, k: (0, bb, jnp.minimum((ch(k) + 1) * hb, nhb - 1), 0)),
                pl.BlockSpec((CONV_W, w), lambda bb, k: (0, 0)),
                pl.BlockSpec((1, w), lambda bb, k: (0, 0)),
                pl.BlockSpec((1, w), lambda bb, k: (0, 0)),
                pl.BlockSpec(wa.shape, lambda bb, k: (0, 0, 0)),
                pl.BlockSpec((1, w), lambda bb, k: (0, 0)),
                pl.BlockSpec(wx.shape, lambda bb, k: (0, 0, 0)),
                pl.BlockSpec((1, w), lambda bb, k: (0, 0))]
    args = [slabs, slabs, slabs, conv_w, conv_b, lam, wa, ba, wx, bx]
    if bwd:
        in_specs += [pl.BlockSpec((1, tc, w), lambda bb, k: (bb, ch(k), 0)),
                     pl.BlockSpec((1, 1, tc, w), lambda bb, k: (2, bb, ch(k), 0))]
        args += [hf, slabs]
    return pl.pallas_call(
        functools.partial(_lru_kernel, bwd=bwd, nch=nch, tc=tc, width=w),
        grid=(b, nch),
        in_specs=in_specs,
        out_specs=pl.BlockSpec((1, tc, w), lambda bb, k: (bb, ch(k), 0)),
        out_shape=jax.ShapeDtypeStruct((b, t, w), BF16 if bwd else F32),
        scratch_shapes=[pltpu.VMEM((tc + 2 * SUBLANES, w), F32),
                        pltpu.VMEM((tc, w), F32),
                        pltpu.VMEM((tc, w), F32),
                        pltpu.VMEM((1, w), F32)],
        compiler_params=_cparams("parallel", "arbitrary"),
        name="lru_bwd" if bwd else "lru_fwd",
    )(*args)


def rope_tables(ctx_len, seq):
    rows = seq // GRID_W
    axis_dim = HEAD_DIM // 2
    row = jnp.repeat(jnp.arange(rows), GRID_W).astype(F32)
    col = jnp.tile(jnp.arange(GRID_W), rows).astype(F32)
    inv = ROPE_THETA ** (-jnp.arange(0, axis_dim, 2, dtype=F32) / axis_dim)
    ang = jnp.concatenate([row[:, None] * inv, col[:, None] * inv], axis=-1)
    cos, sin = jnp.cos(ang), jnp.sin(ang)
    cos_t = jnp.repeat(cos, 2, axis=-1)
    sin_t = jnp.stack([-sin, sin], axis=-1).reshape(seq, HEAD_DIM)
    cos_t = jnp.concatenate([jnp.ones((ctx_len, HEAD_DIM), F32), cos_t], axis=0)
    sin_t = jnp.concatenate([jnp.zeros((ctx_len, HEAD_DIM), F32), sin_t], axis=0)
    return cos_t, sin_t


def even_layer(x_all, mod, g, w_in, w_out, j, q_g, k_g, sgu_g, ws, bs, cos_t, sin_t, ctx_len, latent_only):
    shift_l, scale_l, gate_l, shift_c, scale_c, gate_c = mod
    slabs = inproj(x_all, shift_l, scale_l, shift_c, scale_c, g, w_in, j, ctx_len, tn=512,
                   tm=_pick_tile(x_all.shape[1], 1152))
    q_gain = (q_g * (HEAD_DIM ** -0.5 * math.log2(math.e)))[None, :]
    mix_a = attention(slabs, q_gain, k_g[None, :], cos_t, sin_t, ctx_len)
    mix_b = sgu(slabs, ws.astype(BF16), bs[:, :, None], sgu_g[None, :])
    return outproj(mix_a, mix_b, w_out, j, x_all, gate_l, gate_c, ctx_len, latent_only)


def odd_layer(x_all, mod, g, w_in, w_out, j, s5_ops, d_skip, glu_w, glu_b, conv_w, conv_b, lam, wa, ba, wx, bx,
              ctx_len, latent_only):
    shift_l, scale_l, gate_l, shift_c, scale_c, gate_c = mod
    b, t, _ = x_all.shape
    u_slab, slabs = inproj(x_all, shift_l, scale_l, shift_c, scale_c, g, w_in, j, ctx_len, tn=1024,
                           tm=_pick_tile(t, 768), f32_slabs=1)
    cw = slabs.shape[3]
    kcat, win, wout_src, consts = s5_ops
    t_intra, w_s5in, w_s5out = s5_expand(kcat, win, wout_src, j, kcat.shape[-2], C_STATE)
    u2 = u_slab.reshape(1, b * t, cw)
    y = s5_core(u2, w_s5in, t_intra, w_s5out, consts, j, b, ctx_len)
    mix_c = s5_finish(y, u2, slabs.reshape(slabs.shape[0], b * t, cw), d_skip[None, :], glu_w.astype(BF16),
                      glu_b[None, :], tm=_pick_tile(b * t, 512))
    hf = lru_pass(slabs, conv_w, conv_b[None, :], lam[0][None, :], wa[0].astype(BF16), ba[0][None, :],
                  wx[0].astype(BF16), bx[0][None, :], ctx_len)
    mix_d = lru_pass(slabs, conv_w, conv_b[None, :], lam[1][None, :], wa[1].astype(BF16), ba[1][None, :],
                     wx[1].astype(BF16), bx[1][None, :], ctx_len, hf=hf)
    return outproj(mix_c.reshape(b, t, cw), mix_d, w_out, j, x_all, gate_l, gate_c, ctx_len, latent_only)


def kernel(x, c, ctx, c_ctx, ada_w, ada_b, norm_g, ev_w_in, ev_w_out, ev_q_g, ev_k_g, ev_sgu_g, ev_ws, ev_bs,
           od_w_in, od_w_out, s5_lam_re, s5_lam_im, s5_log_dt, s5_b_re, s5_b_im, s5_c_re, s5_c_im, s5_d,
           s5_glu_w, s5_glu_b, lru_conv_w, lru_conv_b, lru_lam, lru_wa, lru_ba, lru_wx, lru_bx):
    b, seq, d = x.shape
    ctx_len = ctx.shape[1]
    depth = ada_w.shape[0]
    assert b + 1 <= SUBLANES and seq % ctx_len == 0
    cos_t, sin_t = rope_tables(ctx_len, seq)
    cc = jnp.zeros((2 * SUBLANES, d), F32).at[:b].set(c).at[b].set(c_ctx)
    mods = adaln_all(cc, ada_w, ada_b)
    x_all = jnp.concatenate([ctx, x], axis=1)
    s5_ops = jax.vmap(s5_derive)(s5_lam_re, s5_lam_im, s5_log_dt, s5_b_re, s5_b_im, s5_c_re, s5_c_im)
    ev_w_in, ev_w_out = ev_w_in.astype(BF16), ev_w_out.astype(BF16)
    od_w_in, od_w_out = od_w_in.astype(BF16), od_w_out.astype(BF16)
    for layer in range(depth):
        m = mods[layer]
        mod = tuple(m[:b, None, k * d:(k + 1) * d] for k in range(3)) + \
            tuple(m[b:b + 1, k * d:(k + 1) * d] for k in range(3))
        last = layer == depth - 1
        j = layer // 2
        g = norm_g[layer][None, :]
        if layer % 2 == 0:
            x_all = even_layer(x_all, mod, g, ev_w_in, ev_w_out, j, ev_q_g[j], ev_k_g[j], ev_sgu_g[j],
                               ev_ws[j], ev_bs[j], cos_t, sin_t, ctx_len, last)
        else:
            x_all = odd_layer(x_all, mod, g, od_w_in, od_w_out, j, s5_ops,
                              s5_d[j], s5_glu_w[j], s5_glu_b[j],
                              lru_conv_w[j], lru_conv_b[j], lru_lam[j], lru_wa[j], lru_ba[j], lru_wx[j],
                              lru_bx[j], ctx_len, last)
    return x_all
```

```python
import functools
import math

import jax
import jax.numpy as jnp
from jax import lax
from jax.experimental import pallas as pl
from jax.experimental.pallas import tpu as pltpu

F32 = jnp.float32
BF16 = jnp.bfloat16

EPS = 1e-6
GRID_W = 64
HEAD_DIM = 128
A_Q_HEADS = 12
A_KV_HEADS = 4
A_GROUP = A_Q_HEADS // A_KV_HEADS
B_GROUPS = 4
CHUNK = 128
ROPE_THETA = 10000.0
C_GROUP_DIM = 16
C_STATE = 64
D_BLOCKS = 8
CONV_W = 4
LRU_C = 8.0

LANES = 128
SUBLANES = 8
SUB = 8
PIECE_GROUPS = LANES // C_GROUP_DIM
KEY_CHUNKS = 3
VMEM_LIMIT = 50 * 1024 * 1024


def _cparams(*sem):
    return pltpu.CompilerParams(dimension_semantics=sem, vmem_limit_bytes=VMEM_LIMIT)


def _sigmoid(x):
    return 0.5 * jnp.tanh(0.5 * x) + 0.5


def _silu(x):
    return x * _sigmoid(x)


def _pick_tile(rows, target, align=16):
    best = None
    for cand in range(align, min(rows, target) + 1, align):
        if rows % cand == 0:
            best = cand
    assert best is not None, (rows, target)
    return best


def _split_bf16(x, parts):
    out = []
    for _ in range(parts):
        hi = x.astype(BF16)
        out.append(hi)
        x = x - hi.astype(F32)
    return out


def _adaln_kernel(cc_ref, w_ref, b_ref, o_ref):
    s_parts = _split_bf16(_silu(cc_ref[...]), 3)
    w = w_ref[0].astype(BF16)
    acc = jnp.dot(s_parts[2], w, preferred_element_type=F32)
    acc = acc + jnp.dot(s_parts[1], w, preferred_element_type=F32)
    acc = acc + jnp.dot(s_parts[0], w, preferred_element_type=F32)
    o_ref[0] = acc + b_ref[0]


def adaln_all(cc, ada_w, ada_b):
    depth, d, n3 = ada_w.shape
    rows = cc.shape[0]
    tn = _pick_tile(n3, 1536, align=LANES)
    return pl.pallas_call(
        _adaln_kernel,
        grid=(depth, n3 // tn),
        in_specs=[pl.BlockSpec((rows, d), lambda l, j: (0, 0)),
                  pl.BlockSpec((1, d, tn), lambda l, j: (l, 0, j)),
                  pl.BlockSpec((1, 1, tn), lambda l, j: (l, 0, j))],
        out_specs=pl.BlockSpec((1, rows, tn), lambda l, j: (l, 0, j)),
        out_shape=jax.ShapeDtypeStruct((depth, rows, n3), F32),
        compiler_params=_cparams("parallel", "parallel"),
        name="adaln",
    )(cc, ada_w, ada_b.reshape(depth, 1, n3))


def _inproj_kernel(x_ref, shl_ref, scl_ref, shc_ref, scc_ref, g_ref, w_ref, o_ref, h_scr, *, tm, ctx_len):
    i = pl.program_id(1)
    first = pl.program_id(2) == 0

    def modulate(rows, sc_ref, sh_ref):
        x = x_ref[0, rows]
        r = lax.rsqrt(jnp.mean(x * x, axis=-1, keepdims=True) + EPS)
        h_scr[rows] = ((x * r) * (g_ref[...] * (1.0 + sc_ref[...])) + sh_ref[...]).astype(BF16)

    scl, shl = scl_ref.at[0], shl_ref.at[0]
    if tm >= ctx_len:
        @pl.when(jnp.logical_and(first, i == 0))
        def _():
            modulate(slice(0, ctx_len), scc_ref, shc_ref)
            if tm > ctx_len:
                modulate(slice(ctx_len, tm), scl, shl)

        @pl.when(jnp.logical_and(first, i > 0))
        def _():
            modulate(slice(0, tm), scl, shl)
    else:
        @pl.when(jnp.logical_and(first, i < ctx_len // tm))
        def _():
            modulate(slice(0, tm), scc_ref, shc_ref)

        @pl.when(jnp.logical_and(first, i >= ctx_len // tm))
        def _():
            modulate(slice(0, tm), scl, shl)

    o_ref[0, 0] = jnp.dot(h_scr[...], w_ref[...], preferred_element_type=F32).astype(o_ref.dtype)


def inproj(x_all, shift_l, scale_l, shift_c, scale_c, g, w, layer, ctx_len, tn, tm, out_dtype):
    b, t, d = x_all.shape
    n = w.shape[2]
    n_slab = n // tn
    assert tm >= ctx_len or ctx_len % tm == 0
    return pl.pallas_call(
        functools.partial(_inproj_kernel, tm=tm, ctx_len=ctx_len),
        grid=(b, t // tm, n_slab),
        in_specs=[pl.BlockSpec((1, tm, d), lambda bb, i, j: (bb, i, 0)),
                  pl.BlockSpec((1, 1, d), lambda bb, i, j: (bb, 0, 0)),
                  pl.BlockSpec((1, 1, d), lambda bb, i, j: (bb, 0, 0)),
                  pl.BlockSpec((1, d), lambda bb, i, j: (0, 0)),
                  pl.BlockSpec((1, d), lambda bb, i, j: (0, 0)),
                  pl.BlockSpec((1, d), lambda bb, i, j: (0, 0)),
                  pl.BlockSpec((None, d, tn), lambda bb, i, j: (layer, 0, j))],
        out_specs=pl.BlockSpec((1, 1, tm, tn), lambda bb, i, j: (j, bb, i, 0)),
        out_shape=jax.ShapeDtypeStruct((n_slab, b, t, tn), out_dtype),
        scratch_shapes=[pltpu.VMEM((tm, d), BF16)],
        compiler_params=_cparams("parallel", "parallel", "arbitrary"),
        name="inproj",
    )(x_all, shift_l, scale_l, shift_c, scale_c, g, w)


def _outproj_kernel(ma_ref, mb_ref, w1_ref, w2_ref, x_ref, gl_ref, gc_ref, o_ref, *, ctx_tiles):
    i = pl.program_id(1)
    acc = jnp.dot(ma_ref[0], w1_ref[...], preferred_element_type=F32)
    acc = acc + jnp.dot(mb_ref[0], w2_ref[...], preferred_element_type=F32)
    gate = jnp.where(i < ctx_tiles, gc_ref[...], gl_ref[0])
    o_ref[0] = x_ref[0] + gate * acc


def outproj(ma, mb, w, layer, x_all, gate_l, gate_c, ctx_len, latent_only):
    b, t, d = x_all.shape
    k1, k2 = ma.shape[2], mb.shape[2]
    assert w.shape[1] == k1 + k2 and k1 % k2 == 0
    tm = ctx_len
    off = 1 if latent_only else 0
    t_out = t - ctx_len if latent_only else t
    return pl.pallas_call(
        functools.partial(_outproj_kernel, ctx_tiles=0 if latent_only else 1),
        grid=(b, t_out // tm),
        in_specs=[pl.BlockSpec((1, tm, k1), lambda bb, i: (bb, i + off, 0)),
                  pl.BlockSpec((1, tm, k2), lambda bb, i: (bb, i + off, 0)),
                  pl.BlockSpec((None, k1, d), lambda bb, i: (layer, 0, 0)),
                  pl.BlockSpec((None, k2, d), lambda bb, i: (layer, k1 // k2, 0)),
                  pl.BlockSpec((1, tm, d), lambda bb, i: (bb, i + off, 0)),
                  pl.BlockSpec((1, 1, d), lambda bb, i: (bb, 0, 0)),
                  pl.BlockSpec((1, d), lambda bb, i: (0, 0))],
        out_specs=pl.BlockSpec((1, tm, d), lambda bb, i: (bb, i, 0)),
        out_shape=jax.ShapeDtypeStruct((b, t_out, d), F32),
        compiler_params=_cparams("parallel", "parallel"),
        name="outproj",
    )(ma, mb, w, w, x_all, gate_l, gate_c)


def _norm_rope(x, gain, cos, sin):
    y = x * lax.rsqrt(jnp.mean(x * x, axis=-1, keepdims=True) + EPS) * gain
    lane = lax.broadcasted_iota(jnp.int32, y.shape, 1)
    partner = jnp.where((lane & 1) == 0, pltpu.roll(y, HEAD_DIM - 1, 1), pltpu.roll(y, 1, 1))
    return y * cos + partner * sin


def _attn_kernel(q0_ref, q1_ref, q2_ref, k_ref, v_ref, ga0_ref, ga1_ref, ga2_ref, qg_ref, kg_ref,
                 cos_ref, sin_ref, o_ref, k_scr, vaug_scr, *, ctx_len, tq):
    iq = pl.program_id(2)

    @pl.when(iq == 0)
    def _():
        k_scr[...] = _norm_rope(k_ref[0, 0].astype(F32), kg_ref[...], cos_ref[...], sin_ref[...]).astype(BF16)
        vaug_scr[:, :HEAD_DIM] = v_ref[0, 0].astype(BF16)
        vaug_scr[:, HEAD_DIM:] = jnp.ones((vaug_scr.shape[0], HEAD_DIM), BF16)

    def attend(n_keys):
        rows = pl.ds(pl.multiple_of(iq * tq, tq), tq)
        cos_q, sin_q = cos_ref[rows, :], sin_ref[rows, :]
        kc = n_keys // KEY_CHUNKS if n_keys % (KEY_CHUNKS * 2 * HEAD_DIM) == 0 else n_keys
        for g, (q_ref, ga_ref) in enumerate(((q0_ref, ga0_ref), (q1_ref, ga1_ref), (q2_ref, ga2_ref))):
            q = _norm_rope(q_ref[0, 0].astype(F32), qg_ref[...], cos_q, sin_q).astype(BF16)
            parts = []
            for c0 in range(0, n_keys, kc):
                s = lax.dot_general(q, k_scr[c0:c0 + kc], (((1,), (1,)), ((), ())), preferred_element_type=F32)
                m = jnp.max(s, axis=-1, keepdims=True)
                p = jnp.exp2(s - m).astype(BF16)
                parts.append((m, jnp.dot(p, vaug_scr[c0:c0 + kc], preferred_element_type=F32)))
            m_all = functools.reduce(jnp.maximum, [m for m, _ in parts])
            pv = functools.reduce(lambda a, b: a + b, [acc * jnp.exp2(m - m_all) for m, acc in parts])
            o = pv[:, :HEAD_DIM] / pv[:, HEAD_DIM:]
            o_ref[0, :, g * HEAD_DIM:(g + 1) * HEAD_DIM] = (o * _silu(ga_ref[0, 0].astype(F32))).astype(BF16)

    @pl.when(iq == 0)
    def _():
        attend(ctx_len)

    @pl.when(iq > 0)
    def _():
        attend(k_scr.shape[0])


def attention(slabs, q_gain, k_gain, cos_t, sin_t, ctx_len):
    _, b, t, tn = slabs.shape
    per = tn // HEAD_DIM
    tq = ctx_len
    gw = A_GROUP * HEAD_DIM
    assert A_GROUP == 3

    def head_spec(first_slab, g):
        def head_map(bb, kv, iq):
            h = kv * A_GROUP + g
            return (first_slab + h // per, bb, iq, h % per)
        return pl.BlockSpec((1, 1, tq, HEAD_DIM), head_map)

    vec = pl.BlockSpec((1, HEAD_DIM), lambda bb, kv, iq: (0, 0))
    table = pl.BlockSpec((t, HEAD_DIM), lambda bb, kv, iq: (0, 0))
    return pl.pallas_call(
        functools.partial(_attn_kernel, ctx_len=ctx_len, tq=tq),
        grid=(b, A_KV_HEADS, t // tq),
        in_specs=[head_spec(2, 0), head_spec(2, 1), head_spec(2, 2),
                  pl.BlockSpec((1, 1, t, HEAD_DIM), lambda bb, kv, iq: (0, bb, 0, kv)),
                  pl.BlockSpec((1, 1, t, HEAD_DIM), lambda bb, kv, iq: (1, bb, 0, kv)),
                  head_spec(7, 0), head_spec(7, 1), head_spec(7, 2),
                  vec, vec, table, table],
        out_specs=pl.BlockSpec((1, tq, gw), lambda bb, kv, iq: (bb, iq, kv)),
        out_shape=jax.ShapeDtypeStruct((b, t, A_Q_HEADS * HEAD_DIM), BF16),
        scratch_shapes=[pltpu.VMEM((t, HEAD_DIM), BF16), pltpu.VMEM((t, 2 * HEAD_DIM), BF16)],
        compiler_params=_cparams("parallel", "parallel", "arbitrary"),
        name="attention",
    )(slabs, slabs, slabs, slabs, slabs, slabs, slabs, slabs, q_gain, k_gain, cos_t, sin_t)


def _sgu_kernel(u_ref, v_ref, gb_ref, ws_ref, bs_ref, g_ref, o_ref, *, n_chunks):
    def chunk(c, carry):
        rows = pl.ds(pl.multiple_of(c * CHUNK, CHUNK), CHUNK)
        outs = []
        for grp in range(B_GROUPS):
            cols = slice(grp * LANES, (grp + 1) * LANES)
            v = v_ref[0, 0, rows, cols].astype(F32)
            vn = v * lax.rsqrt(jnp.mean(v * v, axis=-1, keepdims=True) + EPS) * g_ref[:, cols]
            mixed = jnp.dot(ws_ref[grp], vn.astype(BF16), preferred_element_type=F32) + bs_ref[grp]
            outs.append(u_ref[0, 0, rows, cols].astype(F32) * mixed
                        * _silu(gb_ref[0, 0, rows, cols].astype(F32)))
        o_ref[0, rows, :] = jnp.concatenate(outs, axis=1).astype(BF16)
        return carry

    lax.fori_loop(0, n_chunks, chunk, 0)


def sgu(slabs, ws, bs, g):
    _, b, t, tn = slabs.shape
    spec = lambda s: pl.BlockSpec((1, 1, t, tn), lambda bb: (s, bb, 0, 0))
    return pl.pallas_call(
        functools.partial(_sgu_kernel, n_chunks=t // CHUNK),
        grid=(b,),
        in_specs=[spec(5), spec(6), spec(10),
                  pl.BlockSpec((B_GROUPS, CHUNK, CHUNK), lambda bb: (0, 0, 0)),
                  pl.BlockSpec((B_GROUPS, CHUNK, 1), lambda bb: (0, 0, 0)),
                  pl.BlockSpec((1, tn), lambda bb: (0, 0))],
        out_specs=pl.BlockSpec((1, t, tn), lambda bb: (bb, 0, 0)),
        out_shape=jax.ShapeDtypeStruct((b, t, tn), BF16),
        compiler_params=_cparams("parallel"),
        name="sgu",
    )(slabs, slabs, slabs, ws, bs, g)


def _s5_fold_rows(u_ref, tm):
    return jnp.concatenate([u_ref[pl.ds(t, tm, stride=SUB), :] for t in range(SUB)], axis=1)


def _s5_scan(g_ref, c_ref, o_ref, *, ctx_pairs, all_pairs, half):
    rows = 2 * SUBLANES
    rowi = lax.broadcasted_iota(jnp.int32, (SUBLANES, half), 0)

    def tile(d, g_re, g_im, c_re, c_im):
        bwd = d == 1
        b_re, b_im = g_re, g_im
        for idx, dd in enumerate((1, 2, 4)):
            a_re, a_im = c_ref[d, idx, 0], c_ref[d, idx, 1]
            sh = SUBLANES - dd if bwd else dd
            s_re, s_im = pltpu.roll(b_re, sh, 0), pltpu.roll(b_im, sh, 0)
            b_re, b_im = (b_re + a_re * s_re - a_im * s_im,
                          b_im + a_re * s_im + a_im * s_re)
        a_re, a_im = c_ref[d, 3, 0], c_ref[d, 3, 1]
        h_re = b_re + a_re * c_re - a_im * c_im
        h_im = b_im + a_re * c_im + a_im * c_re
        if bwd:
            p_re = jnp.where(rowi == SUBLANES - 1, c_re, pltpu.roll(h_re, SUBLANES - 1, 0))
            p_im = jnp.where(rowi == SUBLANES - 1, c_im, pltpu.roll(h_im, SUBLANES - 1, 0))
            return p_re, p_im, h_re[0:1], h_im[0:1]
        p_re = jnp.where(rowi == 0, c_re, pltpu.roll(h_re, 1, 0))
        p_im = jnp.where(rowi == 0, c_im, pltpu.roll(h_im, 1, 0))
        return p_re, p_im, h_re[SUBLANES - 1:SUBLANES], h_im[SUBLANES - 1:SUBLANES]

    def pair(d, pi, carry):
        c_re, c_im = carry
        r0 = pl.multiple_of(pi * rows, rows)
        base = d * 2 * half
        order = (1, 0) if d == 1 else (0, 1)
        res = [None, None]
        for which in order:
            rr = pl.ds(r0 + which * SUBLANES, SUBLANES)
            g_re = g_ref[rr, pl.ds(base, half)]
            g_im = g_ref[rr, pl.ds(base + half, half)]
            p_re, p_im, c_re, c_im = tile(d, g_re, g_im, c_re, c_im)
            res[which] = (p_re, p_im)
        o_ref[pl.ds(r0, rows), pl.ds(base, half)] = jnp.concatenate(
            [res[0][0], res[1][0]], axis=0).astype(BF16)
        o_ref[pl.ds(r0, rows), pl.ds(base + half, half)] = jnp.concatenate(
            [res[0][1], res[1][1]], axis=0).astype(BF16)
        return c_re, c_im

    zero = (jnp.zeros((1, half), F32), jnp.zeros((1, half), F32))
    lax.fori_loop(0, all_pairs, lambda pi, c: pair(0, pi, c), zero)
    c = lax.fori_loop(0, ctx_pairs, lambda n, c: pair(1, ctx_pairs - 1 - n, c), zero)
    lax.fori_loop(0, all_pairs - ctx_pairs, lambda n, c: pair(1, all_pairs - 1 - n, c), c)


def _s5_core_kernel(u_ref, win_ref, t_ref, wout_ref, c_ref, y_ref, g_scr, hp_scr, *, tm, ctx_pairs, half):
    x = _s5_fold_rows(u_ref, tm).astype(BF16)
    g_scr[...] = jnp.dot(x, win_ref[0], preferred_element_type=F32)
    _s5_scan(g_scr, c_ref, hp_scr, ctx_pairs=ctx_pairs, all_pairs=tm // (2 * SUBLANES), half=half)
    y = jnp.dot(x, t_ref[0], preferred_element_type=F32)
    y = y + jnp.dot(hp_scr[...], wout_ref[0], preferred_element_type=F32)
    for t in range(SUB):
        y_ref[pl.ds(t, tm, stride=SUB), :] = y[:, t * LANES:(t + 1) * LANES]


def s5_core(slabs2, w_in, t_intra, w_out, consts, layer, batch, ctx_len):
    _, r, width = slabs2.shape
    t = r // batch
    tm = t // SUB
    n_piece = t_intra.shape[0]
    half = PIECE_GROUPS * C_STATE
    assert w_in.shape[2] == 4 * half and tm % (2 * SUBLANES) == 0 and ctx_len % (2 * SUBLANES * SUB) == 0
    return pl.pallas_call(
        functools.partial(_s5_core_kernel, tm=tm, ctx_pairs=ctx_len // SUB // (2 * SUBLANES), half=half),
        grid=(n_piece, batch),
        in_specs=[pl.BlockSpec((None, t, LANES), lambda i, bb: (0, bb, i)),
                  pl.BlockSpec((1,) + w_in.shape[1:], lambda i, bb: (i, 0, 0)),
                  pl.BlockSpec((1,) + t_intra.shape[1:], lambda i, bb: (i, 0, 0)),
                  pl.BlockSpec((1,) + w_out.shape[1:], lambda i, bb: (i, 0, 0)),
                  pl.BlockSpec((None, 2, 4, 2, SUBLANES, half), lambda i, bb: (layer, 0, 0, 0, 0, i))],
        out_specs=pl.BlockSpec((t, LANES), lambda i, bb: (bb, i)),
        out_shape=jax.ShapeDtypeStruct((r, width), F32),
        scratch_shapes=[pltpu.VMEM((tm, 4 * half), F32), pltpu.VMEM((tm, 4 * half), BF16)],
        compiler_params=_cparams("parallel", "parallel"),
        name="s5_core",
    )(slabs2, w_in, t_intra, w_out, consts)


def _s5_finish_kernel(y_ref, u_ref, gc_ref, d_ref, w_ref, b_ref, o_ref):
    y = y_ref[...] + d_ref[...] * u_ref[0]
    y = jax.nn.gelu(y)
    z = jnp.dot(y.astype(BF16), w_ref[...], preferred_element_type=F32) + b_ref[...]
    o_ref[...] = (y * _sigmoid(z) * _silu(gc_ref[0])).astype(BF16)


def s5_finish(y, slabs2, d_skip, glu_w, glu_b, tm):
    r, w = y.shape
    return pl.pallas_call(
        _s5_finish_kernel,
        grid=(r // tm,),
        in_specs=[pl.BlockSpec((tm, w), lambda i: (i, 0)),
                  pl.BlockSpec((1, tm, w), lambda i: (0, i, 0)),
                  pl.BlockSpec((1, tm, w), lambda i: (2, i, 0)),
                  pl.BlockSpec((1, w), lambda i: (0, 0)),
                  pl.BlockSpec((w, w), lambda i: (0, 0)),
                  pl.BlockSpec((1, w), lambda i: (0, 0))],
        out_specs=pl.BlockSpec((tm, w), lambda i: (i, 0)),
        out_shape=jax.ShapeDtypeStruct((r, w), BF16),
        compiler_params=_cparams("parallel"),
        name="s5_finish",
    )(y, slabs2, slabs2, d_skip, glu_w, glu_b)


def _expand_block_diag(src, unit, row_unit):
    rows, cols = src.shape
    wide = cols * PIECE_GROUPS
    lg = lambda v: int(math.log2(v))
    k = lax.broadcasted_iota(jnp.int32, (cols, wide), 0)
    j = lax.broadcasted_iota(jnp.int32, (cols, wide), 1)
    src_col = ((j >> lg(PIECE_GROUPS * unit)) << lg(unit)) + (j & (unit - 1))
    spread = jnp.where(k == src_col, 1.0, 0.0).astype(BF16)
    out = jnp.dot(src.astype(BF16), spread, preferred_element_type=F32)
    rg = (lax.broadcasted_iota(jnp.int32, (rows, wide), 0) >> lg(row_unit)) & (PIECE_GROUPS - 1)
    ch = (lax.broadcasted_iota(jnp.int32, (rows, wide), 1) >> lg(unit)) & (PIECE_GROUPS - 1)
    return jnp.where(rg == ch, out, 0.0).astype(BF16)


def _s5_expand_kernel(kc_ref, wi_ref, wo_ref, t_ref, win_ref, wout_ref, tsrc_scr, *, p, n):
    lane = lax.broadcasted_iota(jnp.int32, (p, LANES), 1)
    for gl in range(PIECE_GROUPS):
        kf = kc_ref[0, 0, gl]
        kb = kc_ref[1, 0, gl]
        for s in range(SUB):
            shift_b = (LANES - p * (SUB - 1 - s)) % LANES
            f = kf if s == 0 else pltpu.roll(kf, p * s, 1)
            r = kb if shift_b == 0 else pltpu.roll(kb, shift_b, 1)
            blk = jnp.where(lane >= p * s, f, 0.0) + jnp.where(lane < p * (s + 1), r, 0.0)
            tsrc_scr[pl.ds((s * PIECE_GROUPS + gl) * p, p), :] = blk
    t_ref[0] = _expand_block_diag(tsrc_scr[...], p, p)
    rows_t = tsrc_scr.shape[0]
    half = win_ref.shape[2] // 2
    win_ref[0, :, :half] = _expand_block_diag(wi_ref[0].reshape(rows_t, LANES), n, p)
    win_ref[0, :, half:] = _expand_block_diag(wi_ref[1].reshape(rows_t, LANES), n, p)
    wout_ref[0] = _expand_block_diag(wo_ref[...].reshape(wout_ref.shape[1], LANES), p, n)


def s5_expand(kcat, win, wout_src, layer, p, n):
    npiece = kcat.shape[2]
    rows_t = SUB * PIECE_GROUPS * p
    rows_o = 2 * 2 * PIECE_GROUPS * n
    return pl.pallas_call(
        functools.partial(_s5_expand_kernel, p=p, n=n),
        grid=(npiece,),
        in_specs=[pl.BlockSpec((None, 2, 1, PIECE_GROUPS, p, LANES), lambda i: (layer, 0, i, 0, 0, 0)),
                  pl.BlockSpec((None, 2, SUB, 1, PIECE_GROUPS * p, LANES), lambda i: (layer, 0, 0, i, 0, 0)),
                  pl.BlockSpec((None, 2, 2, 1, PIECE_GROUPS * n, LANES), lambda i: (layer, 0, 0, i, 0, 0))],
        out_specs=[pl.BlockSpec((1, rows_t, LANES * PIECE_GROUPS), lambda i: (i, 0, 0)),
                   pl.BlockSpec((1, rows_t, 2 * LANES * PIECE_GROUPS), lambda i: (i, 0, 0)),
                   pl.BlockSpec((1, rows_o, LANES * PIECE_GROUPS), lambda i: (i, 0, 0))],
        out_shape=[jax.ShapeDtypeStruct((npiece, rows_t, LANES * PIECE_GROUPS), BF16),
                   jax.ShapeDtypeStruct((npiece, rows_t, 2 * LANES * PIECE_GROUPS), BF16),
                   jax.ShapeDtypeStruct((npiece, rows_o, LANES * PIECE_GROUPS), BF16)],
        scratch_shapes=[pltpu.VMEM((rows_t, LANES), F32)],
        compiler_params=_cparams("parallel"),
        name="s5_expand",
    )(kcat, win, wout_src)


def s5_derive(lam_re, lam_im, log_dt, b_re, b_im, c_re, c_im):
    hi = lax.Precision.HIGHEST
    n_dir, g, n = lam_re.shape
    p = b_re.shape[-1]
    npiece = g // PIECE_GROUPS
    assert n_dir == 2 and SUB * p == LANES
    dt = jnp.exp(log_dt)[:, :, None]
    up = list(range(SUB))
    down = up[::-1]

    def powers(exps):
        k = jnp.asarray(exps, F32)[:, :, None, None]
        mag = jnp.exp(k * (lam_re * dt)[:, None])
        ang = k * (lam_im * dt)[:, None]
        return mag * jnp.cos(ang), mag * jnp.sin(ang)

    a_re, a_im = powers([[1], [1]])
    den = lam_re * lam_re + lam_im * lam_im
    nr, ni = a_re[:, 0] - 1.0, a_im[:, 0]
    cf_re = ((nr * lam_re + ni * lam_im) / den)[:, :, None, :]
    cf_im = ((ni * lam_re - nr * lam_im) / den)[:, :, None, :]
    bt_re, bt_im = jnp.swapaxes(b_re, 2, 3), jnp.swapaxes(b_im, 2, 3)
    bb_re = cf_re * bt_re - cf_im * bt_im
    bb_im = cf_re * bt_im + cf_im * bt_re
    ct_re, ct_im = jnp.swapaxes(c_re, 2, 3), jnp.swapaxes(c_im, 2, 3)

    lane = jnp.arange(SUB * p)
    spread_k = (lane[None, :] // p == jnp.arange(SUB)[:, None]).astype(F32)
    spread_p = (lane[None, :] % p == jnp.arange(p)[:, None]).astype(F32)
    cw_re = jnp.dot(ct_re, spread_p, precision=hi)
    cw_im = jnp.dot(ct_im, spread_p, precision=hi)

    def c_times_powers(exps):
        e_re, e_im = powers(exps)
        e_re = jnp.einsum('dkgn,km->dgnm', e_re, spread_k, precision=hi)
        e_im = jnp.einsum('dkgn,km->dgnm', e_im, spread_k, precision=hi)
        return cw_re * e_re - cw_im * e_im, cw_re * e_im + cw_im * e_re

    ck_re, ck_im = c_times_powers([up, down])
    lhs = jnp.concatenate([bb_re, -bb_im], axis=-1)
    rhs = jnp.concatenate([ck_re, ck_im], axis=2)
    kcat = jnp.einsum('dgqk,dgkm->dgqm', lhs, rhs, precision=hi)
    kcat = kcat.reshape(n_dir, npiece, PIECE_GROUPS, p, SUB * p)

    e_re, e_im = powers([down, up])
    e1 = jnp.concatenate([e_re, e_re], axis=-1).reshape(n_dir, SUB, npiece, PIECE_GROUPS, 1, 2 * n)
    e2 = jnp.concatenate([-e_im, e_im], axis=-1).reshape(n_dir, SUB, npiece, PIECE_GROUPS, 1, 2 * n)
    b1 = jnp.concatenate([bb_re, bb_im], axis=-1).reshape(n_dir, 1, npiece, PIECE_GROUPS, p, 2 * n)
    b2 = jnp.concatenate([bb_im, bb_re], axis=-1).reshape(n_dir, 1, npiece, PIECE_GROUPS, p, 2 * n)
    win = (e1 * b1 + e2 * b2).reshape(n_dir, SUB, npiece, PIECE_GROUPS * p, 2 * n)

    co_re, co_im = c_times_powers([[k + 1 for k in up], [SUB - k for k in up]])
    wout_src = jnp.stack([co_re, -co_im], axis=1).reshape(n_dir, 2, npiece, PIECE_GROUPS * n, SUB * p)

    steps = [SUB * m for m in range(1, SUBLANES + 1)]
    q_re, q_im = powers([steps, steps])
    q_re = q_re.reshape(n_dir, SUBLANES, g * n)
    q_im = q_im.reshape(n_dir, SUBLANES, g * n)
    row = jnp.arange(SUBLANES)
    consts = []
    for d in range(n_dir):
        kinds = []
        for dd in (1, 2, 4):
            keep = (row + dd <= SUBLANES - 1) if d == 1 else (row >= dd)
            m = keep.astype(F32)[:, None]
            kinds.append(jnp.stack([m * q_re[d, dd - 1][None], m * q_im[d, dd - 1][None]]))
        sel = (SUBLANES - 1 - row) if d == 1 else row
        kinds.append(jnp.stack([q_re[d][sel], q_im[d][sel]]))
        consts.append(jnp.stack(kinds))
    consts = jnp.stack(consts)
    return kcat, win, wout_src, consts


def _lru_kernel(*refs, bwd, nch, tc, width):
    if bwd:
        (x_ref, xp_ref, xn_ref, cw_ref, cb_ref, lam_ref, wa_ref, ba_ref, wx_ref, bx_ref,
         hf_ref, gd_ref, o_ref, pad_scr, a_scr, b_scr, carry_scr) = refs
    else:
        (x_ref, xp_ref, xn_ref, cw_ref, cb_ref, lam_ref, wa_ref, ba_ref, wx_ref, bx_ref,
         o_ref, pad_scr, a_scr, b_scr, carry_scr) = refs
    k = pl.program_id(1)
    chunk = jnp.where(k == 0, 0, nch - k) if bwd else k
    prev_ok = jnp.logical_and(chunk != 0, chunk != 1)
    next_ok = jnp.logical_and(chunk != 0, chunk != nch - 1)
    pad_scr[0:SUBLANES] = jnp.where(prev_ok, xp_ref[0, 0], 0.0)
    pad_scr[SUBLANES:SUBLANES + tc] = x_ref[0, 0]
    pad_scr[SUBLANES + tc:2 * SUBLANES + tc] = jnp.where(next_ok, xn_ref[0, 0], 0.0)
    xc = cb_ref[...]
    for tap in range(CONV_W):
        xc = xc + cw_ref[tap:tap + 1] * pad_scr[SUBLANES - 1 + tap:SUBLANES - 1 + tap + tc]

    bw = width // D_BLOCKS
    r_parts, i_parts = [], []
    for blk in range(D_BLOCKS):
        xb = xc[:, blk * bw:(blk + 1) * bw].astype(BF16)
        r_parts.append(jnp.dot(xb, wa_ref[blk], preferred_element_type=F32))
        i_parts.append(jnp.dot(xb, wx_ref[blk], preferred_element_type=F32))
    r = _sigmoid(jnp.concatenate(r_parts, axis=1) + ba_ref[...])
    ig = _sigmoid(jnp.concatenate(i_parts, axis=1) + bx_ref[...])
    nl = -lam_ref[...]
    e = jnp.exp(-jnp.abs(nl))
    e1 = 1.0 + e
    log1p_e = jnp.where(e1 == 1.0, e, jnp.log(e1) * (e / jnp.where(e1 == 1.0, 1.0, e1 - 1.0)))
    softplus = jnp.maximum(nl, 0.0) + log1p_e
    log_a = (-LRU_C * softplus) * r
    a = jnp.exp(log_a)
    a_scr[...] = a
    b_scr[...] = jnp.sqrt(-jnp.tanh(log_a) * (1.0 + a * a)) * (ig * xc)

    @pl.when(k == 0)
    def _():
        carry_scr[...] = jnp.zeros_like(carry_scr)

    rowi = lax.broadcasted_iota(jnp.int32, (SUBLANES, width), 0)
    n_tiles = tc // SUBLANES

    def body(n, carry):
        ti = n_tiles - 1 - n if bwd else n
        rr = pl.ds(pl.multiple_of(ti * SUBLANES, SUBLANES), SUBLANES)
        a, b = a_scr[rr], b_scr[rr]
        for dd in (1, 2, 4):
            keep = (rowi + dd <= SUBLANES - 1) if bwd else (rowi >= dd)
            sh = SUBLANES - dd if bwd else dd
            b = b + a * jnp.where(keep, pltpu.roll(b, sh, 0), 0.0)
            a = a * jnp.where(keep, pltpu.roll(a, sh, 0), 1.0)
        h = b + a * carry
        b_scr[rr] = h
        return h[0:1] if bwd else h[SUBLANES - 1:SUBLANES]

    carry_scr[...] = lax.fori_loop(0, n_tiles, body, carry_scr[...])
    if bwd:
        o_ref[0] = ((hf_ref[0] + b_scr[...]) * _silu(gd_ref[0, 0])).astype(BF16)
    else:
        o_ref[0] = b_scr[...]


def lru_pass(slabs, conv_w, conv_b, lam, wa, ba, wx, bx, ctx_len, hf=None):
    _, b, t, w = slabs.shape
    bwd = hf is not None
    tc = ctx_len
    nch = t // tc
    hb = tc // SUBLANES
    nhb = t // SUBLANES

    def ch(k):
        return jnp.where(k == 0, 0, nch - k) if bwd else k

    in_specs = [pl.BlockSpec((1, 1, tc, w), lambda bb, k: (1, bb, ch(k), 0)),
                pl.BlockSpec((1, 1, SUBLANES, w), lambda bb, k: (1, bb, jnp.maximum(ch(k) * hb - 1, 0), 0)),
                pl.BlockSpec((1, 1, SUBLANES, w), lambda bb, k: (1, bb, jnp.minimum((ch(k) + 1) * hb, nhb - 1), 0)),
                pl.BlockSpec((CONV_W, w), lambda bb, k: (0, 0)),
                pl.BlockSpec((1, w), lambda bb, k: (0, 0)),
                pl.BlockSpec((1, w), lambda bb, k: (0, 0)),
                pl.BlockSpec(wa.shape, lambda bb, k: (0, 0, 0)),
                pl.BlockSpec((1, w), lambda bb, k: (0, 0)),
                pl.BlockSpec(wx.shape, lambda bb, k: (0, 0, 0)),
                pl.BlockSpec((1, w), lambda bb, k: (0, 0))]
    args = [slabs, slabs, slabs, conv_w, conv_b, lam, wa, ba, wx, bx]
    if bwd:
        in_specs += [pl.BlockSpec((1, tc, w), lambda bb, k: (bb, ch(k), 0)),
                     pl.BlockSpec((1, 1, tc, w), lambda bb, k: (3, bb, ch(k), 0))]
        args += [hf, slabs]
    return pl.pallas_call(
        functools.partial(_lru_kernel, bwd=bwd, nch=nch, tc=tc, width=w),
        grid=(b, nch),
        in_specs=in_specs,
        out_specs=pl.BlockSpec((1, tc, w), lambda bb, k: (bb, ch(k), 0)),
        out_shape=jax.ShapeDtypeStruct((b, t, w), BF16 if bwd else F32),
        scratch_shapes=[pltpu.VMEM((tc + 2 * SUBLANES, w), F32),
                        pltpu.VMEM((tc, w), F32),
                        pltpu.VMEM((tc, w), F32),
                        pltpu.VMEM((1, w), F32)],
        compiler_params=_cparams("parallel", "arbitrary"),
        name="lru_bwd" if bwd else "lru_fwd",
    )(*args)


def rope_tables(ctx_len, seq):
    rows = seq // GRID_W
    axis_dim = HEAD_DIM // 2
    row = jnp.repeat(jnp.arange(rows), GRID_W).astype(F32)
    col = jnp.tile(jnp.arange(GRID_W), rows).astype(F32)
    inv = ROPE_THETA ** (-jnp.arange(0, axis_dim, 2, dtype=F32) / axis_dim)
    ang = jnp.concatenate([row[:, None] * inv, col[:, None] * inv], axis=-1)
    cos, sin = jnp.cos(ang), jnp.sin(ang)
    cos_t = jnp.repeat(cos, 2, axis=-1)
    sin_t = jnp.stack([-sin, sin], axis=-1).reshape(seq, HEAD_DIM)
    cos_t = jnp.concatenate([jnp.ones((ctx_len, HEAD_DIM), F32), cos_t], axis=0)
    sin_t = jnp.concatenate([jnp.zeros((ctx_len, HEAD_DIM), F32), sin_t], axis=0)
    return cos_t, sin_t


def even_layer(x_all, mod, g, w_in, w_out, j, q_g, k_g, sgu_g, ws, bs, cos_t, sin_t, ctx_len, latent_only):
    shift_l, scale_l, gate_l, shift_c, scale_c, gate_c = mod
    slabs = inproj(x_all, shift_l, scale_l, shift_c, scale_c, g, w_in, j, ctx_len, tn=512,
                   tm=_pick_tile(x_all.shape[1], 1152), out_dtype=BF16)
    q_gain = (q_g * (HEAD_DIM ** -0.5 * math.log2(math.e)))[None, :]
    mix_a = attention(slabs, q_gain, k_g[None, :], cos_t, sin_t, ctx_len)
    mix_b = sgu(slabs, ws.astype(BF16), bs[:, :, None], sgu_g[None, :])
    return outproj(mix_a, mix_b, w_out, j, x_all, gate_l, gate_c, ctx_len, latent_only)


def odd_layer(x_all, mod, g, w_in, w_out, j, s5_ops, d_skip, glu_w, glu_b, conv_w, conv_b, lam, wa, ba, wx, bx,
              ctx_len, latent_only):
    shift_l, scale_l, gate_l, shift_c, scale_c, gate_c = mod
    b, t, _ = x_all.shape
    slabs = inproj(x_all, shift_l, scale_l, shift_c, scale_c, g, w_in, j, ctx_len, tn=1024,
                   tm=_pick_tile(t, 1152), out_dtype=F32)
    cw = slabs.shape[3]
    kcat, win, wout_src, consts = s5_ops
    t_intra, w_s5in, w_s5out = s5_expand(kcat, win, wout_src, j, kcat.shape[-2], C_STATE)
    slabs2 = slabs.reshape(slabs.shape[0], b * t, cw)
    y = s5_core(slabs2, w_s5in, t_intra, w_s5out, consts, j, b, ctx_len)
    mix_c = s5_finish(y, slabs2, d_skip[None, :], glu_w.astype(BF16), glu_b[None, :],
                      tm=_pick_tile(b * t, 512))
    hf = lru_pass(slabs, conv_w, conv_b[None, :], lam[0][None, :], wa[0].astype(BF16), ba[0][None, :],
                  wx[0].astype(BF16), bx[0][None, :], ctx_len)
    mix_d = lru_pass(slabs, conv_w, conv_b[None, :], lam[1][None, :], wa[1].astype(BF16), ba[1][None, :],
                     wx[1].astype(BF16), bx[1][None, :], ctx_len, hf=hf)
    return outproj(mix_c.reshape(b, t, cw), mix_d, w_out, j, x_all, gate_l, gate_c, ctx_len, latent_only)


def kernel(x, c, ctx, c_ctx, ada_w, ada_b, norm_g, ev_w_in, ev_w_out, ev_q_g, ev_k_g, ev_sgu_g, ev_ws, ev_bs,
           od_w_in, od_w_out, s5_lam_re, s5_lam_im, s5_log_dt, s5_b_re, s5_b_im, s5_c_re, s5_c_im, s5_d,
           s5_glu_w, s5_glu_b, lru_conv_w, lru_conv_b, lru_lam, lru_wa, lru_ba, lru_wx, lru_bx):
    b, seq, d = x.shape
    ctx_len = ctx.shape[1]
    depth = ada_w.shape[0]
    assert b + 1 <= SUBLANES and seq % ctx_len == 0
    cos_t, sin_t = rope_tables(ctx_len, seq)
    cc = jnp.zeros((2 * SUBLANES, d), F32).at[:b].set(c).at[b].set(c_ctx)
    mods = adaln_all(cc, ada_w, ada_b)
    x_all = jnp.concatenate([ctx, x], axis=1)
    s5_ops = jax.vmap(s5_derive)(s5_lam_re, s5_lam_im, s5_log_dt, s5_b_re, s5_b_im, s5_c_re, s5_c_im)
    ev_w_in, ev_w_out = ev_w_in.astype(BF16), ev_w_out.astype(BF16)
    od_w_in, od_w_out = od_w_in.astype(BF16), od_w_out.astype(BF16)
    for layer in range(depth):
        m = mods[layer]
        mod = tuple(m[:b, None, k * d:(k + 1) * d] for k in range(3)) + \
            tuple(m[b:b + 1, k * d:(k + 1) * d] for k in range(3))
        last = layer == depth - 1
        j = layer // 2
        g = norm_g[layer][None, :]
        if layer % 2 == 0:
            x_all = even_layer(x_all, mod, g, ev_w_in, ev_w_out, j, ev_q_g[j], ev_k_g[j], ev_sgu_g[j],
                               ev_ws[j], ev_bs[j], cos_t, sin_t, ctx_len, last)
        else:
            x_all = odd_layer(x_all, mod, g, od_w_in, od_w_out, j, s5_ops,
                              s5_d[j], s5_glu_w[j], s5_glu_b[j],
                              lru_conv_w[j], lru_conv_b[j], lru_lam[j], lru_wa[j], lru_ba[j], lru_wx[j],
                              lru_bx[j], ctx_len, last)
    return x_all
```

```python
import functools
import math

import jax
import jax.numpy as jnp
from jax import lax
from jax.experimental import pallas as pl
from jax.experimental.pallas import tpu as pltpu

F32 = jnp.float32
BF16 = jnp.bfloat16

EPS = 1e-6
GRID_W = 64
HEAD_DIM = 128
A_Q_HEADS = 12
A_KV_HEADS = 4
A_GROUP = A_Q_HEADS // A_KV_HEADS
B_GROUPS = 4
CHUNK = 128
ROPE_THETA = 10000.0
C_GROUP_DIM = 16
C_STATE = 64
D_BLOCKS = 8
CONV_W = 4
LRU_C = 8.0

LANES = 128
SUBLANES = 8
SUB = 8
PIECE_GROUPS = LANES // C_GROUP_DIM
KEY_CHUNKS = 3
VMEM_LIMIT = 50 * 1024 * 1024


def _cparams(*sem):
    return pltpu.CompilerParams(dimension_semantics=sem, vmem_limit_bytes=VMEM_LIMIT)


def _sigmoid(x):
    return 0.5 * jnp.tanh(0.5 * x) + 0.5


def _silu(x):
    return x * _sigmoid(x)


def _pick_tile(rows, target, align=16):
    best = None
    for cand in range(align, min(rows, target) + 1, align):
        if rows % cand == 0:
            best = cand
    assert best is not None, (rows, target)
    return best


def _split_bf16(x, parts):
    out = []
    for _ in range(parts):
        hi = x.astype(BF16)
        out.append(hi)
        x = x - hi.astype(F32)
    return out


def _adaln_kernel(cc_ref, w_ref, b_ref, o_ref):
    s_parts = _split_bf16(_silu(cc_ref[...]), 3)
    w = w_ref[0].astype(BF16)
    acc = jnp.dot(s_parts[2], w, preferred_element_type=F32)
    acc = acc + jnp.dot(s_parts[1], w, preferred_element_type=F32)
    acc = acc + jnp.dot(s_parts[0], w, preferred_element_type=F32)
    o_ref[0] = acc + b_ref[0]


def adaln_all(cc, ada_w, ada_b):
    depth, d, n3 = ada_w.shape
    rows = cc.shape[0]
    tn = _pick_tile(n3, 1536, align=LANES)
    return pl.pallas_call(
        _adaln_kernel,
        grid=(depth, n3 // tn),
        in_specs=[pl.BlockSpec((rows, d), lambda l, j: (0, 0)),
                  pl.BlockSpec((1, d, tn), lambda l, j: (l, 0, j)),
                  pl.BlockSpec((1, 1, tn), lambda l, j: (l, 0, j))],
        out_specs=pl.BlockSpec((1, rows, tn), lambda l, j: (l, 0, j)),
        out_shape=jax.ShapeDtypeStruct((depth, rows, n3), F32),
        compiler_params=_cparams("parallel", "parallel"),
        name="adaln",
    )(cc, ada_w, ada_b.reshape(depth, 1, n3))


def _inproj_kernel(x_ref, shl_ref, scl_ref, shc_ref, scc_ref, g_ref, w_ref, o_ref, h_scr, *, tm, ctx_len):
    i = pl.program_id(1)
    first = pl.program_id(2) == 0

    def modulate(rows, sc_ref, sh_ref):
        x = x_ref[0, rows]
        r = lax.rsqrt(jnp.mean(x * x, axis=-1, keepdims=True) + EPS)
        h_scr[rows] = ((x * r) * (g_ref[...] * (1.0 + sc_ref[...])) + sh_ref[...]).astype(BF16)

    scl, shl = scl_ref.at[0], shl_ref.at[0]
    if tm >= ctx_len:
        @pl.when(jnp.logical_and(first, i == 0))
        def _():
            modulate(slice(0, ctx_len), scc_ref, shc_ref)
            if tm > ctx_len:
                modulate(slice(ctx_len, tm), scl, shl)

        @pl.when(jnp.logical_and(first, i > 0))
        def _():
            modulate(slice(0, tm), scl, shl)
    else:
        @pl.when(jnp.logical_and(first, i < ctx_len // tm))
        def _():
            modulate(slice(0, tm), scc_ref, shc_ref)

        @pl.when(jnp.logical_and(first, i >= ctx_len // tm))
        def _():
            modulate(slice(0, tm), scl, shl)

    o_ref[0, 0] = jnp.dot(h_scr[...], w_ref[...], preferred_element_type=F32).astype(o_ref.dtype)


def inproj(x_all, shift_l, scale_l, shift_c, scale_c, g, w, layer, ctx_len, tn, tm, out_dtype):
    b, t, d = x_all.shape
    n = w.shape[2]
    n_slab = n // tn
    assert tm >= ctx_len or ctx_len % tm == 0
    return pl.pallas_call(
        functools.partial(_inproj_kernel, tm=tm, ctx_len=ctx_len),
        grid=(b, t // tm, n_slab),
        in_specs=[pl.BlockSpec((1, tm, d), lambda bb, i, j: (bb, i, 0)),
                  pl.BlockSpec((1, 1, d), lambda bb, i, j: (bb, 0, 0)),
                  pl.BlockSpec((1, 1, d), lambda bb, i, j: (bb, 0, 0)),
                  pl.BlockSpec((1, d), lambda bb, i, j: (0, 0)),
                  pl.BlockSpec((1, d), lambda bb, i, j: (0, 0)),
                  pl.BlockSpec((1, d), lambda bb, i, j: (0, 0)),
                  pl.BlockSpec((None, d, tn), lambda bb, i, j: (layer, 0, j))],
        out_specs=pl.BlockSpec((1, 1, tm, tn), lambda bb, i, j: (j, bb, i, 0)),
        out_shape=jax.ShapeDtypeStruct((n_slab, b, t, tn), out_dtype),
        scratch_shapes=[pltpu.VMEM((tm, d), BF16)],
        compiler_params=_cparams("parallel", "parallel", "arbitrary"),
        name="inproj",
    )(x_all, shift_l, scale_l, shift_c, scale_c, g, w)


def _outproj_kernel(ma_ref, mb_ref, w1_ref, w2_ref, x_ref, gl_ref, gc_ref, o_ref, *, ctx_tiles):
    i = pl.program_id(1)
    acc = jnp.dot(ma_ref[0], w1_ref[...], preferred_element_type=F32)
    acc = acc + jnp.dot(mb_ref[0], w2_ref[...], preferred_element_type=F32)
    gate = jnp.where(i < ctx_tiles, gc_ref[...], gl_ref[0])
    o_ref[0] = x_ref[0] + gate * acc


def outproj(ma, mb, w, layer, x_all, gate_l, gate_c, ctx_len, latent_only):
    b, t, d = x_all.shape
    k1, k2 = ma.shape[2], mb.shape[2]
    assert w.shape[1] == k1 + k2 and k1 % k2 == 0
    tm = ctx_len
    off = 1 if latent_only else 0
    t_out = t - ctx_len if latent_only else t
    return pl.pallas_call(
        functools.partial(_outproj_kernel, ctx_tiles=0 if latent_only else 1),
        grid=(b, t_out // tm),
        in_specs=[pl.BlockSpec((1, tm, k1), lambda bb, i: (bb, i + off, 0)),
                  pl.BlockSpec((1, tm, k2), lambda bb, i: (bb, i + off, 0)),
                  pl.BlockSpec((None, k1, d), lambda bb, i: (layer, 0, 0)),
                  pl.BlockSpec((None, k2, d), lambda bb, i: (layer, k1 // k2, 0)),
                  pl.BlockSpec((1, tm, d), lambda bb, i: (bb, i + off, 0)),
                  pl.BlockSpec((1, 1, d), lambda bb, i: (bb, 0, 0)),
                  pl.BlockSpec((1, d), lambda bb, i: (0, 0))],
        out_specs=pl.BlockSpec((1, tm, d), lambda bb, i: (bb, i, 0)),
        out_shape=jax.ShapeDtypeStruct((b, t_out, d), F32),
        compiler_params=_cparams("parallel", "parallel"),
        name="outproj",
    )(ma, mb, w, w, x_all, gate_l, gate_c)


def _norm_rope(x, gain, cos, sin):
    y = x * lax.rsqrt(jnp.mean(x * x, axis=-1, keepdims=True) + EPS) * gain
    lane = lax.broadcasted_iota(jnp.int32, y.shape, 1)
    partner = jnp.where((lane & 1) == 0, pltpu.roll(y, HEAD_DIM - 1, 1), pltpu.roll(y, 1, 1))
    return y * cos + partner * sin


def _attn_kernel(q0_ref, q1_ref, q2_ref, k_ref, v_ref, ga0_ref, ga1_ref, ga2_ref, qg_ref, kg_ref,
                 cos_ref, sin_ref, o_ref, k_scr, vaug_scr, *, ctx_len, tq):
    iq = pl.program_id(2)

    @pl.when(iq == 0)
    def _():
        k_scr[...] = _norm_rope(k_ref[0, 0].astype(F32), kg_ref[...], cos_ref[...], sin_ref[...]).astype(BF16)
        vaug_scr[:, :HEAD_DIM] = v_ref[0, 0].astype(BF16)
        vaug_scr[:, HEAD_DIM:] = jnp.ones((vaug_scr.shape[0], HEAD_DIM), BF16)

    def attend(n_keys):
        rows = pl.ds(pl.multiple_of(iq * tq, tq), tq)
        cos_q, sin_q = cos_ref[rows, :], sin_ref[rows, :]
        kc = n_keys // KEY_CHUNKS if n_keys % (KEY_CHUNKS * 2 * HEAD_DIM) == 0 else n_keys
        for g, (q_ref, ga_ref) in enumerate(((q0_ref, ga0_ref), (q1_ref, ga1_ref), (q2_ref, ga2_ref))):
            q = _norm_rope(q_ref[0, 0].astype(F32), qg_ref[...], cos_q, sin_q).astype(BF16)
            parts = []
            for c0 in range(0, n_keys, kc):
                s = lax.dot_general(q, k_scr[c0:c0 + kc], (((1,), (1,)), ((), ())), preferred_element_type=F32)
                m = jnp.max(s, axis=-1, keepdims=True)
                p = jnp.exp2(s - m).astype(BF16)
                parts.append((m, jnp.dot(p, vaug_scr[c0:c0 + kc], preferred_element_type=F32)))
            m_all = functools.reduce(jnp.maximum, [m for m, _ in parts])
            pv = functools.reduce(lambda a, b: a + b, [acc * jnp.exp2(m - m_all) for m, acc in parts])
            o = pv[:, :HEAD_DIM] / pv[:, HEAD_DIM:]
            o_ref[0, :, g * HEAD_DIM:(g + 1) * HEAD_DIM] = (o * _silu(ga_ref[0, 0].astype(F32))).astype(BF16)

    @pl.when(iq == 0)
    def _():
        attend(ctx_len)

    @pl.when(iq > 0)
    def _():
        attend(k_scr.shape[0])


def attention(slabs, q_gain, k_gain, cos_t, sin_t, ctx_len):
    _, b, t, tn = slabs.shape
    per = tn // HEAD_DIM
    tq = ctx_len
    gw = A_GROUP * HEAD_DIM
    assert A_GROUP == 3

    def head_spec(first_slab, g):
        def head_map(bb, kv, iq):
            h = kv * A_GROUP + g
            return (first_slab + h // per, bb, iq, h % per)
        return pl.BlockSpec((1, 1, tq, HEAD_DIM), head_map)

    vec = pl.BlockSpec((1, HEAD_DIM), lambda bb, kv, iq: (0, 0))
    table = pl.BlockSpec((t, HEAD_DIM), lambda bb, kv, iq: (0, 0))
    return pl.pallas_call(
        functools.partial(_attn_kernel, ctx_len=ctx_len, tq=tq),
        grid=(b, A_KV_HEADS, t // tq),
        in_specs=[head_spec(2, 0), head_spec(2, 1), head_spec(2, 2),
                  pl.BlockSpec((1, 1, t, HEAD_DIM), lambda bb, kv, iq: (0, bb, 0, kv)),
                  pl.BlockSpec((1, 1, t, HEAD_DIM), lambda bb, kv, iq: (1, bb, 0, kv)),
                  head_spec(7, 0), head_spec(7, 1), head_spec(7, 2),
                  vec, vec, table, table],
        out_specs=pl.BlockSpec((1, tq, gw), lambda bb, kv, iq: (bb, iq, kv)),
        out_shape=jax.ShapeDtypeStruct((b, t, A_Q_HEADS * HEAD_DIM), BF16),
        scratch_shapes=[pltpu.VMEM((t, HEAD_DIM), BF16), pltpu.VMEM((t, 2 * HEAD_DIM), BF16)],
        compiler_params=_cparams("parallel", "parallel", "arbitrary"),
        name="attention",
    )(slabs, slabs, slabs, slabs, slabs, slabs, slabs, slabs, q_gain, k_gain, cos_t, sin_t)


def _sgu_kernel(u_ref, v_ref, gb_ref, ws_ref, bs_ref, g_ref, o_ref, *, n_chunks):
    def chunk(c, carry):
        rows = pl.ds(pl.multiple_of(c * CHUNK, CHUNK), CHUNK)
        outs = []
        for grp in range(B_GROUPS):
            cols = slice(grp * LANES, (grp + 1) * LANES)
            v = v_ref[0, 0, rows, cols].astype(F32)
            vn = v * lax.rsqrt(jnp.mean(v * v, axis=-1, keepdims=True) + EPS) * g_ref[:, cols]
            mixed = jnp.dot(ws_ref[grp], vn.astype(BF16), preferred_element_type=F32) + bs_ref[grp]
            outs.append(u_ref[0, 0, rows, cols].astype(F32) * mixed
                        * _silu(gb_ref[0, 0, rows, cols].astype(F32)))
        o_ref[0, rows, :] = jnp.concatenate(outs, axis=1).astype(BF16)
        return carry

    lax.fori_loop(0, n_chunks, chunk, 0)


def sgu(slabs, ws, bs, g):
    _, b, t, tn = slabs.shape
    spec = lambda s: pl.BlockSpec((1, 1, t, tn), lambda bb: (s, bb, 0, 0))
    return pl.pallas_call(
        functools.partial(_sgu_kernel, n_chunks=t // CHUNK),
        grid=(b,),
        in_specs=[spec(5), spec(6), spec(10),
                  pl.BlockSpec((B_GROUPS, CHUNK, CHUNK), lambda bb: (0, 0, 0)),
                  pl.BlockSpec((B_GROUPS, CHUNK, 1), lambda bb: (0, 0, 0)),
                  pl.BlockSpec((1, tn), lambda bb: (0, 0))],
        out_specs=pl.BlockSpec((1, t, tn), lambda bb: (bb, 0, 0)),
        out_shape=jax.ShapeDtypeStruct((b, t, tn), BF16),
        compiler_params=_cparams("parallel"),
        name="sgu",
    )(slabs, slabs, slabs, ws, bs, g)


def _s5_fold_rows(u_ref, tm):
    return jnp.concatenate([u_ref[pl.ds(t, tm, stride=SUB), :] for t in range(SUB)], axis=1)


def _s5_scan(g_ref, c_ref, o_ref, *, ctx_pairs, all_pairs, half):
    rows = 2 * SUBLANES
    rowi = lax.broadcasted_iota(jnp.int32, (SUBLANES, half), 0)

    def tile(d, g_re, g_im, c_re, c_im):
        bwd = d == 1
        b_re, b_im = g_re, g_im
        for idx, dd in enumerate((1, 2, 4)):
            a_re, a_im = c_ref[d, idx, 0], c_ref[d, idx, 1]
            sh = SUBLANES - dd if bwd else dd
            s_re, s_im = pltpu.roll(b_re, sh, 0), pltpu.roll(b_im, sh, 0)
            b_re, b_im = (b_re + a_re * s_re - a_im * s_im,
                          b_im + a_re * s_im + a_im * s_re)
        a_re, a_im = c_ref[d, 3, 0], c_ref[d, 3, 1]
        h_re = b_re + a_re * c_re - a_im * c_im
        h_im = b_im + a_re * c_im + a_im * c_re
        if bwd:
            p_re = jnp.where(rowi == SUBLANES - 1, c_re, pltpu.roll(h_re, SUBLANES - 1, 0))
            p_im = jnp.where(rowi == SUBLANES - 1, c_im, pltpu.roll(h_im, SUBLANES - 1, 0))
            return p_re, p_im, h_re[0:1], h_im[0:1]
        p_re = jnp.where(rowi == 0, c_re, pltpu.roll(h_re, 1, 0))
        p_im = jnp.where(rowi == 0, c_im, pltpu.roll(h_im, 1, 0))
        return p_re, p_im, h_re[SUBLANES - 1:SUBLANES], h_im[SUBLANES - 1:SUBLANES]

    def pair(d, pi, carry):
        c_re, c_im = carry
        r0 = pl.multiple_of(pi * rows, rows)
        base = d * 2 * half
        order = (1, 0) if d == 1 else (0, 1)
        res = [None, None]
        for which in order:
            rr = pl.ds(r0 + which * SUBLANES, SUBLANES)
            g_re = g_ref[rr, pl.ds(base, half)]
            g_im = g_ref[rr, pl.ds(base + half, half)]
            p_re, p_im, c_re, c_im = tile(d, g_re, g_im, c_re, c_im)
            res[which] = (p_re, p_im)
        o_ref[pl.ds(r0, rows), pl.ds(base, half)] = jnp.concatenate(
            [res[0][0], res[1][0]], axis=0).astype(BF16)
        o_ref[pl.ds(r0, rows), pl.ds(base + half, half)] = jnp.concatenate(
            [res[0][1], res[1][1]], axis=0).astype(BF16)
        return c_re, c_im

    zero = (jnp.zeros((1, half), F32), jnp.zeros((1, half), F32))
    lax.fori_loop(0, all_pairs, lambda pi, c: pair(0, pi, c), zero)
    c = lax.fori_loop(0, ctx_pairs, lambda n, c: pair(1, ctx_pairs - 1 - n, c), zero)
    lax.fori_loop(0, all_pairs - ctx_pairs, lambda n, c: pair(1, all_pairs - 1 - n, c), c)


def _s5_core_kernel(u_ref, win_ref, t_ref, wout_ref, c_ref, y_ref, g_scr, hp_scr, *, tm, ctx_pairs, half):
    x = _s5_fold_rows(u_ref, tm).astype(BF16)
    g_scr[...] = jnp.dot(x, win_ref[0], preferred_element_type=F32)
    _s5_scan(g_scr, c_ref, hp_scr, ctx_pairs=ctx_pairs, all_pairs=tm // (2 * SUBLANES), half=half)
    y = jnp.dot(x, t_ref[0], preferred_element_type=F32)
    y = y + jnp.dot(hp_scr[...], wout_ref[0], preferred_element_type=F32)
    for t in range(SUB):
        y_ref[pl.ds(t, tm, stride=SUB), :] = y[:, t * LANES:(t + 1) * LANES]


def s5_core(slabs2, w_in, t_intra, w_out, consts, layer, batch, ctx_len):
    _, r, width = slabs2.shape
    t = r // batch
    tm = t // SUB
    n_piece = t_intra.shape[0]
    half = PIECE_GROUPS * C_STATE
    assert w_in.shape[2] == 4 * half and tm % (2 * SUBLANES) == 0 and ctx_len % (2 * SUBLANES * SUB) == 0
    return pl.pallas_call(
        functools.partial(_s5_core_kernel, tm=tm, ctx_pairs=ctx_len // SUB // (2 * SUBLANES), half=half),
        grid=(n_piece, batch),
        in_specs=[pl.BlockSpec((None, t, LANES), lambda i, bb: (0, bb, i)),
                  pl.BlockSpec((1,) + w_in.shape[1:], lambda i, bb: (i, 0, 0)),
                  pl.BlockSpec((1,) + t_intra.shape[1:], lambda i, bb: (i, 0, 0)),
                  pl.BlockSpec((1,) + w_out.shape[1:], lambda i, bb: (i, 0, 0)),
                  pl.BlockSpec((None, 2, 4, 2, SUBLANES, half), lambda i, bb: (layer, 0, 0, 0, 0, i))],
        out_specs=pl.BlockSpec((t, LANES), lambda i, bb: (bb, i)),
        out_shape=jax.ShapeDtypeStruct((r, width), F32),
        scratch_shapes=[pltpu.VMEM((tm, 4 * half), F32), pltpu.VMEM((tm, 4 * half), BF16)],
        compiler_params=_cparams("parallel", "parallel"),
        name="s5_core",
    )(slabs2, w_in, t_intra, w_out, consts)


def _s5_finish_kernel(y_ref, u_ref, gc_ref, d_ref, w_ref, b_ref, o_ref):
    y = y_ref[...] + d_ref[...] * u_ref[0]
    y = jax.nn.gelu(y)
    z = jnp.dot(y.astype(BF16), w_ref[...], preferred_element_type=F32) + b_ref[...]
    o_ref[...] = (y * _sigmoid(z) * _silu(gc_ref[0])).astype(BF16)


def s5_finish(y, slabs2, d_skip, glu_w, glu_b, tm):
    r, w = y.shape
    return pl.pallas_call(
        _s5_finish_kernel,
        grid=(r // tm,),
        in_specs=[pl.BlockSpec((tm, w), lambda i: (i, 0)),
                  pl.BlockSpec((1, tm, w), lambda i: (0, i, 0)),
                  pl.BlockSpec((1, tm, w), lambda i: (2, i, 0)),
                  pl.BlockSpec((1, w), lambda i: (0, 0)),
                  pl.BlockSpec((w, w), lambda i: (0, 0)),
                  pl.BlockSpec((1, w), lambda i: (0, 0))],
        out_specs=pl.BlockSpec((tm, w), lambda i: (i, 0)),
        out_shape=jax.ShapeDtypeStruct((r, w), BF16),
        compiler_params=_cparams("parallel"),
        name="s5_finish",
    )(y, slabs2, slabs2, d_skip, glu_w, glu_b)


def _expand_block_diag(src, unit, row_unit):
    rows, cols = src.shape
    wide = cols * PIECE_GROUPS
    lg = lambda v: int(math.log2(v))
    k = lax.broadcasted_iota(jnp.int32, (cols, wide), 0)
    j = lax.broadcasted_iota(jnp.int32, (cols, wide), 1)
    src_col = ((j >> lg(PIECE_GROUPS * unit)) << lg(unit)) + (j & (unit - 1))
    spread = jnp.where(k == src_col, 1.0, 0.0).astype(BF16)
    out = jnp.dot(src.astype(BF16), spread, preferred_element_type=F32)
    rg = (lax.broadcasted_iota(jnp.int32, (rows, wide), 0) >> lg(row_unit)) & (PIECE_GROUPS - 1)
    ch = (lax.broadcasted_iota(jnp.int32, (rows, wide), 1) >> lg(unit)) & (PIECE_GROUPS - 1)
    return jnp.where(rg == ch, out, 0.0).astype(BF16)


def _s5_expand_kernel(kc_ref, wi_ref, wo_ref, t_ref, win_ref, wout_ref, tsrc_scr, *, p, n):
    lane = lax.broadcasted_iota(jnp.int32, (p, LANES), 1)
    for gl in range(PIECE_GROUPS):
        kf = kc_ref[0, 0, gl]
        kb = kc_ref[1, 0, gl]
        for s in range(SUB):
            shift_b = (LANES - p * (SUB - 1 - s)) % LANES
            f = kf if s == 0 else pltpu.roll(kf, p * s, 1)
            r = kb if shift_b == 0 else pltpu.roll(kb, shift_b, 1)
            blk = jnp.where(lane >= p * s, f, 0.0) + jnp.where(lane < p * (s + 1), r, 0.0)
            tsrc_scr[pl.ds((s * PIECE_GROUPS + gl) * p, p), :] = blk
    t_ref[0] = _expand_block_diag(tsrc_scr[...], p, p)
    rows_t = tsrc_scr.shape[0]
    half = win_ref.shape[2] // 2
    win_ref[0, :, :half] = _expand_block_diag(wi_ref[0].reshape(rows_t, LANES), n, p)
    win_ref[0, :, half:] = _expand_block_diag(wi_ref[1].reshape(rows_t, LANES), n, p)
    wout_ref[0] = _expand_block_diag(wo_ref[...].reshape(wout_ref.shape[1], LANES), p, n)


def s5_expand(kcat, win, wout_src, layer, p, n):
    npiece = kcat.shape[2]
    rows_t = SUB * PIECE_GROUPS * p
    rows_o = 2 * 2 * PIECE_GROUPS * n
    return pl.pallas_call(
        functools.partial(_s5_expand_kernel, p=p, n=n),
        grid=(npiece,),
        in_specs=[pl.BlockSpec((None, 2, 1, PIECE_GROUPS, p, LANES), lambda i: (layer, 0, i, 0, 0, 0)),
                  pl.BlockSpec((None, 2, SUB, 1, PIECE_GROUPS * p, LANES), lambda i: (layer, 0, 0, i, 0, 0)),
                  pl.BlockSpec((None, 2, 2, 1, PIECE_GROUPS * n, LANES), lambda i: (layer, 0, 0, i, 0, 0))],
        out_specs=[pl.BlockSpec((1, rows_t, LANES * PIECE_GROUPS), lambda i: (i, 0, 0)),
                   pl.BlockSpec((1, rows_t, 2 * LANES * PIECE_GROUPS), lambda i: (i, 0, 0)),
                   pl.BlockSpec((1, rows_o, LANES * PIECE_GROUPS), lambda i: (i, 0, 0))],
        out_shape=[jax.ShapeDtypeStruct((npiece, rows_t, LANES * PIECE_GROUPS), BF16),
                   jax.ShapeDtypeStruct((npiece, rows_t, 2 * LANES * PIECE_GROUPS), BF16),
                   jax.ShapeDtypeStruct((npiece, rows_o, LANES * PIECE_GROUPS), BF16)],
        scratch_shapes=[pltpu.VMEM((rows_t, LANES), F32)],
        compiler_params=_cparams("parallel"),
        name="s5_expand",
    )(kcat, win, wout_src)


def s5_derive(lam_re, lam_im, log_dt, b_re, b_im, c_re, c_im):
    hi = lax.Precision.HIGHEST
    n_dir, g, n = lam_re.shape
    p = b_re.shape[-1]
    npiece = g // PIECE_GROUPS
    assert n_dir == 2 and SUB * p == LANES
    dt = jnp.exp(log_dt)[:, :, None]
    up = list(range(SUB))
    down = up[::-1]

    def powers(exps):
        k = jnp.asarray(exps, F32)[:, :, None, None]
        mag = jnp.exp(k * (lam_re * dt)[:, None])
        ang = k * (lam_im * dt)[:, None]
        return mag * jnp.cos(ang), mag * jnp.sin(ang)

    a_re, a_im = powers([[1], [1]])
    den = lam_re * lam_re + lam_im * lam_im
    nr, ni = a_re[:, 0] - 1.0, a_im[:, 0]
    cf_re = ((nr * lam_re + ni * lam_im) / den)[:, :, None, :]
    cf_im = ((ni * lam_re - nr * lam_im) / den)[:, :, None, :]
    bt_re, bt_im = jnp.swapaxes(b_re, 2, 3), jnp.swapaxes(b_im, 2, 3)
    bb_re = cf_re * bt_re - cf_im * bt_im
    bb_im = cf_re * bt_im + cf_im * bt_re
    ct_re, ct_im = jnp.swapaxes(c_re, 2, 3), jnp.swapaxes(c_im, 2, 3)

    lane = jnp.arange(SUB * p)
    spread_p = (lane[None, :] % p == jnp.arange(p)[:, None]).astype(F32)
    cw_re = jnp.dot(ct_re, spread_p, precision=hi)
    cw_im = jnp.dot(ct_im, spread_p, precision=hi)

    def c_times_powers(exps):
        k = jnp.repeat(jnp.asarray(exps, F32), p, axis=1)[:, None, None, :]
        mag = jnp.exp(k * (lam_re * dt)[..., None])
        ang = k * (lam_im * dt)[..., None]
        e_re, e_im = mag * jnp.cos(ang), mag * jnp.sin(ang)
        return cw_re * e_re - cw_im * e_im, cw_re * e_im + cw_im * e_re

    ck_re, ck_im = c_times_powers([up, down])
    lhs = jnp.concatenate([bb_re, -bb_im], axis=-1)
    rhs = jnp.concatenate([ck_re, ck_im], axis=2)
    kcat = jnp.einsum('dgqk,dgkm->dgqm', lhs, rhs, precision=hi)
    kcat = kcat.reshape(n_dir, npiece, PIECE_GROUPS, p, SUB * p)

    e_re, e_im = powers([down, up])
    e1 = jnp.concatenate([e_re, e_re], axis=-1).reshape(n_dir, SUB, npiece, PIECE_GROUPS, 1, 2 * n)
    e2 = jnp.concatenate([-e_im, e_im], axis=-1).reshape(n_dir, SUB, npiece, PIECE_GROUPS, 1, 2 * n)
    b1 = jnp.concatenate([bb_re, bb_im], axis=-1).reshape(n_dir, 1, npiece, PIECE_GROUPS, p, 2 * n)
    b2 = jnp.concatenate([bb_im, bb_re], axis=-1).reshape(n_dir, 1, npiece, PIECE_GROUPS, p, 2 * n)
    win = (e1 * b1 + e2 * b2).reshape(n_dir, SUB, npiece, PIECE_GROUPS * p, 2 * n)

    co_re, co_im = c_times_powers([[k + 1 for k in up], [SUB - k for k in up]])
    wout_src = jnp.stack([co_re, -co_im], axis=1).reshape(n_dir, 2, npiece, PIECE_GROUPS * n, SUB * p)

    steps = [SUB * m for m in range(1, SUBLANES + 1)]
    q_re, q_im = powers([steps, steps])
    q_re = q_re.reshape(n_dir, SUBLANES, g * n)
    q_im = q_im.reshape(n_dir, SUBLANES, g * n)
    row = jnp.arange(SUBLANES)
    consts = []
    for d in range(n_dir):
        kinds = []
        for dd in (1, 2, 4):
            keep = (row + dd <= SUBLANES - 1) if d == 1 else (row >= dd)
            m = keep.astype(F32)[:, None]
            kinds.append(jnp.stack([m * q_re[d, dd - 1][None], m * q_im[d, dd - 1][None]]))
        sel = (SUBLANES - 1 - row) if d == 1 else row
        kinds.append(jnp.stack([q_re[d][sel], q_im[d][sel]]))
        consts.append(jnp.stack(kinds))
    consts = jnp.stack(consts)
    return kcat, win, wout_src, consts


def _lru_kernel(*refs, bwd, nch, tc, width):
    if bwd:
        (x_ref, xp_ref, xn_ref, cw_ref, cb_ref, lam_ref, wa_ref, ba_ref, wx_ref, bx_ref,
         hf_ref, gd_ref, o_ref, pad_scr, a_scr, b_scr, carry_scr) = refs
    else:
        (x_ref, xp_ref, xn_ref, cw_ref, cb_ref, lam_ref, wa_ref, ba_ref, wx_ref, bx_ref,
         o_ref, pad_scr, a_scr, b_scr, carry_scr) = refs
    k = pl.program_id(1)
    chunk = jnp.where(k == 0, 0, nch - k) if bwd else k
    prev_ok = jnp.logical_and(chunk != 0, chunk != 1)
    next_ok = jnp.logical_and(chunk != 0, chunk != nch - 1)
    pad_scr[0:SUBLANES] = jnp.where(prev_ok, xp_ref[0, 0], 0.0)
    pad_scr[SUBLANES:SUBLANES + tc] = x_ref[0, 0]
    pad_scr[SUBLANES + tc:2 * SUBLANES + tc] = jnp.where(next_ok, xn_ref[0, 0], 0.0)
    xc = cb_ref[...]
    for tap in range(CONV_W):
        xc = xc + cw_ref[tap:tap + 1] * pad_scr[SUBLANES - 1 + tap:SUBLANES - 1 + tap + tc]

    bw = width // D_BLOCKS
    r_parts, i_parts = [], []
    for blk in range(D_BLOCKS):
        xb = xc[:, blk * bw:(blk + 1) * bw].astype(BF16)
        r_parts.append(jnp.dot(xb, wa_ref[blk], preferred_element_type=F32))
        i_parts.append(jnp.dot(xb, wx_ref[blk], preferred_element_type=F32))
    r = _sigmoid(jnp.concatenate(r_parts, axis=1) + ba_ref[...])
    ig = _sigmoid(jnp.concatenate(i_parts, axis=1) + bx_ref[...])
    nl = -lam_ref[...]
    e = jnp.exp(-jnp.abs(nl))
    e1 = 1.0 + e
    log1p_e = jnp.where(e1 == 1.0, e, jnp.log(e1) * (e / jnp.where(e1 == 1.0, 1.0, e1 - 1.0)))
    softplus = jnp.maximum(nl, 0.0) + log1p_e
    log_a = (-LRU_C * softplus) * r
    a = jnp.exp(log_a)
    a_scr[...] = a
    b_scr[...] = jnp.sqrt(-jnp.tanh(log_a) * (1.0 + a * a)) * (ig * xc)

    @pl.when(k == 0)
    def _():
        carry_scr[...] = jnp.zeros_like(carry_scr)

    rowi = lax.broadcasted_iota(jnp.int32, (SUBLANES, width), 0)
    n_tiles = tc // SUBLANES

    def body(n, carry):
        ti = n_tiles - 1 - n if bwd else n
        rr = pl.ds(pl.multiple_of(ti * SUBLANES, SUBLANES), SUBLANES)
        a, b = a_scr[rr], b_scr[rr]
        for dd in (1, 2, 4):
            keep = (rowi + dd <= SUBLANES - 1) if bwd else (rowi >= dd)
            sh = SUBLANES - dd if bwd else dd
            b = b + a * jnp.where(keep, pltpu.roll(b, sh, 0), 0.0)
            a = a * jnp.where(keep, pltpu.roll(a, sh, 0), 1.0)
        h = b + a * carry
        b_scr[rr] = h
        return h[0:1] if bwd else h[SUBLANES - 1:SUBLANES]

    carry_scr[...] = lax.fori_loop(0, n_tiles, body, carry_scr[...])
    if bwd:
        o_ref[0] = ((hf_ref[0] + b_scr[...]) * _silu(gd_ref[0, 0])).astype(BF16)
    else:
        o_ref[0] = b_scr[...]


def lru_pass(slabs, conv_w, conv_b, lam, wa, ba, wx, bx, ctx_len, hf=None):
    _, b, t, w = slabs.shape
    bwd = hf is not None
    tc = ctx_len
    nch = t // tc
    hb = tc // SUBLANES
    nhb = t // SUBLANES

    def ch(k):
        return jnp.where(k == 0, 0, nch - k) if bwd else k

    in_specs = [pl.BlockSpec((1, 1, tc, w), lambda bb, k: (1, bb, ch(k), 0)),
                pl.BlockSpec((1, 1, SUBLANES, w), lambda bb, k: (1, bb, jnp.maximum(ch(k) * hb - 1, 0), 0)),
                pl.BlockSpec((1, 1, SUBLANES, w), lambda bb, k: (1, bb, jnp.minimum((ch(k) + 1) * hb, nhb - 1), 0)),
                pl.BlockSpec((CONV_W, w), lambda bb, k: (0, 0)),
                pl.BlockSpec((1, w), lambda bb, k: (0, 0)),
                pl.BlockSpec((1, w), lambda bb, k: (0, 0)),
                pl.BlockSpec(wa.shape, lambda bb, k: (0, 0, 0)),
                pl.BlockSpec((1, w), lambda bb, k: (0, 0)),
                pl.BlockSpec(wx.shape, lambda bb, k: (0, 0, 0)),
                pl.BlockSpec((1, w), lambda bb, k: (0, 0))]
    args = [slabs, slabs, slabs, conv_w, conv_b, lam, wa, ba, wx, bx]
    if bwd:
        in_specs += [pl.BlockSpec((1, tc, w), lambda bb, k: (bb, ch(k), 0)),
                     pl.BlockSpec((1, 1, tc, w), lambda bb, k: (3, bb, ch(k), 0))]
        args += [hf, slabs]
    return pl.pallas_call(
        functools.partial(_lru_kernel, bwd=bwd, nch=nch, tc=tc, width=w),
        grid=(b, nch),
        in_specs=in_specs,
        out_specs=pl.BlockSpec((1, tc, w), lambda bb, k: (bb, ch(k), 0)),
        out_shape=jax.ShapeDtypeStruct((b, t, w), BF16 if bwd else F32),
        scratch_shapes=[pltpu.VMEM((tc + 2 * SUBLANES, w), F32),
                        pltpu.VMEM((tc, w), F32),
                        pltpu.VMEM((tc, w), F32),
                        pltpu.VMEM((1, w), F32)],
        compiler_params=_cparams("parallel", "arbitrary"),
        name="lru_bwd" if bwd else "lru_fwd",
    )(*args)


def rope_tables(ctx_len, seq):
    rows = seq // GRID_W
    axis_dim = HEAD_DIM // 2
    row = jnp.repeat(jnp.arange(rows), GRID_W).astype(F32)
    col = jnp.tile(jnp.arange(GRID_W), rows).astype(F32)
    inv = ROPE_THETA ** (-jnp.arange(0, axis_dim, 2, dtype=F32) / axis_dim)
    ang = jnp.concatenate([row[:, None] * inv, col[:, None] * inv], axis=-1)
    cos, sin = jnp.cos(ang), jnp.sin(ang)
    cos_t = jnp.repeat(cos, 2, axis=-1)
    sin_t = jnp.stack([-sin, sin], axis=-1).reshape(seq, HEAD_DIM)
    cos_t = jnp.concatenate([jnp.ones((ctx_len, HEAD_DIM), F32), cos_t], axis=0)
    sin_t = jnp.concatenate([jnp.zeros((ctx_len, HEAD_DIM), F32), sin_t], axis=0)
    return cos_t, sin_t


def even_layer(x_all, mod, g, w_in, w_out, j, q_g, k_g, sgu_g, ws, bs, cos_t, sin_t, ctx_len, latent_only):
    shift_l, scale_l, gate_l, shift_c, scale_c, gate_c = mod
    slabs = inproj(x_all, shift_l, scale_l, shift_c, scale_c, g, w_in, j, ctx_len, tn=512,
                   tm=_pick_tile(x_all.shape[1], 1152), out_dtype=BF16)
    q_gain = (q_g * (HEAD_DIM ** -0.5 * math.log2(math.e)))[None, :]
    mix_a = attention(slabs, q_gain, k_g[None, :], cos_t, sin_t, ctx_len)
    mix_b = sgu(slabs, ws.astype(BF16), bs[:, :, None], sgu_g[None, :])
    return outproj(mix_a, mix_b, w_out, j, x_all, gate_l, gate_c, ctx_len, latent_only)


def odd_layer(x_all, mod, g, w_in, w_out, j, s5_ops, d_skip, glu_w, glu_b, conv_w, conv_b, lam, wa, ba, wx, bx,
              ctx_len, latent_only):
    shift_l, scale_l, gate_l, shift_c, scale_c, gate_c = mod
    b, t, _ = x_all.shape
    slabs = inproj(x_all, shift_l, scale_l, shift_c, scale_c, g, w_in, j, ctx_len, tn=1024,
                   tm=_pick_tile(t, 1152), out_dtype=F32)
    cw = slabs.shape[3]
    kcat, win, wout_src, consts = s5_ops
    t_intra, w_s5in, w_s5out = s5_expand(kcat, win, wout_src, j, kcat.shape[-2], C_STATE)
    slabs2 = slabs.reshape(slabs.shape[0], b * t, cw)
    y = s5_core(slabs2, w_s5in, t_intra, w_s5out, consts, j, b, ctx_len)
    mix_c = s5_finish(y, slabs2, d_skip[None, :], glu_w.astype(BF16), glu_b[None, :],
                      tm=_pick_tile(b * t, 512))
    hf = lru_pass(slabs, conv_w, conv_b[None, :], lam[0][None, :], wa[0].astype(BF16), ba[0][None, :],
                  wx[0].astype(BF16), bx[0][None, :], ctx_len)
    mix_d = lru_pass(slabs, conv_w, conv_b[None, :], lam[1][None, :], wa[1].astype(BF16), ba[1][None, :],
                     wx[1].astype(BF16), bx[1][None, :], ctx_len, hf=hf)
    return outproj(mix_c.reshape(b, t, cw), mix_d, w_out, j, x_all, gate_l, gate_c, ctx_len, latent_only)


def kernel(x, c, ctx, c_ctx, ada_w, ada_b, norm_g, ev_w_in, ev_w_out, ev_q_g, ev_k_g, ev_sgu_g, ev_ws, ev_bs,
           od_w_in, od_w_out, s5_lam_re, s5_lam_im, s5_log_dt, s5_b_re, s5_b_im, s5_c_re, s5_c_im, s5_d,
           s5_glu_w, s5_glu_b, lru_conv_w, lru_conv_b, lru_lam, lru_wa, lru_ba, lru_wx, lru_bx):
    b, seq, d = x.shape
    ctx_len = ctx.shape[1]
    depth = ada_w.shape[0]
    assert b + 1 <= SUBLANES and seq % ctx_len == 0
    cos_t, sin_t = rope_tables(ctx_len, seq)
    cc = jnp.zeros((2 * SUBLANES, d), F32).at[:b].set(c).at[b].set(c_ctx)
    mods = adaln_all(cc, ada_w, ada_b)
    x_all = jnp.concatenate([ctx, x], axis=1)
    s5_ops = jax.vmap(s5_derive)(s5_lam_re, s5_lam_im, s5_log_dt, s5_b_re, s5_b_im, s5_c_re, s5_c_im)
    ev_w_in, ev_w_out = ev_w_in.astype(BF16), ev_w_out.astype(BF16)
    od_w_in, od_w_out = od_w_in.astype(BF16), od_w_out.astype(BF16)
    for layer in range(depth):
        m = mods[layer]
        mod = tuple(m[:b, None, k * d:(k + 1) * d] for k in range(3)) + \
            tuple(m[b:b + 1, k * d:(k + 1) * d] for k in range(3))
        last = layer == depth - 1
        j = layer // 2
        g = norm_g[layer][None, :]
        if layer % 2 == 0:
            x_all = even_layer(x_all, mod, g, ev_w_in, ev_w_out, j, ev_q_g[j], ev_k_g[j], ev_sgu_g[j],
                               ev_ws[j], ev_bs[j], cos_t, sin_t, ctx_len, last)
        else:
            x_all = odd_layer(x_all, mod, g, od_w_in, od_w_out, j, s5_ops,
                              s5_d[j], s5_glu_w[j], s5_glu_b[j],
                              lru_conv_w[j], lru_conv_b[j], lru_lam[j], lru_wa[j], lru_ba[j], lru_wx[j],
                              lru_bx[j], ctx_len, last)
    return x_all
```

```python
import functools
import math

import jax
import jax.numpy as jnp
from jax import lax
from jax.experimental import pallas as pl
from jax.experimental.pallas import tpu as pltpu

F32 = jnp.float32
BF16 = jnp.bfloat16

EPS = 1e-6
GRID_W = 64
HEAD_DIM = 128
A_Q_HEADS = 12
A_KV_HEADS = 4
A_GROUP = A_Q_HEADS // A_KV_HEADS
B_GROUPS = 4
CHUNK = 128
ROPE_THETA = 10000.0
C_GROUP_DIM = 16
C_STATE = 64
D_BLOCKS = 8
CONV_W = 4
LRU_C = 8.0

LANES = 128
SUBLANES = 8
SUB = 8
PIECE_GROUPS = LANES // C_GROUP_DIM
KEY_CHUNKS = 3
KV_PER_STEP = 4
VMEM_LIMIT = 50 * 1024 * 1024


def _cparams(*sem):
    return pltpu.CompilerParams(dimension_semantics=sem, vmem_limit_bytes=VMEM_LIMIT)


def _sigmoid(x):
    return 0.5 * jnp.tanh(0.5 * x) + 0.5


def _silu(x):
    return x * _sigmoid(x)


def _pick_tile(rows, target, align=16):
    best = None
    for cand in range(align, min(rows, target) + 1, align):
        if rows % cand == 0:
            best = cand
    assert best is not None, (rows, target)
    return best


def _split_bf16(x, parts):
    out = []
    for _ in range(parts):
        hi = x.astype(BF16)
        out.append(hi)
        x = x - hi.astype(F32)
    return out


def _adaln_kernel(cc_ref, w_ref, b_ref, o_ref):
    s_parts = _split_bf16(_silu(cc_ref[...]), 3)
    w = w_ref[0].astype(BF16)
    acc = jnp.dot(s_parts[2], w, preferred_element_type=F32)
    acc = acc + jnp.dot(s_parts[1], w, preferred_element_type=F32)
    acc = acc + jnp.dot(s_parts[0], w, preferred_element_type=F32)
    o_ref[0] = acc + b_ref[0]


def adaln_all(cc, ada_w, ada_b):
    depth, d, n3 = ada_w.shape
    rows = cc.shape[0]
    tn = _pick_tile(n3, 1536, align=LANES)
    return pl.pallas_call(
        _adaln_kernel,
        grid=(depth, n3 // tn),
        in_specs=[pl.BlockSpec((rows, d), lambda l, j: (0, 0)),
                  pl.BlockSpec((1, d, tn), lambda l, j: (l, 0, j)),
                  pl.BlockSpec((1, 1, tn), lambda l, j: (l, 0, j))],
        out_specs=pl.BlockSpec((1, rows, tn), lambda l, j: (l, 0, j)),
        out_shape=jax.ShapeDtypeStruct((depth, rows, n3), F32),
        compiler_params=_cparams("parallel", "parallel"),
        name="adaln",
    )(cc, ada_w, ada_b.reshape(depth, 1, n3))


def _inproj_kernel(x_ref, shl_ref, scl_ref, shc_ref, scc_ref, g_ref, w_ref, o_ref, h_scr, *, tm, ctx_len):
    i = pl.program_id(1)
    first = pl.program_id(2) == 0

    def modulate(rows, sc_ref, sh_ref):
        x = x_ref[0, rows]
        r = lax.rsqrt(jnp.mean(x * x, axis=-1, keepdims=True) + EPS)
        h_scr[rows] = ((x * r) * (g_ref[...] * (1.0 + sc_ref[...])) + sh_ref[...]).astype(BF16)

    scl, shl = scl_ref.at[0], shl_ref.at[0]
    if tm >= ctx_len:
        @pl.when(jnp.logical_and(first, i == 0))
        def _():
            modulate(slice(0, ctx_len), scc_ref, shc_ref)
            if tm > ctx_len:
                modulate(slice(ctx_len, tm), scl, shl)

        @pl.when(jnp.logical_and(first, i > 0))
        def _():
            modulate(slice(0, tm), scl, shl)
    else:
        @pl.when(jnp.logical_and(first, i < ctx_len // tm))
        def _():
            modulate(slice(0, tm), scc_ref, shc_ref)

        @pl.when(jnp.logical_and(first, i >= ctx_len // tm))
        def _():
            modulate(slice(0, tm), scl, shl)

    o_ref[0, 0] = jnp.dot(h_scr[...], w_ref[...], preferred_element_type=F32).astype(o_ref.dtype)


def inproj(x_all, shift_l, scale_l, shift_c, scale_c, g, w, layer, ctx_len, tn, tm, out_dtype):
    b, t, d = x_all.shape
    n = w.shape[2]
    n_slab = n // tn
    assert tm >= ctx_len or ctx_len % tm == 0
    return pl.pallas_call(
        functools.partial(_inproj_kernel, tm=tm, ctx_len=ctx_len),
        grid=(b, t // tm, n_slab),
        in_specs=[pl.BlockSpec((1, tm, d), lambda bb, i, j: (bb, i, 0)),
                  pl.BlockSpec((1, 1, d), lambda bb, i, j: (bb, 0, 0)),
                  pl.BlockSpec((1, 1, d), lambda bb, i, j: (bb, 0, 0)),
                  pl.BlockSpec((1, d), lambda bb, i, j: (0, 0)),
                  pl.BlockSpec((1, d), lambda bb, i, j: (0, 0)),
                  pl.BlockSpec((1, d), lambda bb, i, j: (0, 0)),
                  pl.BlockSpec((None, d, tn), lambda bb, i, j: (layer, 0, j))],
        out_specs=pl.BlockSpec((1, 1, tm, tn), lambda bb, i, j: (j, bb, i, 0)),
        out_shape=jax.ShapeDtypeStruct((n_slab, b, t, tn), out_dtype),
        scratch_shapes=[pltpu.VMEM((tm, d), BF16)],
        compiler_params=_cparams("parallel", "parallel", "arbitrary"),
        name="inproj",
    )(x_all, shift_l, scale_l, shift_c, scale_c, g, w)


def _outproj_kernel(ma_ref, mb_ref, w1_ref, w2_ref, x_ref, gl_ref, gc_ref, o_ref, *, ctx_tiles):
    i = pl.program_id(1)
    acc = jnp.dot(ma_ref[0], w1_ref[...], preferred_element_type=F32)
    acc = acc + jnp.dot(mb_ref[0], w2_ref[...], preferred_element_type=F32)
    gate = jnp.where(i < ctx_tiles, gc_ref[...], gl_ref[0])
    o_ref[0] = x_ref[0] + gate * acc


def outproj(ma, mb, w, layer, x_all, gate_l, gate_c, ctx_len, latent_only):
    b, t, d = x_all.shape
    k1, k2 = ma.shape[2], mb.shape[2]
    assert w.shape[1] == k1 + k2 and k1 % k2 == 0
    tm = ctx_len
    off = 1 if latent_only else 0
    t_out = t - ctx_len if latent_only else t
    return pl.pallas_call(
        functools.partial(_outproj_kernel, ctx_tiles=0 if latent_only else 1),
        grid=(b, t_out // tm),
        in_specs=[pl.BlockSpec((1, tm, k1), lambda bb, i: (bb, i + off, 0)),
                  pl.BlockSpec((1, tm, k2), lambda bb, i: (bb, i + off, 0)),
                  pl.BlockSpec((None, k1, d), lambda bb, i: (layer, 0, 0)),
                  pl.BlockSpec((None, k2, d), lambda bb, i: (layer, k1 // k2, 0)),
                  pl.BlockSpec((1, tm, d), lambda bb, i: (bb, i + off, 0)),
                  pl.BlockSpec((1, 1, d), lambda bb, i: (bb, 0, 0)),
                  pl.BlockSpec((1, d), lambda bb, i: (0, 0))],
        out_specs=pl.BlockSpec((1, tm, d), lambda bb, i: (bb, i, 0)),
        out_shape=jax.ShapeDtypeStruct((b, t_out, d), F32),
        compiler_params=_cparams("parallel", "parallel"),
        name="outproj",
    )(ma, mb, w, w, x_all, gate_l, gate_c)


def _norm_rope(x, gain, cos, sin):
    y = x * lax.rsqrt(jnp.mean(x * x, axis=-1, keepdims=True) + EPS) * gain
    lane = lax.broadcasted_iota(jnp.int32, y.shape, 1)
    partner = jnp.where((lane & 1) == 0, pltpu.roll(y, HEAD_DIM - 1, 1), pltpu.roll(y, 1, 1))
    return y * cos + partner * sin


def _attn_kernel(*refs, ctx_len, tq):
    nq = KV_PER_STEP * A_GROUP
    q_refs, (k_ref, v_ref), ga_refs = refs[:nq], refs[nq:nq + 2], refs[nq + 2:2 * nq + 2]
    qg_ref, kg_ref, cos_ref, sin_ref, o_ref, k_scr, vaug_scr = refs[2 * nq + 2:]
    iq = pl.program_id(2)

    @pl.when(iq == 0)
    def _():
        for j in range(KV_PER_STEP):
            cols = slice(j * HEAD_DIM, (j + 1) * HEAD_DIM)
            k_scr[j] = _norm_rope(k_ref[0, 0, :, cols].astype(F32), kg_ref[...], cos_ref[...],
                                  sin_ref[...]).astype(BF16)
            vaug_scr[j, :, :HEAD_DIM] = v_ref[0, 0, :, cols].astype(BF16)
            vaug_scr[j, :, HEAD_DIM:] = jnp.ones((vaug_scr.shape[1], HEAD_DIM), BF16)

    def attend(n_keys):
        rows = pl.ds(pl.multiple_of(iq * tq, tq), tq)
        cos_q, sin_q = cos_ref[rows, :], sin_ref[rows, :]
        kc = n_keys // KEY_CHUNKS if n_keys % (KEY_CHUNKS * 2 * HEAD_DIM) == 0 else n_keys
        for h, (q_ref, ga_ref) in enumerate(zip(q_refs, ga_refs)):
            j = h // A_GROUP
            q = _norm_rope(q_ref[0, 0].astype(F32), qg_ref[...], cos_q, sin_q).astype(BF16)
            parts = []
            for c0 in range(0, n_keys, kc):
                s = lax.dot_general(q, k_scr[j, c0:c0 + kc], (((1,), (1,)), ((), ())),
                                    preferred_element_type=F32)
                m = jnp.max(s, axis=-1, keepdims=True)
                p = jnp.exp2(s - m).astype(BF16)
                parts.append((m, jnp.dot(p, vaug_scr[j, c0:c0 + kc], preferred_element_type=F32)))
            m_all = functools.reduce(jnp.maximum, [m for m, _ in parts])
            pv = functools.reduce(lambda a, b: a + b, [acc * jnp.exp2(m - m_all) for m, acc in parts])
            o = pv[:, :HEAD_DIM] / pv[:, HEAD_DIM:]
            o_ref[0, :, h * HEAD_DIM:(h + 1) * HEAD_DIM] = (o * _silu(ga_ref[0, 0].astype(F32))).astype(BF16)

    @pl.when(iq == 0)
    def _():
        attend(ctx_len)

    @pl.when(iq > 0)
    def _():
        attend(k_scr.shape[1])


def attention(slabs, q_gain, k_gain, cos_t, sin_t, ctx_len):
    _, b, t, tn = slabs.shape
    per = tn // HEAD_DIM
    tq = ctx_len
    nq = KV_PER_STEP * A_GROUP
    kvw = KV_PER_STEP * HEAD_DIM
    assert A_KV_HEADS % KV_PER_STEP == 0 and tn % kvw == 0

    def head_spec(first_slab, idx):
        def head_map(bb, kv, iq):
            h = kv * nq + idx
            return (first_slab + h // per, bb, iq, h % per)
        return pl.BlockSpec((1, 1, tq, HEAD_DIM), head_map)

    vec = pl.BlockSpec((1, HEAD_DIM), lambda bb, kv, iq: (0, 0))
    table = pl.BlockSpec((t, HEAD_DIM), lambda bb, kv, iq: (0, 0))
    return pl.pallas_call(
        functools.partial(_attn_kernel, ctx_len=ctx_len, tq=tq),
        grid=(b, A_KV_HEADS // KV_PER_STEP, t // tq),
        in_specs=[head_spec(2, idx) for idx in range(nq)]
        + [pl.BlockSpec((1, 1, t, kvw), lambda bb, kv, iq: (0, bb, 0, kv)),
           pl.BlockSpec((1, 1, t, kvw), lambda bb, kv, iq: (1, bb, 0, kv))]
        + [head_spec(7, idx) for idx in range(nq)]
        + [vec, vec, table, table],
        out_specs=pl.BlockSpec((1, tq, nq * HEAD_DIM), lambda bb, kv, iq: (bb, iq, kv)),
        out_shape=jax.ShapeDtypeStruct((b, t, A_Q_HEADS * HEAD_DIM), BF16),
        scratch_shapes=[pltpu.VMEM((KV_PER_STEP, t, HEAD_DIM), BF16),
                        pltpu.VMEM((KV_PER_STEP, t, 2 * HEAD_DIM), BF16)],
        compiler_params=_cparams("parallel", "parallel", "arbitrary"),
        name="attention",
    )(*([slabs] * (2 * nq + 2)), q_gain, k_gain, cos_t, sin_t)


def _sgu_kernel(u_ref, v_ref, gb_ref, ws_ref, bs_ref, g_ref, o_ref, *, n_chunks):
    def chunk(c, carry):
        rows = pl.ds(pl.multiple_of(c * CHUNK, CHUNK), CHUNK)
        outs = []
        for grp in range(B_GROUPS):
            cols = slice(grp * LANES, (grp + 1) * LANES)
            v = v_ref[0, 0, rows, cols].astype(F32)
            vn = v * lax.rsqrt(jnp.mean(v * v, axis=-1, keepdims=True) + EPS) * g_ref[:, cols]
            mixed = jnp.dot(ws_ref[grp], vn.astype(BF16), preferred_element_type=F32) + bs_ref[grp]
            outs.append(u_ref[0, 0, rows, cols].astype(F32) * mixed
                        * _silu(gb_ref[0, 0, rows, cols].astype(F32)))
        o_ref[0, rows, :] = jnp.concatenate(outs, axis=1).astype(BF16)
        return carry

    lax.fori_loop(0, n_chunks, chunk, 0)


def sgu(slabs, ws, bs, g):
    _, b, t, tn = slabs.shape
    spec = lambda s: pl.BlockSpec((1, 1, t, tn), lambda bb: (s, bb, 0, 0))
    return pl.pallas_call(
        functools.partial(_sgu_kernel, n_chunks=t // CHUNK),
        grid=(b,),
        in_specs=[spec(5), spec(6), spec(10),
                  pl.BlockSpec((B_GROUPS, CHUNK, CHUNK), lambda bb: (0, 0, 0)),
                  pl.BlockSpec((B_GROUPS, CHUNK, 1), lambda bb: (0, 0, 0)),
                  pl.BlockSpec((1, tn), lambda bb: (0, 0))],
        out_specs=pl.BlockSpec((1, t, tn), lambda bb: (bb, 0, 0)),
        out_shape=jax.ShapeDtypeStruct((b, t, tn), BF16),
        compiler_params=_cparams("parallel"),
        name="sgu",
    )(slabs, slabs, slabs, ws, bs, g)


def _s5_fold_rows(u_ref, tm):
    return jnp.concatenate([u_ref[pl.ds(t, tm, stride=SUB), :] for t in range(SUB)], axis=1)


def _s5_scan(g_ref, c_ref, o_ref, *, ctx_pairs, all_pairs, half):
    rows = 2 * SUBLANES
    rowi = lax.broadcasted_iota(jnp.int32, (SUBLANES, half), 0)

    def tile(d, g_re, g_im, c_re, c_im):
        bwd = d == 1
        b_re, b_im = g_re, g_im
        for idx, dd in enumerate((1, 2, 4)):
            a_re, a_im = c_ref[d, idx, 0], c_ref[d, idx, 1]
            sh = SUBLANES - dd if bwd else dd
            s_re, s_im = pltpu.roll(b_re, sh, 0), pltpu.roll(b_im, sh, 0)
            b_re, b_im = (b_re + a_re * s_re - a_im * s_im,
                          b_im + a_re * s_im + a_im * s_re)
        a_re, a_im = c_ref[d, 3, 0], c_ref[d, 3, 1]
        h_re = b_re + a_re * c_re - a_im * c_im
        h_im = b_im + a_re * c_im + a_im * c_re
        if bwd:
            p_re = jnp.where(rowi == SUBLANES - 1, c_re, pltpu.roll(h_re, SUBLANES - 1, 0))
            p_im = jnp.where(rowi == SUBLANES - 1, c_im, pltpu.roll(h_im, SUBLANES - 1, 0))
            return p_re, p_im, h_re[0:1], h_im[0:1]
        p_re = jnp.where(rowi == 0, c_re, pltpu.roll(h_re, 1, 0))
        p_im = jnp.where(rowi == 0, c_im, pltpu.roll(h_im, 1, 0))
        return p_re, p_im, h_re[SUBLANES - 1:SUBLANES], h_im[SUBLANES - 1:SUBLANES]

    def pair(d, pi, carry):
        c_re, c_im = carry
        r0 = pl.multiple_of(pi * rows, rows)
        base = d * 2 * half
        order = (1, 0) if d == 1 else (0, 1)
        res = [None, None]
        for which in order:
            rr = pl.ds(r0 + which * SUBLANES, SUBLANES)
            g_re = g_ref[rr, pl.ds(base, half)]
            g_im = g_ref[rr, pl.ds(base + half, half)]
            p_re, p_im, c_re, c_im = tile(d, g_re, g_im, c_re, c_im)
            res[which] = (p_re, p_im)
        o_ref[pl.ds(r0, rows), pl.ds(base, half)] = jnp.concatenate(
            [res[0][0], res[1][0]], axis=0).astype(BF16)
        o_ref[pl.ds(r0, rows), pl.ds(base + half, half)] = jnp.concatenate(
            [res[0][1], res[1][1]], axis=0).astype(BF16)
        return c_re, c_im

    zero = (jnp.zeros((1, half), F32), jnp.zeros((1, half), F32))
    lax.fori_loop(0, all_pairs, lambda pi, c: pair(0, pi, c), zero)
    c = lax.fori_loop(0, ctx_pairs, lambda n, c: pair(1, ctx_pairs - 1 - n, c), zero)
    lax.fori_loop(0, all_pairs - ctx_pairs, lambda n, c: pair(1, all_pairs - 1 - n, c), c)


def _s5_core_kernel(u_ref, win_ref, t_ref, wout_ref, c_ref, y_ref, g_scr, hp_scr, *, tm, ctx_pairs, half):
    x = _s5_fold_rows(u_ref, tm).astype(BF16)
    g_scr[...] = jnp.dot(x, win_ref[0], preferred_element_type=F32)
    _s5_scan(g_scr, c_ref, hp_scr, ctx_pairs=ctx_pairs, all_pairs=tm // (2 * SUBLANES), half=half)
    y = jnp.dot(x, t_ref[0], preferred_element_type=F32)
    y = y + jnp.dot(hp_scr[...], wout_ref[0], preferred_element_type=F32)
    for t in range(SUB):
        y_ref[pl.ds(t, tm, stride=SUB), :] = y[:, t * LANES:(t + 1) * LANES]


def s5_core(slabs2, w_in, t_intra, w_out, consts, layer, batch, ctx_len):
    _, r, width = slabs2.shape
    t = r // batch
    tm = t // SUB
    n_piece = t_intra.shape[0]
    half = PIECE_GROUPS * C_STATE
    assert w_in.shape[2] == 4 * half and tm % (2 * SUBLANES) == 0 and ctx_len % (2 * SUBLANES * SUB) == 0
    return pl.pallas_call(
        functools.partial(_s5_core_kernel, tm=tm, ctx_pairs=ctx_len // SUB // (2 * SUBLANES), half=half),
        grid=(n_piece, batch),
        in_specs=[pl.BlockSpec((None, t, LANES), lambda i, bb: (0, bb, i)),
                  pl.BlockSpec((1,) + w_in.shape[1:], lambda i, bb: (i, 0, 0)),
                  pl.BlockSpec((1,) + t_intra.shape[1:], lambda i, bb: (i, 0, 0)),
                  pl.BlockSpec((1,) + w_out.shape[1:], lambda i, bb: (i, 0, 0)),
                  pl.BlockSpec((None, 2, 4, 2, SUBLANES, half), lambda i, bb: (layer, 0, 0, 0, 0, i))],
        out_specs=pl.BlockSpec((t, LANES), lambda i, bb: (bb, i)),
        out_shape=jax.ShapeDtypeStruct((r, width), F32),
        scratch_shapes=[pltpu.VMEM((tm, 4 * half), F32), pltpu.VMEM((tm, 4 * half), BF16)],
        compiler_params=_cparams("parallel", "parallel"),
        name="s5_core",
    )(slabs2, w_in, t_intra, w_out, consts)


def _s5_finish_kernel(y_ref, u_ref, gc_ref, d_ref, w_ref, b_ref, o_ref):
    y = y_ref[...] + d_ref[...] * u_ref[0]
    y = jax.nn.gelu(y)
    z = jnp.dot(y.astype(BF16), w_ref[...], preferred_element_type=F32) + b_ref[...]
    o_ref[...] = (y * _sigmoid(z) * _silu(gc_ref[0])).astype(BF16)


def s5_finish(y, slabs2, d_skip, glu_w, glu_b, tm):
    r, w = y.shape
    return pl.pallas_call(
        _s5_finish_kernel,
        grid=(r // tm,),
        in_specs=[pl.BlockSpec((tm, w), lambda i: (i, 0)),
                  pl.BlockSpec((1, tm, w), lambda i: (0, i, 0)),
                  pl.BlockSpec((1, tm, w), lambda i: (2, i, 0)),
                  pl.BlockSpec((1, w), lambda i: (0, 0)),
                  pl.BlockSpec((w, w), lambda i: (0, 0)),
                  pl.BlockSpec((1, w), lambda i: (0, 0))],
        out_specs=pl.BlockSpec((tm, w), lambda i: (i, 0)),
        out_shape=jax.ShapeDtypeStruct((r, w), BF16),
        compiler_params=_cparams("parallel"),
        name="s5_finish",
    )(y, slabs2, slabs2, d_skip, glu_w, glu_b)


def _expand_block_diag(src, unit, row_unit):
    rows, cols = src.shape
    wide = cols * PIECE_GROUPS
    lg = lambda v: int(math.log2(v))
    k = lax.broadcasted_iota(jnp.int32, (cols, wide), 0)
    j = lax.broadcasted_iota(jnp.int32, (cols, wide), 1)
    src_col = ((j >> lg(PIECE_GROUPS * unit)) << lg(unit)) + (j & (unit - 1))
    spread = jnp.where(k == src_col, 1.0, 0.0).astype(BF16)
    out = jnp.dot(src.astype(BF16), spread, preferred_element_type=F32)
    rg = (lax.broadcasted_iota(jnp.int32, (rows, wide), 0) >> lg(row_unit)) & (PIECE_GROUPS - 1)
    ch = (lax.broadcasted_iota(jnp.int32, (rows, wide), 1) >> lg(unit)) & (PIECE_GROUPS - 1)
    return jnp.where(rg == ch, out, 0.0).astype(BF16)


def _s5_expand_kernel(kc_ref, wi_ref, wo_ref, t_ref, win_ref, wout_ref, tsrc_scr, *, p, n):
    lane = lax.broadcasted_iota(jnp.int32, (p, LANES), 1)
    for gl in range(PIECE_GROUPS):
        kf = kc_ref[0, 0, gl]
        kb = kc_ref[1, 0, gl]
        for s in range(SUB):
            shift_b = (LANES - p * (SUB - 1 - s)) % LANES
            f = kf if s == 0 else pltpu.roll(kf, p * s, 1)
            r = kb if shift_b == 0 else pltpu.roll(kb, shift_b, 1)
            blk = jnp.where(lane >= p * s, f, 0.0) + jnp.where(lane < p * (s + 1), r, 0.0)
            tsrc_scr[pl.ds((s * PIECE_GROUPS + gl) * p, p), :] = blk
    t_ref[0] = _expand_block_diag(tsrc_scr[...], p, p)
    rows_t = tsrc_scr.shape[0]
    half = win_ref.shape[2] // 2
    win_ref[0, :, :half] = _expand_block_diag(wi_ref[0].reshape(rows_t, LANES), n, p)
    win_ref[0, :, half:] = _expand_block_diag(wi_ref[1].reshape(rows_t, LANES), n, p)
    wout_ref[0] = _expand_block_diag(wo_ref[...].reshape(wout_ref.shape[1], LANES), p, n)


def s5_expand(kcat, win, wout_src, layer, p, n):
    npiece = kcat.shape[2]
    rows_t = SUB * PIECE_GROUPS * p
    rows_o = 2 * 2 * PIECE_GROUPS * n
    return pl.pallas_call(
        functools.partial(_s5_expand_kernel, p=p, n=n),
        grid=(npiece,),
        in_specs=[pl.BlockSpec((None, 2, 1, PIECE_GROUPS, p, LANES), lambda i: (layer, 0, i, 0, 0, 0)),
                  pl.BlockSpec((None, 2, SUB, 1, PIECE_GROUPS * p, LANES), lambda i: (layer, 0, 0, i, 0, 0)),
                  pl.BlockSpec((None, 2, 2, 1, PIECE_GROUPS * n, LANES), lambda i: (layer, 0, 0, i, 0, 0))],
        out_specs=[pl.BlockSpec((1, rows_t, LANES * PIECE_GROUPS), lambda i: (i, 0, 0)),
                   pl.BlockSpec((1, rows_t, 2 * LANES * PIECE_GROUPS), lambda i: (i, 0, 0)),
                   pl.BlockSpec((1, rows_o, LANES * PIECE_GROUPS), lambda i: (i, 0, 0))],
        out_shape=[jax.ShapeDtypeStruct((npiece, rows_t, LANES * PIECE_GROUPS), BF16),
                   jax.ShapeDtypeStruct((npiece, rows_t, 2 * LANES * PIECE_GROUPS), BF16),
                   jax.ShapeDtypeStruct((npiece, rows_o, LANES * PIECE_GROUPS), BF16)],
        scratch_shapes=[pltpu.VMEM((rows_t, LANES), F32)],
        compiler_params=_cparams("parallel"),
        name="s5_expand",
    )(kcat, win, wout_src)


def s5_derive(lam_re, lam_im, log_dt, b_re, b_im, c_re, c_im):
    hi = lax.Precision.HIGHEST
    n_dir, g, n = lam_re.shape
    p = b_re.shape[-1]
    npiece = g // PIECE_GROUPS
    assert n_dir == 2 and SUB * p == LANES
    dt = jnp.exp(log_dt)[:, :, None]
    up = list(range(SUB))
    down = up[::-1]

    def powers(exps):
        k = jnp.asarray(exps, F32)[:, :, None, None]
        mag = jnp.exp(k * (lam_re * dt)[:, None])
        ang = k * (lam_im * dt)[:, None]
        return mag * jnp.cos(ang), mag * jnp.sin(ang)

    a_re, a_im = powers([[1], [1]])
    den = lam_re * lam_re + lam_im * lam_im
    nr, ni = a_re[:, 0] - 1.0, a_im[:, 0]
    cf_re = ((nr * lam_re + ni * lam_im) / den)[:, :, None, :]
    cf_im = ((ni * lam_re - nr * lam_im) / den)[:, :, None, :]
    bt_re, bt_im = jnp.swapaxes(b_re, 2, 3), jnp.swapaxes(b_im, 2, 3)
    bb_re = cf_re * bt_re - cf_im * bt_im
    bb_im = cf_re * bt_im + cf_im * bt_re
    ct_re, ct_im = jnp.swapaxes(c_re, 2, 3), jnp.swapaxes(c_im, 2, 3)

    lane = jnp.arange(SUB * p)
    spread_k = (lane[None, :] // p == jnp.arange(SUB)[:, None]).astype(F32)
    spread_p = (lane[None, :] % p == jnp.arange(p)[:, None]).astype(F32)
    cw_re = jnp.dot(ct_re, spread_p, precision=hi)
    cw_im = jnp.dot(ct_im, spread_p, precision=hi)

    def c_times_powers(exps):
        e_re, e_im = powers(exps)
        e_re = jnp.einsum('dkgn,km->dgnm', e_re, spread_k, precision=hi)
        e_im = jnp.einsum('dkgn,km->dgnm', e_im, spread_k, precision=hi)
        return cw_re * e_re - cw_im * e_im, cw_re * e_im + cw_im * e_re

    ck_re, ck_im = c_times_powers([up, down])
    lhs = jnp.concatenate([bb_re, -bb_im], axis=-1)
    rhs = jnp.concatenate([ck_re, ck_im], axis=2)
    kcat = jnp.einsum('dgqk,dgkm->dgqm', lhs, rhs, precision=hi)
    kcat = kcat.reshape(n_dir, npiece, PIECE_GROUPS, p, SUB * p)

    e_re, e_im = powers([down, up])
    e1 = jnp.concatenate([e_re, e_re], axis=-1).reshape(n_dir, SUB, npiece, PIECE_GROUPS, 1, 2 * n)
    e2 = jnp.concatenate([-e_im, e_im], axis=-1).reshape(n_dir, SUB, npiece, PIECE_GROUPS, 1, 2 * n)
    b1 = jnp.concatenate([bb_re, bb_im], axis=-1).reshape(n_dir, 1, npiece, PIECE_GROUPS, p, 2 * n)
    b2 = jnp.concatenate([bb_im, bb_re], axis=-1).reshape(n_dir, 1, npiece, PIECE_GROUPS, p, 2 * n)
    win = (e1 * b1 + e2 * b2).reshape(n_dir, SUB, npiece, PIECE_GROUPS * p, 2 * n)

    co_re, co_im = c_times_powers([[k + 1 for k in up], [SUB - k for k in up]])
    wout_src = jnp.stack([co_re, -co_im], axis=1).reshape(n_dir, 2, npiece, PIECE_GROUPS * n, SUB * p)

    steps = [SUB * m for m in range(1, SUBLANES + 1)]
    q_re, q_im = powers([steps, steps])
    q_re = q_re.reshape(n_dir, SUBLANES, g * n)
    q_im = q_im.reshape(n_dir, SUBLANES, g * n)
    row = jnp.arange(SUBLANES)
    consts = []
    for d in range(n_dir):
        kinds = []
        for dd in (1, 2, 4):
            keep = (row + dd <= SUBLANES - 1) if d == 1 else (row >= dd)
            m = keep.astype(F32)[:, None]
            kinds.append(jnp.stack([m * q_re[d, dd - 1][None], m * q_im[d, dd - 1][None]]))
        sel = (SUBLANES - 1 - row) if d == 1 else row
        kinds.append(jnp.stack([q_re[d][sel], q_im[d][sel]]))
        consts.append(jnp.stack(kinds))
    consts = jnp.stack(consts)
    return kcat, win, wout_src, consts


def _lru_kernel(*refs, bwd, nch, tc, width):
    if bwd:
        (x_ref, xp_ref, xn_ref, cw_ref, cb_ref, lam_ref, wa_ref, ba_ref, wx_ref, bx_ref,
         hf_ref, gd_ref, o_ref, pad_scr, a_scr, b_scr, carry_scr) = refs
    else:
        (x_ref, xp_ref, xn_ref, cw_ref, cb_ref, lam_ref, wa_ref, ba_ref, wx_ref, bx_ref,
         o_ref, pad_scr, a_scr, b_scr, carry_scr) = refs
    k = pl.program_id(1)
    chunk = jnp.where(k == 0, 0, nch - k) if bwd else k
    prev_ok = jnp.logical_and(chunk != 0, chunk != 1)
    next_ok = jnp.logical_and(chunk != 0, chunk != nch - 1)
    pad_scr[0:SUBLANES] = jnp.where(prev_ok, xp_ref[0, 0], 0.0)
    pad_scr[SUBLANES:SUBLANES + tc] = x_ref[0, 0]
    pad_scr[SUBLANES + tc:2 * SUBLANES + tc] = jnp.where(next_ok, xn_ref[0, 0], 0.0)
    xc = cb_ref[...]
    for tap in range(CONV_W):
        xc = xc + cw_ref[tap:tap + 1] * pad_scr[SUBLANES - 1 + tap:SUBLANES - 1 + tap + tc]

    bw = width // D_BLOCKS
    r_parts, i_parts = [], []
    for blk in range(D_BLOCKS):
        xb = xc[:, blk * bw:(blk + 1) * bw].astype(BF16)
        r_parts.append(jnp.dot(xb, wa_ref[blk], preferred_element_type=F32))
        i_parts.append(jnp.dot(xb, wx_ref[blk], preferred_element_type=F32))
    r = _sigmoid(jnp.concatenate(r_parts, axis=1) + ba_ref[...])
    ig = _sigmoid(jnp.concatenate(i_parts, axis=1) + bx_ref[...])
    nl = -lam_ref[...]
    e = jnp.exp(-jnp.abs(nl))
    e1 = 1.0 + e
    log1p_e = jnp.where(e1 == 1.0, e, jnp.log(e1) * (e / jnp.where(e1 == 1.0, 1.0, e1 - 1.0)))
    softplus = jnp.maximum(nl, 0.0) + log1p_e
    log_a = (-LRU_C * softplus) * r
    a = jnp.exp(log_a)
    a_scr[...] = a
    b_scr[...] = jnp.sqrt(-jnp.tanh(log_a) * (1.0 + a * a)) * (ig * xc)

    @pl.when(k == 0)
    def _():
        carry_scr[...] = jnp.zeros_like(carry_scr)

    rowi = lax.broadcasted_iota(jnp.int32, (SUBLANES, width), 0)
    n_tiles = tc // SUBLANES

    def body(n, carry):
        ti = n_tiles - 1 - n if bwd else n
        rr = pl.ds(pl.multiple_of(ti * SUBLANES, SUBLANES), SUBLANES)
        a, b = a_scr[rr], b_scr[rr]
        for dd in (1, 2, 4):
            keep = (rowi + dd <= SUBLANES - 1) if bwd else (rowi >= dd)
            sh = SUBLANES - dd if bwd else dd
            b = b + a * jnp.where(keep, pltpu.roll(b, sh, 0), 0.0)
            a = a * jnp.where(keep, pltpu.roll(a, sh, 0), 1.0)
        h = b + a * carry
        b_scr[rr] = h
        return h[0:1] if bwd else h[SUBLANES - 1:SUBLANES]

    carry_scr[...] = lax.fori_loop(0, n_tiles, body, carry_scr[...])
    if bwd:
        o_ref[0] = ((hf_ref[0] + b_scr[...]) * _silu(gd_ref[0, 0])).astype(BF16)
    else:
        o_ref[0] = b_scr[...]


def lru_pass(slabs, conv_w, conv_b, lam, wa, ba, wx, bx, ctx_len, hf=None):
    _, b, t, w = slabs.shape
    bwd = hf is not None
    tc = ctx_len
    nch = t // tc
    hb = tc // SUBLANES
    nhb = t // SUBLANES

    def ch(k):
        return jnp.where(k == 0, 0, nch - k) if bwd else k

    in_specs = [pl.BlockSpec((1, 1, tc, w), lambda bb, k: (1, bb, ch(k), 0)),
                pl.BlockSpec((1, 1, SUBLANES, w), lambda bb, k: (1, bb, jnp.maximum(ch(k) * hb - 1, 0), 0)),
                pl.BlockSpec((1, 1, SUBLANES, w), lambda bb, k: (1, bb, jnp.minimum((ch(k) + 1) * hb, nhb - 1), 0)),
                pl.BlockSpec((CONV_W, w), lambda bb, k: (0, 0)),
                pl.BlockSpec((1, w), lambda bb, k: (0, 0)),
                pl.BlockSpec((1, w), lambda bb, k: (0, 0)),
                pl.BlockSpec(wa.shape, lambda bb, k: (0, 0, 0)),
                pl.BlockSpec((1, w), lambda bb, k: (0, 0)),
                pl.BlockSpec(wx.shape, lambda bb, k: (0, 0, 0)),
                pl.BlockSpec((1, w), lambda bb, k: (0, 0))]
    args = [slabs, slabs, slabs, conv_w, conv_b, lam, wa, ba, wx, bx]
    if bwd:
        in_specs += [pl.BlockSpec((1, tc, w), lambda bb, k: (bb, ch(k), 0)),
                     pl.BlockSpec((1, 1, tc, w), lambda bb, k: (3, bb, ch(k), 0))]
        args += [hf, slabs]
    return pl.pallas_call(
        functools.partial(_lru_kernel, bwd=bwd, nch=nch, tc=tc, width=w),
        grid=(b, nch),
        in_specs=in_specs,
        out_specs=pl.BlockSpec((1, tc, w), lambda bb, k: (bb, ch(k), 0)),
        out_shape=jax.ShapeDtypeStruct((b, t, w), BF16 if bwd else F32),
        scratch_shapes=[pltpu.VMEM((tc + 2 * SUBLANES, w), F32),
                        pltpu.VMEM((tc, w), F32),
                        pltpu.VMEM((tc, w), F32),
                        pltpu.VMEM((1, w), F32)],
        compiler_params=_cparams("parallel", "arbitrary"),
        name="lru_bwd" if bwd else "lru_fwd",
    )(*args)


def rope_tables(ctx_len, seq):
    rows = seq // GRID_W
    axis_dim = HEAD_DIM // 2
    row = jnp.repeat(jnp.arange(rows), GRID_W).astype(F32)
    col = jnp.tile(jnp.arange(GRID_W), rows).astype(F32)
    inv = ROPE_THETA ** (-jnp.arange(0, axis_dim, 2, dtype=F32) / axis_dim)
    ang = jnp.concatenate([row[:, None] * inv, col[:, None] * inv], axis=-1)
    cos, sin = jnp.cos(ang), jnp.sin(ang)
    cos_t = jnp.repeat(cos, 2, axis=-1)
    sin_t = jnp.stack([-sin, sin], axis=-1).reshape(seq, HEAD_DIM)
    cos_t = jnp.concatenate([jnp.ones((ctx_len, HEAD_DIM), F32), cos_t], axis=0)
    sin_t = jnp.concatenate([jnp.zeros((ctx_len, HEAD_DIM), F32), sin_t], axis=0)
    return cos_t, sin_t


def even_layer(x_all, mod, g, w_in, w_out, j, q_g, k_g, sgu_g, ws, bs, cos_t, sin_t, ctx_len, latent_only):
    shift_l, scale_l, gate_l, shift_c, scale_c, gate_c = mod
    slabs = inproj(x_all, shift_l, scale_l, shift_c, scale_c, g, w_in, j, ctx_len, tn=512,
                   tm=_pick_tile(x_all.shape[1], 1152), out_dtype=BF16)
    q_gain = (q_g * (HEAD_DIM ** -0.5 * math.log2(math.e)))[None, :]
    mix_a = attention(slabs, q_gain, k_g[None, :], cos_t, sin_t, ctx_len)
    mix_b = sgu(slabs, ws.astype(BF16), bs[:, :, None], sgu_g[None, :])
    return outproj(mix_a, mix_b, w_out, j, x_all, gate_l, gate_c, ctx_len, latent_only)


def odd_layer(x_all, mod, g, w_in, w_out, j, s5_ops, d_skip, glu_w, glu_b, conv_w, conv_b, lam, wa, ba, wx, bx,
              ctx_len, latent_only):
    shift_l, scale_l, gate_l, shift_c, scale_c, gate_c = mod
    b, t, _ = x_all.shape
    slabs = inproj(x_all, shift_l, scale_l, shift_c, scale_c, g, w_in, j, ctx_len, tn=1024,
                   tm=_pick_tile(t, 1152), out_dtype=F32)
    cw = slabs.shape[3]
    kcat, win, wout_src, consts = s5_ops
    t_intra, w_s5in, w_s5out = s5_expand(kcat, win, wout_src, j, kcat.shape[-2], C_STATE)
    slabs2 = slabs.reshape(slabs.shape[0], b * t, cw)
    y = s5_core(slabs2, w_s5in, t_intra, w_s5out, consts, j, b, ctx_len)
    mix_c = s5_finish(y, slabs2, d_skip[None, :], glu_w.astype(BF16), glu_b[None, :],
                      tm=_pick_tile(b * t, 512))
    hf = lru_pass(slabs, conv_w, conv_b[None, :], lam[0][None, :], wa[0].astype(BF16), ba[0][None, :],
                  wx[0].astype(BF16), bx[0][None, :], ctx_len)
    mix_d = lru_pass(slabs, conv_w, conv_b[None, :], lam[1][None, :], wa[1].astype(BF16), ba[1][None, :],
                     wx[1].astype(BF16), bx[1][None, :], ctx_len, hf=hf)
    return outproj(mix_c.reshape(b, t, cw), mix_d, w_out, j, x_all, gate_l, gate_c, ctx_len, latent_only)


def kernel(x, c, ctx, c_ctx, ada_w, ada_b, norm_g, ev_w_in, ev_w_out, ev_q_g, ev_k_g, ev_sgu_g, ev_ws, ev_bs,
           od_w_in, od_w_out, s5_lam_re, s5_lam_im, s5_log_dt, s5_b_re, s5_b_im, s5_c_re, s5_c_im, s5_d,
           s5_glu_w, s5_glu_b, lru_conv_w, lru_conv_b, lru_lam, lru_wa, lru_ba, lru_wx, lru_bx):
    b, seq, d = x.shape
    ctx_len = ctx.shape[1]
    depth = ada_w.shape[0]
    assert b + 1 <= SUBLANES and seq % ctx_len == 0
    cos_t, sin_t = rope_tables(ctx_len, seq)
    cc = jnp.zeros((2 * SUBLANES, d), F32).at[:b].set(c).at[b].set(c_ctx)
    mods = adaln_all(cc, ada_w, ada_b)
    x_all = jnp.concatenate([ctx, x], axis=1)
    s5_ops = jax.vmap(s5_derive)(s5_lam_re, s5_lam_im, s5_log_dt, s5_b_re, s5_b_im, s5_c_re, s5_c_im)
    ev_w_in, ev_w_out = ev_w_in.astype(BF16), ev_w_out.astype(BF16)
    od_w_in, od_w_out = od_w_in.astype(BF16), od_w_out.astype(BF16)
    for layer in range(depth):
        m = mods[layer]
        mod = tuple(m[:b, None, k * d:(k + 1) * d] for k in range(3)) + \
            tuple(m[b:b + 1, k * d:(k + 1) * d] for k in range(3))
        last = layer == depth - 1
        j = layer // 2
        g = norm_g[layer][None, :]
        if layer % 2 == 0:
            x_all = even_layer(x_all, mod, g, ev_w_in, ev_w_out, j, ev_q_g[j], ev_k_g[j], ev_sgu_g[j],
                               ev_ws[j], ev_bs[j], cos_t, sin_t, ctx_len, last)
        else:
            x_all = odd_layer(x_all, mod, g, od_w_in, od_w_out, j, s5_ops,
                              s5_d[j], s5_glu_w[j], s5_glu_b[j],
                              lru_conv_w[j], lru_conv_b[j], lru_lam[j], lru_wa[j], lru_ba[j], lru_wx[j],
                              lru_bx[j], ctx_len, last)
    return x_all
```

```python
import functools
import math

import jax
import jax.numpy as jnp
from jax import lax
from jax.experimental import pallas as pl
from jax.experimental.pallas import tpu as pltpu

F32 = jnp.float32
BF16 = jnp.bfloat16

EPS = 1e-6
GRID_W = 64
HEAD_DIM = 128
A_Q_HEADS = 12
A_KV_HEADS = 4
A_GROUP = A_Q_HEADS // A_KV_HEADS
B_GROUPS = 4
CHUNK = 128
ROPE_THETA = 10000.0
C_GROUP_DIM = 16
C_STATE = 64
D_BLOCKS = 8
CONV_W = 4
LRU_C = 8.0

LANES = 128
SUBLANES = 8
SUB = 8
PIECE_GROUPS = LANES // C_GROUP_DIM
KV_PER_STEP = 4
VMEM_LIMIT = 50 * 1024 * 1024


def _cparams(*sem):
    return pltpu.CompilerParams(dimension_semantics=sem, vmem_limit_bytes=VMEM_LIMIT)


def _sigmoid(x):
    return 0.5 * jnp.tanh(0.5 * x) + 0.5


def _silu(x):
    return x * _sigmoid(x)


def _pick_tile(rows, target, align=16):
    best = None
    for cand in range(align, min(rows, target) + 1, align):
        if rows % cand == 0:
            best = cand
    assert best is not None, (rows, target)
    return best


def _split_bf16(x, parts):
    out = []
    for _ in range(parts):
        hi = x.astype(BF16)
        out.append(hi)
        x = x - hi.astype(F32)
    return out


def _adaln_kernel(cc_ref, w_ref, b_ref, o_ref):
    s_parts = _split_bf16(_silu(cc_ref[...]), 3)
    w = w_ref[0].astype(BF16)
    acc = jnp.dot(s_parts[2], w, preferred_element_type=F32)
    acc = acc + jnp.dot(s_parts[1], w, preferred_element_type=F32)
    acc = acc + jnp.dot(s_parts[0], w, preferred_element_type=F32)
    o_ref[0] = acc + b_ref[0]


def adaln_all(cc, ada_w, ada_b):
    depth, d, n3 = ada_w.shape
    rows = cc.shape[0]
    tn = _pick_tile(n3, 1536, align=LANES)
    return pl.pallas_call(
        _adaln_kernel,
        grid=(depth, n3 // tn),
        in_specs=[pl.BlockSpec((rows, d), lambda l, j: (0, 0)),
                  pl.BlockSpec((1, d, tn), lambda l, j: (l, 0, j)),
                  pl.BlockSpec((1, 1, tn), lambda l, j: (l, 0, j))],
        out_specs=pl.BlockSpec((1, rows, tn), lambda l, j: (l, 0, j)),
        out_shape=jax.ShapeDtypeStruct((depth, rows, n3), F32),
        compiler_params=_cparams("parallel", "parallel"),
        name="adaln",
    )(cc, ada_w, ada_b.reshape(depth, 1, n3))


def _inproj_kernel(x_ref, shl_ref, scl_ref, shc_ref, scc_ref, g_ref, w_ref, o_ref, h_scr, *, tm, ctx_len):
    i = pl.program_id(1)
    first = pl.program_id(2) == 0

    def modulate(rows, sc_ref, sh_ref):
        x = x_ref[0, rows]
        r = lax.rsqrt(jnp.mean(x * x, axis=-1, keepdims=True) + EPS)
        h_scr[rows] = ((x * r) * (g_ref[...] * (1.0 + sc_ref[...])) + sh_ref[...]).astype(BF16)

    scl, shl = scl_ref.at[0], shl_ref.at[0]
    if tm >= ctx_len:
        @pl.when(jnp.logical_and(first, i == 0))
        def _():
            modulate(slice(0, ctx_len), scc_ref, shc_ref)
            if tm > ctx_len:
                modulate(slice(ctx_len, tm), scl, shl)

        @pl.when(jnp.logical_and(first, i > 0))
        def _():
            modulate(slice(0, tm), scl, shl)
    else:
        @pl.when(jnp.logical_and(first, i < ctx_len // tm))
        def _():
            modulate(slice(0, tm), scc_ref, shc_ref)

        @pl.when(jnp.logical_and(first, i >= ctx_len // tm))
        def _():
            modulate(slice(0, tm), scl, shl)

    o_ref[0, 0] = jnp.dot(h_scr[...], w_ref[...], preferred_element_type=F32).astype(o_ref.dtype)


def inproj(x_all, shift_l, scale_l, shift_c, scale_c, g, w, layer, ctx_len, tn, tm, out_dtype):
    b, t, d = x_all.shape
    n = w.shape[2]
    n_slab = n // tn
    assert tm >= ctx_len or ctx_len % tm == 0
    return pl.pallas_call(
        functools.partial(_inproj_kernel, tm=tm, ctx_len=ctx_len),
        grid=(b, t // tm, n_slab),
        in_specs=[pl.BlockSpec((1, tm, d), lambda bb, i, j: (bb, i, 0)),
                  pl.BlockSpec((1, 1, d), lambda bb, i, j: (bb, 0, 0)),
                  pl.BlockSpec((1, 1, d), lambda bb, i, j: (bb, 0, 0)),
                  pl.BlockSpec((1, d), lambda bb, i, j: (0, 0)),
                  pl.BlockSpec((1, d), lambda bb, i, j: (0, 0)),
                  pl.BlockSpec((1, d), lambda bb, i, j: (0, 0)),
                  pl.BlockSpec((None, d, tn), lambda bb, i, j: (layer, 0, j))],
        out_specs=pl.BlockSpec((1, 1, tm, tn), lambda bb, i, j: (j, bb, i, 0)),
        out_shape=jax.ShapeDtypeStruct((n_slab, b, t, tn), out_dtype),
        scratch_shapes=[pltpu.VMEM((tm, d), BF16)],
        compiler_params=_cparams("parallel", "parallel", "arbitrary"),
        name="inproj",
    )(x_all, shift_l, scale_l, shift_c, scale_c, g, w)


def _outproj_kernel(ma_ref, mb_ref, w1_ref, w2_ref, x_ref, gl_ref, gc_ref, o_ref, *, ctx_tiles):
    i = pl.program_id(1)
    acc = jnp.dot(ma_ref[0], w1_ref[...], preferred_element_type=F32)
    acc = acc + jnp.dot(mb_ref[0], w2_ref[...], preferred_element_type=F32)
    gate = jnp.where(i < ctx_tiles, gc_ref[...], gl_ref[0])
    o_ref[0] = x_ref[0] + gate * acc


def outproj(ma, mb, w, layer, x_all, gate_l, gate_c, ctx_len, latent_only):
    b, t, d = x_all.shape
    k1, k2 = ma.shape[2], mb.shape[2]
    assert w.shape[1] == k1 + k2 and k1 % k2 == 0
    tm = ctx_len
    off = 1 if latent_only else 0
    t_out = t - ctx_len if latent_only else t
    return pl.pallas_call(
        functools.partial(_outproj_kernel, ctx_tiles=0 if latent_only else 1),
        grid=(b, t_out // tm),
        in_specs=[pl.BlockSpec((1, tm, k1), lambda bb, i: (bb, i + off, 0)),
                  pl.BlockSpec((1, tm, k2), lambda bb, i: (bb, i + off, 0)),
                  pl.BlockSpec((None, k1, d), lambda bb, i: (layer, 0, 0)),
                  pl.BlockSpec((None, k2, d), lambda bb, i: (layer, k1 // k2, 0)),
                  pl.BlockSpec((1, tm, d), lambda bb, i: (bb, i + off, 0)),
                  pl.BlockSpec((1, 1, d), lambda bb, i: (bb, 0, 0)),
                  pl.BlockSpec((1, d), lambda bb, i: (0, 0))],
        out_specs=pl.BlockSpec((1, tm, d), lambda bb, i: (bb, i, 0)),
        out_shape=jax.ShapeDtypeStruct((b, t_out, d), F32),
        compiler_params=_cparams("parallel", "parallel"),
        name="outproj",
    )(ma, mb, w, w, x_all, gate_l, gate_c)


def _norm_rope(x, gain, cos, sin):
    y = x * lax.rsqrt(jnp.mean(x * x, axis=-1, keepdims=True) + EPS) * gain
    lane = lax.broadcasted_iota(jnp.int32, y.shape, 1)
    partner = jnp.where((lane & 1) == 0, pltpu.roll(y, HEAD_DIM - 1, 1), pltpu.roll(y, 1, 1))
    return y * cos + partner * sin


def _attn_kernel(*refs, ctx_len, tq):
    nq = KV_PER_STEP * A_GROUP
    q_refs, (k_ref, v_ref), ga_refs = refs[:nq], refs[nq:nq + 2], refs[nq + 2:2 * nq + 2]
    qg_ref, kg_ref, cos_ref, sin_ref, o_ref, k_scr, vaug_scr = refs[2 * nq + 2:]
    iq = pl.program_id(2)

    @pl.when(iq == 0)
    def _():
        for j in range(KV_PER_STEP):
            cols = slice(j * HEAD_DIM, (j + 1) * HEAD_DIM)
            k_scr[j] = _norm_rope(k_ref[0, 0, :, cols].astype(F32), kg_ref[...], cos_ref[...],
                                  sin_ref[...]).astype(BF16)
            vaug_scr[j, :, :HEAD_DIM] = v_ref[0, 0, :, cols].astype(BF16)
            vaug_scr[j, :, HEAD_DIM:] = jnp.ones((vaug_scr.shape[1], HEAD_DIM), BF16)

    def attend(n_keys):
        rows = pl.ds(pl.multiple_of(iq * tq, tq), tq)
        cos_q, sin_q = cos_ref[rows, :], sin_ref[rows, :]
        for h, (q_ref, ga_ref) in enumerate(zip(q_refs, ga_refs)):
            j = h // A_GROUP
            q = _norm_rope(q_ref[0, 0].astype(F32), qg_ref[...], cos_q, sin_q).astype(BF16)
            s = lax.dot_general(q, k_scr[j, :n_keys], (((1,), (1,)), ((), ())), preferred_element_type=F32)
            p = jnp.exp2(s - jnp.max(s, axis=-1, keepdims=True)).astype(BF16)
            pv = jnp.dot(p, vaug_scr[j, :n_keys], preferred_element_type=F32)
            o = pv[:, :HEAD_DIM] / pv[:, HEAD_DIM:]
            o_ref[0, :, h * HEAD_DIM:(h + 1) * HEAD_DIM] = (o * _silu(ga_ref[0, 0].astype(F32))).astype(BF16)

    @pl.when(iq == 0)
    def _():
        attend(ctx_len)

    @pl.when(iq > 0)
    def _():
        attend(k_scr.shape[1])


def attention(slabs, q_gain, k_gain, cos_t, sin_t, ctx_len):
    _, b, t, tn = slabs.shape
    per = tn // HEAD_DIM
    tq = ctx_len
    nq = KV_PER_STEP * A_GROUP
    kvw = KV_PER_STEP * HEAD_DIM
    assert A_KV_HEADS % KV_PER_STEP == 0 and tn % kvw == 0

    def head_spec(first_slab, idx):
        def head_map(bb, kv, iq):
            h = kv * nq + idx
            return (first_slab + h // per, bb, iq, h % per)
        return pl.BlockSpec((1, 1, tq, HEAD_DIM), head_map)

    vec = pl.BlockSpec((1, HEAD_DIM), lambda bb, kv, iq: (0, 0))
    table = pl.BlockSpec((t, HEAD_DIM), lambda bb, kv, iq: (0, 0))
    return pl.pallas_call(
        functools.partial(_attn_kernel, ctx_len=ctx_len, tq=tq),
        grid=(b, A_KV_HEADS // KV_PER_STEP, t // tq),
        in_specs=[head_spec(2, idx) for idx in range(nq)]
        + [pl.BlockSpec((1, 1, t, kvw), lambda bb, kv, iq: (0, bb, 0, kv)),
           pl.BlockSpec((1, 1, t, kvw), lambda bb, kv, iq: (1, bb, 0, kv))]
        + [head_spec(7, idx) for idx in range(nq)]
        + [vec, vec, table, table],
        out_specs=pl.BlockSpec((1, tq, nq * HEAD_DIM), lambda bb, kv, iq: (bb, iq, kv)),
        out_shape=jax.ShapeDtypeStruct((b, t, A_Q_HEADS * HEAD_DIM), BF16),
        scratch_shapes=[pltpu.VMEM((KV_PER_STEP, t, HEAD_DIM), BF16),
                        pltpu.VMEM((KV_PER_STEP, t, 2 * HEAD_DIM), BF16)],
        compiler_params=_cparams("parallel", "parallel", "arbitrary"),
        name="attention",
    )(*([slabs] * (2 * nq + 2)), q_gain, k_gain, cos_t, sin_t)


def _sgu_kernel(u_ref, v_ref, gb_ref, ws_ref, bs_ref, g_ref, o_ref, *, n_chunks):
    def chunk(c, carry):
        rows = pl.ds(pl.multiple_of(c * CHUNK, CHUNK), CHUNK)
        outs = []
        for grp in range(B_GROUPS):
            cols = slice(grp * LANES, (grp + 1) * LANES)
            v = v_ref[0, 0, rows, cols].astype(F32)
            vn = v * lax.rsqrt(jnp.mean(v * v, axis=-1, keepdims=True) + EPS) * g_ref[:, cols]
            mixed = jnp.dot(ws_ref[grp], vn.astype(BF16), preferred_element_type=F32) + bs_ref[grp]
            outs.append(u_ref[0, 0, rows, cols].astype(F32) * mixed
                        * _silu(gb_ref[0, 0, rows, cols].astype(F32)))
        o_ref[0, rows, :] = jnp.concatenate(outs, axis=1).astype(BF16)
        return carry

    lax.fori_loop(0, n_chunks, chunk, 0)


def sgu(slabs, ws, bs, g):
    _, b, t, tn = slabs.shape
    spec = lambda s: pl.BlockSpec((1, 1, t, tn), lambda bb: (s, bb, 0, 0))
    return pl.pallas_call(
        functools.partial(_sgu_kernel, n_chunks=t // CHUNK),
        grid=(b,),
        in_specs=[spec(5), spec(6), spec(10),
                  pl.BlockSpec((B_GROUPS, CHUNK, CHUNK), lambda bb: (0, 0, 0)),
                  pl.BlockSpec((B_GROUPS, CHUNK, 1), lambda bb: (0, 0, 0)),
                  pl.BlockSpec((1, tn), lambda bb: (0, 0))],
        out_specs=pl.BlockSpec((1, t, tn), lambda bb: (bb, 0, 0)),
        out_shape=jax.ShapeDtypeStruct((b, t, tn), BF16),
        compiler_params=_cparams("parallel"),
        name="sgu",
    )(slabs, slabs, slabs, ws, bs, g)


def _s5_fold_rows(u_ref, tm):
    return jnp.concatenate([u_ref[pl.ds(t, tm, stride=SUB), :] for t in range(SUB)], axis=1)


def _s5_scan(g_ref, c_ref, o_ref, *, ctx_pairs, all_pairs, half):
    rows = 2 * SUBLANES
    rowi = lax.broadcasted_iota(jnp.int32, (SUBLANES, half), 0)

    def tile(d, g_re, g_im, c_re, c_im):
        bwd = d == 1
        b_re, b_im = g_re, g_im
        for idx, dd in enumerate((1, 2, 4)):
            a_re, a_im = c_ref[d, idx, 0], c_ref[d, idx, 1]
            sh = SUBLANES - dd if bwd else dd
            s_re, s_im = pltpu.roll(b_re, sh, 0), pltpu.roll(b_im, sh, 0)
            b_re, b_im = (b_re + a_re * s_re - a_im * s_im,
                          b_im + a_re * s_im + a_im * s_re)
        a_re, a_im = c_ref[d, 3, 0], c_ref[d, 3, 1]
        h_re = b_re + a_re * c_re - a_im * c_im
        h_im = b_im + a_re * c_im + a_im * c_re
        if bwd:
            p_re = jnp.where(rowi == SUBLANES - 1, c_re, pltpu.roll(h_re, SUBLANES - 1, 0))
            p_im = jnp.where(rowi == SUBLANES - 1, c_im, pltpu.roll(h_im, SUBLANES - 1, 0))
            return p_re, p_im, h_re[0:1], h_im[0:1]
        p_re = jnp.where(rowi == 0, c_re, pltpu.roll(h_re, 1, 0))
        p_im = jnp.where(rowi == 0, c_im, pltpu.roll(h_im, 1, 0))
        return p_re, p_im, h_re[SUBLANES - 1:SUBLANES], h_im[SUBLANES - 1:SUBLANES]

    def pair(d, pi, carry):
        c_re, c_im = carry
        r0 = pl.multiple_of(pi * rows, rows)
        base = d * 2 * half
        order = (1, 0) if d == 1 else (0, 1)
        res = [None, None]
        for which in order:
            rr = pl.ds(r0 + which * SUBLANES, SUBLANES)
            g_re = g_ref[rr, pl.ds(base, half)]
            g_im = g_ref[rr, pl.ds(base + half, half)]
            p_re, p_im, c_re, c_im = tile(d, g_re, g_im, c_re, c_im)
            res[which] = (p_re, p_im)
        o_ref[pl.ds(r0, rows), pl.ds(base, half)] = jnp.concatenate(
            [res[0][0], res[1][0]], axis=0).astype(BF16)
        o_ref[pl.ds(r0, rows), pl.ds(base + half, half)] = jnp.concatenate(
            [res[0][1], res[1][1]], axis=0).astype(BF16)
        return c_re, c_im

    zero = (jnp.zeros((1, half), F32), jnp.zeros((1, half), F32))
    lax.fori_loop(0, all_pairs, lambda pi, c: pair(0, pi, c), zero)
    c = lax.fori_loop(0, ctx_pairs, lambda n, c: pair(1, ctx_pairs - 1 - n, c), zero)
    lax.fori_loop(0, all_pairs - ctx_pairs, lambda n, c: pair(1, all_pairs - 1 - n, c), c)


def _s5_core_kernel(u_ref, win_ref, t_ref, wout_ref, c_ref, y_ref, g_scr, hp_scr, *, tm, ctx_pairs, half):
    x = _s5_fold_rows(u_ref, tm).astype(BF16)
    g_scr[...] = jnp.dot(x, win_ref[0], preferred_element_type=F32)
    _s5_scan(g_scr, c_ref, hp_scr, ctx_pairs=ctx_pairs, all_pairs=tm // (2 * SUBLANES), half=half)
    y = jnp.dot(x, t_ref[0], preferred_element_type=F32)
    y = y + jnp.dot(hp_scr[...], wout_ref[0], preferred_element_type=F32)
    for t in range(SUB):
        y_ref[pl.ds(t, tm, stride=SUB), :] = y[:, t * LANES:(t + 1) * LANES]


def s5_core(slabs2, w_in, t_intra, w_out, consts, layer, batch, ctx_len):
    _, r, width = slabs2.shape
    t = r // batch
    tm = t // SUB
    n_piece = t_intra.shape[0]
    half = PIECE_GROUPS * C_STATE
    assert w_in.shape[2] == 4 * half and tm % (2 * SUBLANES) == 0 and ctx_len % (2 * SUBLANES * SUB) == 0
    return pl.pallas_call(
        functools.partial(_s5_core_kernel, tm=tm, ctx_pairs=ctx_len // SUB // (2 * SUBLANES), half=half),
        grid=(n_piece, batch),
        in_specs=[pl.BlockSpec((None, t, LANES), lambda i, bb: (0, bb, i)),
                  pl.BlockSpec((1,) + w_in.shape[1:], lambda i, bb: (i, 0, 0)),
                  pl.BlockSpec((1,) + t_intra.shape[1:], lambda i, bb: (i, 0, 0)),
                  pl.BlockSpec((1,) + w_out.shape[1:], lambda i, bb: (i, 0, 0)),
                  pl.BlockSpec((None, 2, 4, 2, SUBLANES, half), lambda i, bb: (layer, 0, 0, 0, 0, i))],
        out_specs=pl.BlockSpec((t, LANES), lambda i, bb: (bb, i)),
        out_shape=jax.ShapeDtypeStruct((r, width), F32),
        scratch_shapes=[pltpu.VMEM((tm, 4 * half), F32), pltpu.VMEM((tm, 4 * half), BF16)],
        compiler_params=_cparams("parallel", "parallel"),
        name="s5_core",
    )(slabs2, w_in, t_intra, w_out, consts)


def _s5_finish_kernel(y_ref, u_ref, gc_ref, d_ref, w_ref, b_ref, o_ref):
    y = y_ref[...] + d_ref[...] * u_ref[0]
    y = jax.nn.gelu(y)
    z = jnp.dot(y.astype(BF16), w_ref[...], preferred_element_type=F32) + b_ref[...]
    o_ref[...] = (y * _sigmoid(z) * _silu(gc_ref[0])).astype(BF16)


def s5_finish(y, slabs2, d_skip, glu_w, glu_b, tm):
    r, w = y.shape
    return pl.pallas_call(
        _s5_finish_kernel,
        grid=(r // tm,),
        in_specs=[pl.BlockSpec((tm, w), lambda i: (i, 0)),
                  pl.BlockSpec((1, tm, w), lambda i: (0, i, 0)),
                  pl.BlockSpec((1, tm, w), lambda i: (2, i, 0)),
                  pl.BlockSpec((1, w), lambda i: (0, 0)),
                  pl.BlockSpec((w, w), lambda i: (0, 0)),
                  pl.BlockSpec((1, w), lambda i: (0, 0))],
        out_specs=pl.BlockSpec((tm, w), lambda i: (i, 0)),
        out_shape=jax.ShapeDtypeStruct((r, w), BF16),
        compiler_params=_cparams("parallel"),
        name="s5_finish",
    )(y, slabs2, slabs2, d_skip, glu_w, glu_b)


def _expand_block_diag(src, unit, row_unit):
    rows, cols = src.shape
    wide = cols * PIECE_GROUPS
    lg = lambda v: int(math.log2(v))
    k = lax.broadcasted_iota(jnp.int32, (cols, wide), 0)
    j = lax.broadcasted_iota(jnp.int32, (cols, wide), 1)
    src_col = ((j >> lg(PIECE_GROUPS * unit)) << lg(unit)) + (j & (unit - 1))
    spread = jnp.where(k == src_col, 1.0, 0.0).astype(BF16)
    out = jnp.dot(src.astype(BF16), spread, preferred_element_type=F32)
    rg = (lax.broadcasted_iota(jnp.int32, (rows, wide), 0) >> lg(row_unit)) & (PIECE_GROUPS - 1)
    ch = (lax.broadcasted_iota(jnp.int32, (rows, wide), 1) >> lg(unit)) & (PIECE_GROUPS - 1)
    return jnp.where(rg == ch, out, 0.0).astype(BF16)


def _s5_expand_kernel(kc_ref, ep_ref, bp_ref, wo_ref, t_ref, win_ref, wout_ref, tsrc_scr, *, p, n):
    lane = lax.broadcasted_iota(jnp.int32, (p, LANES), 1)
    for gl in range(PIECE_GROUPS):
        kf = kc_ref[0, 0, gl]
        kb = kc_ref[1, 0, gl]
        for s in range(SUB):
            shift_b = (LANES - p * (SUB - 1 - s)) % LANES
            f = kf if s == 0 else pltpu.roll(kf, p * s, 1)
            r = kb if shift_b == 0 else pltpu.roll(kb, shift_b, 1)
            blk = jnp.where(lane >= p * s, f, 0.0) + jnp.where(lane < p * (s + 1), r, 0.0)
            tsrc_scr[pl.ds((s * PIECE_GROUPS + gl) * p, p), :] = blk
    t_ref[0] = _expand_block_diag(tsrc_scr[...], p, p)
    half = win_ref.shape[2] // 2
    for d in range(2):
        for s in range(SUB):
            for gl in range(PIECE_GROUPS):
                rows = slice(gl * p, (gl + 1) * p)
                blk = (ep_ref[0, d, 0, s, gl:gl + 1, :] * bp_ref[0, d, 0, rows, :]
                       + ep_ref[1, d, 0, s, gl:gl + 1, :] * bp_ref[1, d, 0, rows, :])
                tsrc_scr[pl.ds((s * PIECE_GROUPS + gl) * p, p), :] = blk
        win_ref[0, :, d * half:(d + 1) * half] = _expand_block_diag(tsrc_scr[...], n, p)
    wout_ref[0] = _expand_block_diag(wo_ref[...].reshape(wout_ref.shape[1], LANES), p, n)


def s5_expand(kcat, e_pair, b_pair, wout_src, layer, p, n):
    npiece = kcat.shape[2]
    rows_t = SUB * PIECE_GROUPS * p
    rows_o = 2 * 2 * PIECE_GROUPS * n
    return pl.pallas_call(
        functools.partial(_s5_expand_kernel, p=p, n=n),
        grid=(npiece,),
        in_specs=[pl.BlockSpec((None, 2, 1, PIECE_GROUPS, p, LANES), lambda i: (layer, 0, i, 0, 0, 0)),
                  pl.BlockSpec((None, 2, 2, 1, SUB, PIECE_GROUPS, LANES), lambda i: (layer, 0, 0, i, 0, 0, 0)),
                  pl.BlockSpec((None, 2, 2, 1, PIECE_GROUPS * p, LANES), lambda i: (layer, 0, 0, i, 0, 0)),
                  pl.BlockSpec((None, 2, 2, 1, PIECE_GROUPS * n, LANES), lambda i: (layer, 0, 0, i, 0, 0))],
        out_specs=[pl.BlockSpec((1, rows_t, LANES * PIECE_GROUPS), lambda i: (i, 0, 0)),
                   pl.BlockSpec((1, rows_t, 2 * LANES * PIECE_GROUPS), lambda i: (i, 0, 0)),
                   pl.BlockSpec((1, rows_o, LANES * PIECE_GROUPS), lambda i: (i, 0, 0))],
        out_shape=[jax.ShapeDtypeStruct((npiece, rows_t, LANES * PIECE_GROUPS), BF16),
                   jax.ShapeDtypeStruct((npiece, rows_t, 2 * LANES * PIECE_GROUPS), BF16),
                   jax.ShapeDtypeStruct((npiece, rows_o, LANES * PIECE_GROUPS), BF16)],
        scratch_shapes=[pltpu.VMEM((rows_t, LANES), F32)],
        compiler_params=_cparams("parallel"),
        name="s5_expand",
    )(kcat, e_pair, b_pair, wout_src)


def s5_derive(lam_re, lam_im, log_dt, b_re, b_im, c_re, c_im):
    hi = lax.Precision.HIGHEST
    n_dir, g, n = lam_re.shape
    p = b_re.shape[-1]
    npiece = g // PIECE_GROUPS
    assert n_dir == 2 and SUB * p == LANES
    dt = jnp.exp(log_dt)[:, :, None]
    up = list(range(SUB))
    down = up[::-1]

    def powers(exps):
        k = jnp.asarray(exps, F32)[:, :, None, None]
        mag = jnp.exp(k * (lam_re * dt)[:, None])
        ang = k * (lam_im * dt)[:, None]
        return mag * jnp.cos(ang), mag * jnp.sin(ang)

    a_re, a_im = powers([[1], [1]])
    den = lam_re * lam_re + lam_im * lam_im
    nr, ni = a_re[:, 0] - 1.0, a_im[:, 0]
    cf_re = ((nr * lam_re + ni * lam_im) / den)[:, :, None, :]
    cf_im = ((ni * lam_re - nr * lam_im) / den)[:, :, None, :]
    bt_re, bt_im = jnp.swapaxes(b_re, 2, 3), jnp.swapaxes(b_im, 2, 3)
    bb_re = cf_re * bt_re - cf_im * bt_im
    bb_im = cf_re * bt_im + cf_im * bt_re
    ct_re, ct_im = jnp.swapaxes(c_re, 2, 3), jnp.swapaxes(c_im, 2, 3)

    lane = jnp.arange(SUB * p)
    spread_k = (lane[None, :] // p == jnp.arange(SUB)[:, None]).astype(F32)
    spread_p = (lane[None, :] % p == jnp.arange(p)[:, None]).astype(F32)
    cw_re = jnp.dot(ct_re, spread_p, precision=hi)
    cw_im = jnp.dot(ct_im, spread_p, precision=hi)

    def c_times_powers(exps):
        e_re, e_im = powers(exps)
        e_re = jnp.einsum('dkgn,km->dgnm', e_re, spread_k, precision=hi)
        e_im = jnp.einsum('dkgn,km->dgnm', e_im, spread_k, precision=hi)
        return cw_re * e_re - cw_im * e_im, cw_re * e_im + cw_im * e_re

    ck_re, ck_im = c_times_powers([up, down])
    lhs = jnp.concatenate([bb_re, -bb_im], axis=-1)
    rhs = jnp.concatenate([ck_re, ck_im], axis=2)
    kcat = jnp.einsum('dgqk,dgkm->dgqm', lhs, rhs, precision=hi)
    kcat = kcat.reshape(n_dir, npiece, PIECE_GROUPS, p, SUB * p)

    e_re, e_im = powers([down, up])
    e_pair = jnp.stack([jnp.concatenate([e_re, e_re], axis=-1), jnp.concatenate([-e_im, e_im], axis=-1)])
    e_pair = e_pair.reshape(2, n_dir, SUB, npiece, PIECE_GROUPS, 2 * n).transpose(0, 1, 3, 2, 4, 5)
    b_pair = jnp.stack([jnp.concatenate([bb_re, bb_im], axis=-1), jnp.concatenate([bb_im, bb_re], axis=-1)])
    b_pair = b_pair.reshape(2, n_dir, npiece, PIECE_GROUPS * p, 2 * n)

    co_re, co_im = c_times_powers([[k + 1 for k in up], [SUB - k for k in up]])
    wout_src = jnp.stack([co_re, -co_im], axis=1).reshape(n_dir, 2, npiece, PIECE_GROUPS * n, SUB * p)

    steps = [SUB * m for m in range(1, SUBLANES + 1)]
    q_re, q_im = powers([steps, steps])
    q_re = q_re.reshape(n_dir, SUBLANES, g * n)
    q_im = q_im.reshape(n_dir, SUBLANES, g * n)
    row = jnp.arange(SUBLANES)
    consts = []
    for d in range(n_dir):
        kinds = []
        for dd in (1, 2, 4):
            keep = (row + dd <= SUBLANES - 1) if d == 1 else (row >= dd)
            m = keep.astype(F32)[:, None]
            kinds.append(jnp.stack([m * q_re[d, dd - 1][None], m * q_im[d, dd - 1][None]]))
        sel = (SUBLANES - 1 - row) if d == 1 else row
        kinds.append(jnp.stack([q_re[d][sel], q_im[d][sel]]))
        consts.append(jnp.stack(kinds))
    consts = jnp.stack(consts)
    return kcat, e_pair, b_pair, wout_src, consts


def _lru_kernel(*refs, bwd, nch, tc, width):
    if bwd:
        (x_ref, xp_ref, xn_ref, cw_ref, cb_ref, lam_ref, wa_ref, ba_ref, wx_ref, bx_ref,
         hf_ref, gd_ref, o_ref, pad_scr, a_scr, b_scr, carry_scr) = refs
    else:
        (x_ref, xp_ref, xn_ref, cw_ref, cb_ref, lam_ref, wa_ref, ba_ref, wx_ref, bx_ref,
         o_ref, pad_scr, a_scr, b_scr, carry_scr) = refs
    k = pl.program_id(1)
    chunk = jnp.where(k == 0, 0, nch - k) if bwd else k
    prev_ok = jnp.logical_and(chunk != 0, chunk != 1)
    next_ok = jnp.logical_and(chunk != 0, chunk != nch - 1)
    pad_scr[0:SUBLANES] = jnp.where(prev_ok, xp_ref[0, 0], 0.0)
    pad_scr[SUBLANES:SUBLANES + tc] = x_ref[0, 0]
    pad_scr[SUBLANES + tc:2 * SUBLANES + tc] = jnp.where(next_ok, xn_ref[0, 0], 0.0)
    xc = cb_ref[...]
    for tap in range(CONV_W):
        xc = xc + cw_ref[tap:tap + 1] * pad_scr[SUBLANES - 1 + tap:SUBLANES - 1 + tap + tc]

    bw = width // D_BLOCKS
    r_parts, i_parts = [], []
    for blk in range(D_BLOCKS):
        xb = xc[:, blk * bw:(blk + 1) * bw].astype(BF16)
        r_parts.append(jnp.dot(xb, wa_ref[blk], preferred_element_type=F32))
        i_parts.append(jnp.dot(xb, wx_ref[blk], preferred_element_type=F32))
    r = _sigmoid(jnp.concatenate(r_parts, axis=1) + ba_ref[...])
    ig = _sigmoid(jnp.concatenate(i_parts, axis=1) + bx_ref[...])
    nl = -lam_ref[...]
    e = jnp.exp(-jnp.abs(nl))
    e1 = 1.0 + e
    log1p_e = jnp.where(e1 == 1.0, e, jnp.log(e1) * (e / jnp.where(e1 == 1.0, 1.0, e1 - 1.0)))
    softplus = jnp.maximum(nl, 0.0) + log1p_e
    log_a = (-LRU_C * softplus) * r
    a = jnp.exp(log_a)
    a_scr[...] = a
    b_scr[...] = jnp.sqrt(-jnp.tanh(log_a) * (1.0 + a * a)) * (ig * xc)

    @pl.when(k == 0)
    def _():
        carry_scr[...] = jnp.zeros_like(carry_scr)

    rowi = lax.broadcasted_iota(jnp.int32, (SUBLANES, width), 0)
    n_tiles = tc // SUBLANES

    def body(n, carry):
        ti = n_tiles - 1 - n if bwd else n
        rr = pl.ds(pl.multiple_of(ti * SUBLANES, SUBLANES), SUBLANES)
        a, b = a_scr[rr], b_scr[rr]
        for dd in (1, 2, 4):
            keep = (rowi + dd <= SUBLANES - 1) if bwd else (rowi >= dd)
            sh = SUBLANES - dd if bwd else dd
            b = b + a * jnp.where(keep, pltpu.roll(b, sh, 0), 0.0)
            a = a * jnp.where(keep, pltpu.roll(a, sh, 0), 1.0)
        h = b + a * carry
        b_scr[rr] = h
        return h[0:1] if bwd else h[SUBLANES - 1:SUBLANES]

    carry_scr[...] = lax.fori_loop(0, n_tiles, body, carry_scr[...])
    if bwd:
        o_ref[0] = ((hf_ref[0] + b_scr[...]) * _silu(gd_ref[0, 0])).astype(BF16)
    else:
        o_ref[0] = b_scr[...]


def lru_pass(slabs, conv_w, conv_b, lam, wa, ba, wx, bx, ctx_len, hf=None):
    _, b, t, w = slabs.shape
    bwd = hf is not None
    tc = ctx_len
    nch = t // tc
    hb = tc // SUBLANES
    nhb = t // SUBLANES

    def ch(k):
        return jnp.where(k == 0, 0, nch - k) if bwd else k

    in_specs = [pl.BlockSpec((1, 1, tc, w), lambda bb, k: (1, bb, ch(k), 0)),
                pl.BlockSpec((1, 1, SUBLANES, w), lambda bb, k: (1, bb, jnp.maximum(ch(k) * hb - 1, 0), 0)),
                pl.BlockSpec((1, 1, SUBLANES, w), lambda bb, k: (1, bb, jnp.minimum((ch(k) + 1) * hb, nhb - 1), 0)),
                pl.BlockSpec((CONV_W, w), lambda bb, k: (0, 0)),
                pl.BlockSpec((1, w), lambda bb, k: (0, 0)),
                pl.BlockSpec((1, w), lambda bb, k: (0, 0)),
                pl.BlockSpec(wa.shape, lambda bb, k: (0, 0, 0)),
                pl.BlockSpec((1, w), lambda bb, k: (0, 0)),
                pl.BlockSpec(wx.shape, lambda bb, k: (0, 0, 0)),
                pl.BlockSpec((1, w), lambda bb, k: (0, 0))]
    args = [slabs, slabs, slabs, conv_w, conv_b, lam, wa, ba, wx, bx]
    if bwd:
        in_specs += [pl.BlockSpec((1, tc, w), lambda bb, k: (bb, ch(k), 0)),
                     pl.BlockSpec((1, 1, tc, w), lambda bb, k: (3, bb, ch(k), 0))]
        args += [hf, slabs]
    return pl.pallas_call(
        functools.partial(_lru_kernel, bwd=bwd, nch=nch, tc=tc, width=w),
        grid=(b, nch),
        in_specs=in_specs,
        out_specs=pl.BlockSpec((1, tc, w), lambda bb, k: (bb, ch(k), 0)),
        out_shape=jax.ShapeDtypeStruct((b, t, w), BF16 if bwd else F32),
        scratch_shapes=[pltpu.VMEM((tc + 2 * SUBLANES, w), F32),
                        pltpu.VMEM((tc, w), F32),
                        pltpu.VMEM((tc, w), F32),
                        pltpu.VMEM((1, w), F32)],
        compiler_params=_cparams("parallel", "arbitrary"),
        name="lru_bwd" if bwd else "lru_fwd",
    )(*args)


def rope_tables(ctx_len, seq):
    rows = seq // GRID_W
    axis_dim = HEAD_DIM // 2
    row = jnp.repeat(jnp.arange(rows), GRID_W).astype(F32)
    col = jnp.tile(jnp.arange(GRID_W), rows).astype(F32)
    inv = ROPE_THETA ** (-jnp.arange(0, axis_dim, 2, dtype=F32) / axis_dim)
    ang = jnp.concatenate([row[:, None] * inv, col[:, None] * inv], axis=-1)
    cos, sin = jnp.cos(ang), jnp.sin(ang)
    cos_t = jnp.repeat(cos, 2, axis=-1)
    sin_t = jnp.stack([-sin, sin], axis=-1).reshape(seq, HEAD_DIM)
    cos_t = jnp.concatenate([jnp.ones((ctx_len, HEAD_DIM), F32), cos_t], axis=0)
    sin_t = jnp.concatenate([jnp.zeros((ctx_len, HEAD_DIM), F32), sin_t], axis=0)
    return cos_t, sin_t


def even_layer(x_all, mod, g, w_in, w_out, j, q_g, k_g, sgu_g, ws, bs, cos_t, sin_t, ctx_len, latent_only):
    shift_l, scale_l, gate_l, shift_c, scale_c, gate_c = mod
    slabs = inproj(x_all, shift_l, scale_l, shift_c, scale_c, g, w_in, j, ctx_len, tn=512,
                   tm=_pick_tile(x_all.shape[1], 1152), out_dtype=BF16)
    q_gain = (q_g * (HEAD_DIM ** -0.5 * math.log2(math.e)))[None, :]
    mix_a = attention(slabs, q_gain, k_g[None, :], cos_t, sin_t, ctx_len)
    mix_b = sgu(slabs, ws.astype(BF16), bs[:, :, None], sgu_g[None, :])
    return outproj(mix_a, mix_b, w_out, j, x_all, gate_l, gate_c, ctx_len, latent_only)


def odd_layer(x_all, mod, g, w_in, w_out, j, s5_ops, d_skip, glu_w, glu_b, conv_w, conv_b, lam, wa, ba, wx, bx,
              ctx_len, latent_only):
    shift_l, scale_l, gate_l, shift_c, scale_c, gate_c = mod
    b, t, _ = x_all.shape
    slabs = inproj(x_all, shift_l, scale_l, shift_c, scale_c, g, w_in, j, ctx_len, tn=1024,
                   tm=_pick_tile(t, 1152), out_dtype=F32)
    cw = slabs.shape[3]
    kcat, e_pair, b_pair, wout_src, consts = s5_ops
    t_intra, w_s5in, w_s5out = s5_expand(kcat, e_pair, b_pair, wout_src, j, kcat.shape[-2], C_STATE)
    slabs2 = slabs.reshape(slabs.shape[0], b * t, cw)
    y = s5_core(slabs2, w_s5in, t_intra, w_s5out, consts, j, b, ctx_len)
    mix_c = s5_finish(y, slabs2, d_skip[None, :], glu_w.astype(BF16), glu_b[None, :],
                      tm=_pick_tile(b * t, 512))
    hf = lru_pass(slabs, conv_w, conv_b[None, :], lam[0][None, :], wa[0].astype(BF16), ba[0][None, :],
                  wx[0].astype(BF16), bx[0][None, :], ctx_len)
    mix_d = lru_pass(slabs, conv_w, conv_b[None, :], lam[1][None, :], wa[1].astype(BF16), ba[1][None, :],
                     wx[1].astype(BF16), bx[1][None, :], ctx_len, hf=hf)
    return outproj(mix_c.reshape(b, t, cw), mix_d, w_out, j, x_all, gate_l, gate_c, ctx_len, latent_only)


def kernel(x, c, ctx, c_ctx, ada_w, ada_b, norm_g, ev_w_in, ev_w_out, ev_q_g, ev_k_g, ev_sgu_g, ev_ws, ev_bs,
           od_w_in, od_w_out, s5_lam_re, s5_lam_im, s5_log_dt, s5_b_re, s5_b_im, s5_c_re, s5_c_im, s5_d,
           s5_glu_w, s5_glu_b, lru_conv_w, lru_conv_b, lru_lam, lru_wa, lru_ba, lru_wx, lru_bx):
    b, seq, d = x.shape
    ctx_len = ctx.shape[1]
    depth = ada_w.shape[0]
    assert b + 1 <= SUBLANES and seq % ctx_len == 0
    cos_t, sin_t = rope_tables(ctx_len, seq)
    cc = jnp.zeros((2 * SUBLANES, d), F32).at[:b].set(c).at[b].set(c_ctx)
    mods = adaln_all(cc, ada_w, ada_b)
    x_all = jnp.concatenate([ctx, x], axis=1)
    s5_ops = jax.vmap(s5_derive)(s5_lam_re, s5_lam_im, s5_log_dt, s5_b_re, s5_b_im, s5_c_re, s5_c_im)
    ev_w_in, ev_w_out = ev_w_in.astype(BF16), ev_w_out.astype(BF16)
    od_w_in, od_w_out = od_w_in.astype(BF16), od_w_out.astype(BF16)
    for layer in range(depth):
        m = mods[layer]
        mod = tuple(m[:b, None, k * d:(k + 1) * d] for k in range(3)) + \
            tuple(m[b:b + 1, k * d:(k + 1) * d] for k in range(3))
        last = layer == depth - 1
        j = layer // 2
        g = norm_g[layer][None, :]
        if layer % 2 == 0:
            x_all = even_layer(x_all, mod, g, ev_w_in, ev_w_out, j, ev_q_g[j], ev_k_g[j], ev_sgu_g[j],
                               ev_ws[j], ev_bs[j], cos_t, sin_t, ctx_len, last)
        else:
            x_all = odd_layer(x_all, mod, g, od_w_in, od_w_out, j, s5_ops,
                              s5_d[j], s5_glu_w[j], s5_glu_b[j],
                              lru_conv_w[j], lru_conv_b[j], lru_lam[j], lru_wa[j], lru_ba[j], lru_wx[j],
                              lru_bx[j], ctx_len, last)
    return x_all
```

```python
import functools
import math

import jax
import jax.numpy as jnp
from jax import lax
from jax.experimental import pallas as pl
from jax.experimental.pallas import tpu as pltpu

F32 = jnp.float32
BF16 = jnp.bfloat16

EPS = 1e-6
GRID_W = 64
HEAD_DIM = 128
A_Q_HEADS = 12
A_KV_HEADS = 4
A_GROUP = A_Q_HEADS // A_KV_HEADS
B_GROUPS = 4
CHUNK = 128
ROPE_THETA = 10000.0
C_GROUP_DIM = 16
C_STATE = 64
D_BLOCKS = 8
CONV_W = 4
LRU_C = 8.0

LANES = 128
SUBLANES = 8
SUB = 8
PIECE_GROUPS = LANES // C_GROUP_DIM
X_RING = 3
KV_PER_STEP = 4
VMEM_LIMIT = 50 * 1024 * 1024


def _cparams(*sem):
    return pltpu.CompilerParams(dimension_semantics=sem, vmem_limit_bytes=VMEM_LIMIT)


def _sigmoid(x):
    return 0.5 * jnp.tanh(0.5 * x) + 0.5


def _silu(x):
    return x * _sigmoid(x)


def _pick_tile(rows, target, align=16):
    best = None
    for cand in range(align, min(rows, target) + 1, align):
        if rows % cand == 0:
            best = cand
    assert best is not None, (rows, target)
    return best


def _split_bf16(x, parts):
    out = []
    for _ in range(parts):
        hi = x.astype(BF16)
        out.append(hi)
        x = x - hi.astype(F32)
    return out


def _adaln_kernel(cc_ref, w_ref, b_ref, o_ref):
    s_parts = _split_bf16(_silu(cc_ref[...]), 3)
    w = w_ref[0].astype(BF16)
    acc = jnp.dot(s_parts[2], w, preferred_element_type=F32)
    acc = acc + jnp.dot(s_parts[1], w, preferred_element_type=F32)
    acc = acc + jnp.dot(s_parts[0], w, preferred_element_type=F32)
    o_ref[0] = acc + b_ref[0]


def adaln_all(cc, ada_w, ada_b):
    depth, d, n3 = ada_w.shape
    rows = cc.shape[0]
    tn = _pick_tile(n3, 1536, align=LANES)
    return pl.pallas_call(
        _adaln_kernel,
        grid=(depth, n3 // tn),
        in_specs=[pl.BlockSpec((rows, d), lambda l, j: (0, 0)),
                  pl.BlockSpec((1, d, tn), lambda l, j: (l, 0, j)),
                  pl.BlockSpec((1, 1, tn), lambda l, j: (l, 0, j))],
        out_specs=pl.BlockSpec((1, rows, tn), lambda l, j: (l, 0, j)),
        out_shape=jax.ShapeDtypeStruct((depth, rows, n3), F32),
        compiler_params=_cparams("parallel", "parallel"),
        name="adaln",
    )(cc, ada_w, ada_b.reshape(depth, 1, n3))


def _inproj_kernel(x_ref, shl_ref, scl_ref, shc_ref, scc_ref, g_ref, w_ref, o_ref, h_scr, *, tm, ctx_len):
    i = pl.program_id(1)
    first = pl.program_id(2) == 0

    def modulate(rows, sc_ref, sh_ref):
        x = x_ref[0, rows]
        r = lax.rsqrt(jnp.mean(x * x, axis=-1, keepdims=True) + EPS)
        h_scr[rows] = ((x * r) * (g_ref[...] * (1.0 + sc_ref[...])) + sh_ref[...]).astype(BF16)

    scl, shl = scl_ref.at[0], shl_ref.at[0]
    if tm >= ctx_len:
        @pl.when(jnp.logical_and(first, i == 0))
        def _():
            modulate(slice(0, ctx_len), scc_ref, shc_ref)
            if tm > ctx_len:
                modulate(slice(ctx_len, tm), scl, shl)

        @pl.when(jnp.logical_and(first, i > 0))
        def _():
            modulate(slice(0, tm), scl, shl)
    else:
        @pl.when(jnp.logical_and(first, i < ctx_len // tm))
        def _():
            modulate(slice(0, tm), scc_ref, shc_ref)

        @pl.when(jnp.logical_and(first, i >= ctx_len // tm))
        def _():
            modulate(slice(0, tm), scl, shl)

    o_ref[0, 0] = jnp.dot(h_scr[...], w_ref[...], preferred_element_type=F32).astype(o_ref.dtype)


def inproj(x_all, shift_l, scale_l, shift_c, scale_c, g, w, layer, ctx_len, tn, tm, out_dtype):
    b, t, d = x_all.shape
    n = w.shape[2]
    n_slab = n // tn
    assert tm >= ctx_len or ctx_len % tm == 0
    return pl.pallas_call(
        functools.partial(_inproj_kernel, tm=tm, ctx_len=ctx_len),
        grid=(b, t // tm, n_slab),
        in_specs=[pl.BlockSpec((1, tm, d), lambda bb, i, j: (bb, i, 0)),
                  pl.BlockSpec((1, 1, d), lambda bb, i, j: (bb, 0, 0)),
                  pl.BlockSpec((1, 1, d), lambda bb, i, j: (bb, 0, 0)),
                  pl.BlockSpec((1, d), lambda bb, i, j: (0, 0)),
                  pl.BlockSpec((1, d), lambda bb, i, j: (0, 0)),
                  pl.BlockSpec((1, d), lambda bb, i, j: (0, 0)),
                  pl.BlockSpec((None, d, tn), lambda bb, i, j: (layer, 0, j))],
        out_specs=pl.BlockSpec((1, 1, tm, tn), lambda bb, i, j: (j, bb, i, 0)),
        out_shape=jax.ShapeDtypeStruct((n_slab, b, t, tn), out_dtype),
        scratch_shapes=[pltpu.VMEM((tm, d), BF16)],
        compiler_params=_cparams("parallel", "parallel", "arbitrary"),
        name="inproj",
    )(x_all, shift_l, scale_l, shift_c, scale_c, g, w)


def _outproj_kernel(ma_ref, mb_ref, w1_ref, w2_ref, x_hbm, gl_ref, gc_ref, o_ref, x_ring, sems, *,
                    ctx_tiles, n_i, off, tm):
    i = pl.program_id(1)
    step = pl.program_id(0) * n_i + i
    total = pl.num_programs(0) * n_i

    def copy(s):
        slot = s % X_RING
        r0 = (s % n_i + off) * tm
        rows = pl.ds(r0 if isinstance(r0, int) else pl.multiple_of(r0, tm), tm)
        return pltpu.make_async_copy(x_hbm.at[s // n_i, rows, :], x_ring.at[slot], sems.at[slot])

    @pl.when(step == 0)
    def _():
        for s in range(X_RING - 1):
            @pl.when(s < total)
            def _(s=s):
                copy(s).start()

    @pl.when(step + (X_RING - 1) < total)
    def _():
        copy(step + (X_RING - 1)).start()

    acc = jnp.dot(ma_ref[0], w1_ref[...], preferred_element_type=F32)
    acc = acc + jnp.dot(mb_ref[0], w2_ref[...], preferred_element_type=F32)
    gate = jnp.where(i < ctx_tiles, gc_ref[...], gl_ref[0])
    copy(step).wait()
    o_ref[0] = x_ring[step % X_RING] + gate * acc


def outproj(ma, mb, w, layer, x_all, gate_l, gate_c, ctx_len, latent_only):
    b, t, d = x_all.shape
    k1, k2 = ma.shape[2], mb.shape[2]
    assert w.shape[1] == k1 + k2 and k1 % k2 == 0
    tm = ctx_len
    off = 1 if latent_only else 0
    t_out = t - ctx_len if latent_only else t
    n_i = t_out // tm
    return pl.pallas_call(
        functools.partial(_outproj_kernel, ctx_tiles=0 if latent_only else 1, n_i=n_i, off=off, tm=tm),
        grid=(b, n_i),
        in_specs=[pl.BlockSpec((1, tm, k1), lambda bb, i: (bb, i + off, 0)),
                  pl.BlockSpec((1, tm, k2), lambda bb, i: (bb, i + off, 0)),
                  pl.BlockSpec((None, k1, d), lambda bb, i: (layer, 0, 0)),
                  pl.BlockSpec((None, k2, d), lambda bb, i: (layer, k1 // k2, 0)),
                  pl.BlockSpec(memory_space=pl.ANY),
                  pl.BlockSpec((1, 1, d), lambda bb, i: (bb, 0, 0)),
                  pl.BlockSpec((1, d), lambda bb, i: (0, 0))],
        out_specs=pl.BlockSpec((1, tm, d), lambda bb, i: (bb, i, 0)),
        out_shape=jax.ShapeDtypeStruct((b, t_out, d), F32),
        scratch_shapes=[pltpu.VMEM((X_RING, tm, d), F32), pltpu.SemaphoreType.DMA((X_RING,))],
        compiler_params=_cparams("arbitrary", "arbitrary"),
        name="outproj",
    )(ma, mb, w, w, x_all, gate_l, gate_c)


def _norm_rope(x, gain, cos, sin):
    y = x * lax.rsqrt(jnp.mean(x * x, axis=-1, keepdims=True) + EPS) * gain
    lane = lax.broadcasted_iota(jnp.int32, y.shape, 1)
    partner = jnp.where((lane & 1) == 0, pltpu.roll(y, HEAD_DIM - 1, 1), pltpu.roll(y, 1, 1))
    return y * cos + partner * sin


def _attn_kernel(*refs, ctx_len, tq):
    nq = KV_PER_STEP * A_GROUP
    q_refs, (k_ref, v_ref), ga_refs = refs[:nq], refs[nq:nq + 2], refs[nq + 2:2 * nq + 2]
    qg_ref, kg_ref, cos_ref, sin_ref, o_ref, k_scr, vaug_scr = refs[2 * nq + 2:]
    iq = pl.program_id(2)

    @pl.when(iq == 0)
    def _():
        for j in range(KV_PER_STEP):
            cols = slice(j * HEAD_DIM, (j + 1) * HEAD_DIM)
            k_scr[j] = _norm_rope(k_ref[0, 0, :, cols].astype(F32), kg_ref[...], cos_ref[...],
                                  sin_ref[...]).astype(BF16)
            vaug_scr[j, :, :HEAD_DIM] = v_ref[0, 0, :, cols].astype(BF16)
            vaug_scr[j, :, HEAD_DIM:] = jnp.ones((vaug_scr.shape[1], HEAD_DIM), BF16)

    def attend(n_keys):
        rows = pl.ds(pl.multiple_of(iq * tq, tq), tq)
        cos_q, sin_q = cos_ref[rows, :], sin_ref[rows, :]
        for h, (q_ref, ga_ref) in enumerate(zip(q_refs, ga_refs)):
            j = h // A_GROUP
            q = _norm_rope(q_ref[0, 0].astype(F32), qg_ref[...], cos_q, sin_q).astype(BF16)
            s = lax.dot_general(q, k_scr[j, :n_keys], (((1,), (1,)), ((), ())), preferred_element_type=F32)
            p = jnp.exp2(s - jnp.max(s, axis=-1, keepdims=True)).astype(BF16)
            pv = jnp.dot(p, vaug_scr[j, :n_keys], preferred_element_type=F32)
            o = pv[:, :HEAD_DIM] / pv[:, HEAD_DIM:]
            o_ref[0, :, h * HEAD_DIM:(h + 1) * HEAD_DIM] = (o * _silu(ga_ref[0, 0].astype(F32))).astype(BF16)

    @pl.when(iq == 0)
    def _():
        attend(ctx_len)

    @pl.when(iq > 0)
    def _():
        attend(k_scr.shape[1])


def attention(slabs, q_gain, k_gain, cos_t, sin_t, ctx_len):
    _, b, t, tn = slabs.shape
    per = tn // HEAD_DIM
    tq = ctx_len
    nq = KV_PER_STEP * A_GROUP
    kvw = KV_PER_STEP * HEAD_DIM
    assert A_KV_HEADS % KV_PER_STEP == 0 and tn % kvw == 0

    def head_spec(first_slab, idx):
        def head_map(bb, kv, iq):
            h = kv * nq + idx
            return (first_slab + h // per, bb, iq, h % per)
        return pl.BlockSpec((1, 1, tq, HEAD_DIM), head_map)

    vec = pl.BlockSpec((1, HEAD_DIM), lambda bb, kv, iq: (0, 0))
    table = pl.BlockSpec((t, HEAD_DIM), lambda bb, kv, iq: (0, 0))
    return pl.pallas_call(
        functools.partial(_attn_kernel, ctx_len=ctx_len, tq=tq),
        grid=(b, A_KV_HEADS // KV_PER_STEP, t // tq),
        in_specs=[head_spec(2, idx) for idx in range(nq)]
        + [pl.BlockSpec((1, 1, t, kvw), lambda bb, kv, iq: (0, bb, 0, kv)),
           pl.BlockSpec((1, 1, t, kvw), lambda bb, kv, iq: (1, bb, 0, kv))]
        + [head_spec(7, idx) for idx in range(nq)]
        + [vec, vec, table, table],
        out_specs=pl.BlockSpec((1, tq, nq * HEAD_DIM), lambda bb, kv, iq: (bb, iq, kv)),
        out_shape=jax.ShapeDtypeStruct((b, t, A_Q_HEADS * HEAD_DIM), BF16),
        scratch_shapes=[pltpu.VMEM((KV_PER_STEP, t, HEAD_DIM), BF16),
                        pltpu.VMEM((KV_PER_STEP, t, 2 * HEAD_DIM), BF16)],
        compiler_params=_cparams("parallel", "parallel", "arbitrary"),
        name="attention",
    )(*([slabs] * (2 * nq + 2)), q_gain, k_gain, cos_t, sin_t)


def _sgu_kernel(u_ref, v_ref, gb_ref, ws_ref, bs_ref, g_ref, o_ref, *, n_chunks):
    def chunk(c, carry):
        rows = pl.ds(pl.multiple_of(c * CHUNK, CHUNK), CHUNK)
        outs = []
        for grp in range(B_GROUPS):
            cols = slice(grp * LANES, (grp + 1) * LANES)
            v = v_ref[0, 0, rows, cols].astype(F32)
            vn = v * lax.rsqrt(jnp.mean(v * v, axis=-1, keepdims=True) + EPS) * g_ref[:, cols]
            mixed = jnp.dot(ws_ref[grp], vn.astype(BF16), preferred_element_type=F32) + bs_ref[grp]
            outs.append(u_ref[0, 0, rows, cols].astype(F32) * mixed
                        * _silu(gb_ref[0, 0, rows, cols].astype(F32)))
        o_ref[0, rows, :] = jnp.concatenate(outs, axis=1).astype(BF16)
        return carry

    lax.fori_loop(0, n_chunks, chunk, 0)


def sgu(slabs, ws, bs, g):
    _, b, t, tn = slabs.shape
    spec = lambda s: pl.BlockSpec((1, 1, t, tn), lambda bb: (s, bb, 0, 0))
    return pl.pallas_call(
        functools.partial(_sgu_kernel, n_chunks=t // CHUNK),
        grid=(b,),
        in_specs=[spec(5), spec(6), spec(10),
                  pl.BlockSpec((B_GROUPS, CHUNK, CHUNK), lambda bb: (0, 0, 0)),
                  pl.BlockSpec((B_GROUPS, CHUNK, 1), lambda bb: (0, 0, 0)),
                  pl.BlockSpec((1, tn), lambda bb: (0, 0))],
        out_specs=pl.BlockSpec((1, t, tn), lambda bb: (bb, 0, 0)),
        out_shape=jax.ShapeDtypeStruct((b, t, tn), BF16),
        compiler_params=_cparams("parallel"),
        name="sgu",
    )(slabs, slabs, slabs, ws, bs, g)


def _s5_fold_rows(u_ref, tm):
    return jnp.concatenate([u_ref[pl.ds(t, tm, stride=SUB), :] for t in range(SUB)], axis=1)


def _s5_scan(g_ref, c_ref, o_ref, *, ctx_pairs, all_pairs, half):
    rows = 2 * SUBLANES
    rowi = lax.broadcasted_iota(jnp.int32, (SUBLANES, half), 0)

    def tile(d, g_re, g_im, c_re, c_im):
        bwd = d == 1
        b_re, b_im = g_re, g_im
        for idx, dd in enumerate((1, 2, 4)):
            a_re, a_im = c_ref[d, idx, 0], c_ref[d, idx, 1]
            sh = SUBLANES - dd if bwd else dd
            s_re, s_im = pltpu.roll(b_re, sh, 0), pltpu.roll(b_im, sh, 0)
            b_re, b_im = (b_re + a_re * s_re - a_im * s_im,
                          b_im + a_re * s_im + a_im * s_re)
        a_re, a_im = c_ref[d, 3, 0], c_ref[d, 3, 1]
        h_re = b_re + a_re * c_re - a_im * c_im
        h_im = b_im + a_re * c_im + a_im * c_re
        if bwd:
            p_re = jnp.where(rowi == SUBLANES - 1, c_re, pltpu.roll(h_re, SUBLANES - 1, 0))
            p_im = jnp.where(rowi == SUBLANES - 1, c_im, pltpu.roll(h_im, SUBLANES - 1, 0))
            return p_re, p_im, h_re[0:1], h_im[0:1]
        p_re = jnp.where(rowi == 0, c_re, pltpu.roll(h_re, 1, 0))
        p_im = jnp.where(rowi == 0, c_im, pltpu.roll(h_im, 1, 0))
        return p_re, p_im, h_re[SUBLANES - 1:SUBLANES], h_im[SUBLANES - 1:SUBLANES]

    def pair(d, pi, carry):
        c_re, c_im = carry
        r0 = pl.multiple_of(pi * rows, rows)
        base = d * 2 * half
        order = (1, 0) if d == 1 else (0, 1)
        res = [None, None]
        for which in order:
            rr = pl.ds(r0 + which * SUBLANES, SUBLANES)
            g_re = g_ref[rr, pl.ds(base, half)]
            g_im = g_ref[rr, pl.ds(base + half, half)]
            p_re, p_im, c_re, c_im = tile(d, g_re, g_im, c_re, c_im)
            res[which] = (p_re, p_im)
        o_ref[pl.ds(r0, rows), pl.ds(base, half)] = jnp.concatenate(
            [res[0][0], res[1][0]], axis=0).astype(BF16)
        o_ref[pl.ds(r0, rows), pl.ds(base + half, half)] = jnp.concatenate(
            [res[0][1], res[1][1]], axis=0).astype(BF16)
        return c_re, c_im

    zero = (jnp.zeros((1, half), F32), jnp.zeros((1, half), F32))
    lax.fori_loop(0, all_pairs, lambda pi, c: pair(0, pi, c), zero)
    c = lax.fori_loop(0, ctx_pairs, lambda n, c: pair(1, ctx_pairs - 1 - n, c), zero)
    lax.fori_loop(0, all_pairs - ctx_pairs, lambda n, c: pair(1, all_pairs - 1 - n, c), c)


def _s5_core_kernel(u_ref, win_ref, t_ref, wout_ref, c_ref, y_ref, g_scr, hp_scr, *, tm, ctx_pairs, half):
    x = _s5_fold_rows(u_ref, tm).astype(BF16)
    g_scr[...] = jnp.dot(x, win_ref[0], preferred_element_type=F32)
    _s5_scan(g_scr, c_ref, hp_scr, ctx_pairs=ctx_pairs, all_pairs=tm // (2 * SUBLANES), half=half)
    y = jnp.dot(x, t_ref[0], preferred_element_type=F32)
    y = y + jnp.dot(hp_scr[...], wout_ref[0], preferred_element_type=F32)
    for t in range(SUB):
        y_ref[pl.ds(t, tm, stride=SUB), :] = y[:, t * LANES:(t + 1) * LANES]


def s5_core(slabs2, w_in, t_intra, w_out, consts, layer, batch, ctx_len):
    _, r, width = slabs2.shape
    t = r // batch
    tm = t // SUB
    n_piece = t_intra.shape[0]
    half = PIECE_GROUPS * C_STATE
    assert w_in.shape[2] == 4 * half and tm % (2 * SUBLANES) == 0 and ctx_len % (2 * SUBLANES * SUB) == 0
    return pl.pallas_call(
        functools.partial(_s5_core_kernel, tm=tm, ctx_pairs=ctx_len // SUB // (2 * SUBLANES), half=half),
        grid=(n_piece, batch),
        in_specs=[pl.BlockSpec((None, t, LANES), lambda i, bb: (0, bb, i)),
                  pl.BlockSpec((1,) + w_in.shape[1:], lambda i, bb: (i, 0, 0)),
                  pl.BlockSpec((1,) + t_intra.shape[1:], lambda i, bb: (i, 0, 0)),
                  pl.BlockSpec((1,) + w_out.shape[1:], lambda i, bb: (i, 0, 0)),
                  pl.BlockSpec((None, 2, 4, 2, SUBLANES, half), lambda i, bb: (layer, 0, 0, 0, 0, i))],
        out_specs=pl.BlockSpec((t, LANES), lambda i, bb: (bb, i)),
        out_shape=jax.ShapeDtypeStruct((r, width), F32),
        scratch_shapes=[pltpu.VMEM((tm, 4 * half), F32), pltpu.VMEM((tm, 4 * half), BF16)],
        compiler_params=_cparams("parallel", "parallel"),
        name="s5_core",
    )(slabs2, w_in, t_intra, w_out, consts)


def _s5_finish_kernel(y_ref, u_ref, gc_ref, d_ref, w_ref, b_ref, o_ref):
    y = y_ref[...] + d_ref[...] * u_ref[0]
    y = jax.nn.gelu(y)
    z = jnp.dot(y.astype(BF16), w_ref[...], preferred_element_type=F32) + b_ref[...]
    o_ref[...] = (y * _sigmoid(z) * _silu(gc_ref[0])).astype(BF16)


def s5_finish(y, slabs2, d_skip, glu_w, glu_b, tm):
    r, w = y.shape
    return pl.pallas_call(
        _s5_finish_kernel,
        grid=(r // tm,),
        in_specs=[pl.BlockSpec((tm, w), lambda i: (i, 0)),
                  pl.BlockSpec((1, tm, w), lambda i: (0, i, 0)),
                  pl.BlockSpec((1, tm, w), lambda i: (2, i, 0)),
                  pl.BlockSpec((1, w), lambda i: (0, 0)),
                  pl.BlockSpec((w, w), lambda i: (0, 0)),
                  pl.BlockSpec((1, w), lambda i: (0, 0))],
        out_specs=pl.BlockSpec((tm, w), lambda i: (i, 0)),
        out_shape=jax.ShapeDtypeStruct((r, w), BF16),
        compiler_params=_cparams("parallel"),
        name="s5_finish",
    )(y, slabs2, slabs2, d_skip, glu_w, glu_b)


def _expand_block_diag(src, unit, row_unit):
    rows, cols = src.shape
    wide = cols * PIECE_GROUPS
    lg = lambda v: int(math.log2(v))
    k = lax.broadcasted_iota(jnp.int32, (cols, wide), 0)
    j = lax.broadcasted_iota(jnp.int32, (cols, wide), 1)
    src_col = ((j >> lg(PIECE_GROUPS * unit)) << lg(unit)) + (j & (unit - 1))
    spread = jnp.where(k == src_col, 1.0, 0.0).astype(BF16)
    out = jnp.dot(src.astype(BF16), spread, preferred_element_type=F32)
    rg = (lax.broadcasted_iota(jnp.int32, (rows, wide), 0) >> lg(row_unit)) & (PIECE_GROUPS - 1)
    ch = (lax.broadcasted_iota(jnp.int32, (rows, wide), 1) >> lg(unit)) & (PIECE_GROUPS - 1)
    return jnp.where(rg == ch, out, 0.0).astype(BF16)


def _s5_expand_kernel(kc_ref, ep_ref, bp_ref, wo_ref, t_ref, win_ref, wout_ref, tsrc_scr, *, p, n):
    lane = lax.broadcasted_iota(jnp.int32, (p, LANES), 1)
    for gl in range(PIECE_GROUPS):
        kf = kc_ref[0, 0, gl]
        kb = kc_ref[1, 0, gl]
        for s in range(SUB):
            shift_b = (LANES - p * (SUB - 1 - s)) % LANES
            f = kf if s == 0 else pltpu.roll(kf, p * s, 1)
            r = kb if shift_b == 0 else pltpu.roll(kb, shift_b, 1)
            blk = jnp.where(lane >= p * s, f, 0.0) + jnp.where(lane < p * (s + 1), r, 0.0)
            tsrc_scr[pl.ds((s * PIECE_GROUPS + gl) * p, p), :] = blk
    t_ref[0] = _expand_block_diag(tsrc_scr[...], p, p)
    half = win_ref.shape[2] // 2
    for d in range(2):
        for s in range(SUB):
            for gl in range(PIECE_GROUPS):
                rows = slice(gl * p, (gl + 1) * p)
                blk = (ep_ref[0, d, 0, s, gl:gl + 1, :] * bp_ref[0, d, 0, rows, :]
                       + ep_ref[1, d, 0, s, gl:gl + 1, :] * bp_ref[1, d, 0, rows, :])
                tsrc_scr[pl.ds((s * PIECE_GROUPS + gl) * p, p), :] = blk
        win_ref[0, :, d * half:(d + 1) * half] = _expand_block_diag(tsrc_scr[...], n, p)
    wout_ref[0] = _expand_block_diag(wo_ref[...].reshape(wout_ref.shape[1], LANES), p, n)


def s5_expand(kcat, e_pair, b_pair, wout_src, layer, p, n):
    npiece = kcat.shape[2]
    rows_t = SUB * PIECE_GROUPS * p
    rows_o = 2 * 2 * PIECE_GROUPS * n
    return pl.pallas_call(
        functools.partial(_s5_expand_kernel, p=p, n=n),
        grid=(npiece,),
        in_specs=[pl.BlockSpec((None, 2, 1, PIECE_GROUPS, p, LANES), lambda i: (layer, 0, i, 0, 0, 0)),
                  pl.BlockSpec((None, 2, 2, 1, SUB, PIECE_GROUPS, LANES), lambda i: (layer, 0, 0, i, 0, 0, 0)),
                  pl.BlockSpec((None, 2, 2, 1, PIECE_GROUPS * p, LANES), lambda i: (layer, 0, 0, i, 0, 0)),
                  pl.BlockSpec((None, 2, 2, 1, PIECE_GROUPS * n, LANES), lambda i: (layer, 0, 0, i, 0, 0))],
        out_specs=[pl.BlockSpec((1, rows_t, LANES * PIECE_GROUPS), lambda i: (i, 0, 0)),
                   pl.BlockSpec((1, rows_t, 2 * LANES * PIECE_GROUPS), lambda i: (i, 0, 0)),
                   pl.BlockSpec((1, rows_o, LANES * PIECE_GROUPS), lambda i: (i, 0, 0))],
        out_shape=[jax.ShapeDtypeStruct((npiece, rows_t, LANES * PIECE_GROUPS), BF16),
                   jax.ShapeDtypeStruct((npiece, rows_t, 2 * LANES * PIECE_GROUPS), BF16),
                   jax.ShapeDtypeStruct((npiece, rows_o, LANES * PIECE_GROUPS), BF16)],
        scratch_shapes=[pltpu.VMEM((rows_t, LANES), F32)],
        compiler_params=_cparams("parallel"),
        name="s5_expand",
    )(kcat, e_pair, b_pair, wout_src)


def s5_derive(lam_re, lam_im, log_dt, b_re, b_im, c_re, c_im):
    hi = lax.Precision.HIGHEST
    n_dir, g, n = lam_re.shape
    p = b_re.shape[-1]
    npiece = g // PIECE_GROUPS
    assert n_dir == 2 and SUB * p == LANES
    dt = jnp.exp(log_dt)[:, :, None]
    up = list(range(SUB))
    down = up[::-1]

    def powers(exps):
        k = jnp.asarray(exps, F32)[:, :, None, None]
        mag = jnp.exp(k * (lam_re * dt)[:, None])
        ang = k * (lam_im * dt)[:, None]
        return mag * jnp.cos(ang), mag * jnp.sin(ang)

    a_re, a_im = powers([[1], [1]])
    den = lam_re * lam_re + lam_im * lam_im
    nr, ni = a_re[:, 0] - 1.0, a_im[:, 0]
    cf_re = ((nr * lam_re + ni * lam_im) / den)[:, :, None, :]
    cf_im = ((ni * lam_re - nr * lam_im) / den)[:, :, None, :]
    bt_re, bt_im = jnp.swapaxes(b_re, 2, 3), jnp.swapaxes(b_im, 2, 3)
    bb_re = cf_re * bt_re - cf_im * bt_im
    bb_im = cf_re * bt_im + cf_im * bt_re
    ct_re, ct_im = jnp.swapaxes(c_re, 2, 3), jnp.swapaxes(c_im, 2, 3)

    lane = jnp.arange(SUB * p)
    spread_k = (lane[None, :] // p == jnp.arange(SUB)[:, None]).astype(F32)
    spread_p = (lane[None, :] % p == jnp.arange(p)[:, None]).astype(F32)
    cw_re = jnp.dot(ct_re, spread_p, precision=hi)
    cw_im = jnp.dot(ct_im, spread_p, precision=hi)

    def c_times_powers(exps):
        e_re, e_im = powers(exps)
        e_re = jnp.einsum('dkgn,km->dgnm', e_re, spread_k, precision=hi)
        e_im = jnp.einsum('dkgn,km->dgnm', e_im, spread_k, precision=hi)
        return cw_re * e_re - cw_im * e_im, cw_re * e_im + cw_im * e_re

    ck_re, ck_im = c_times_powers([up, down])
    lhs = jnp.concatenate([bb_re, -bb_im], axis=-1)
    rhs = jnp.concatenate([ck_re, ck_im], axis=2)
    kcat = jnp.einsum('dgqk,dgkm->dgqm', lhs, rhs, precision=hi)
    kcat = kcat.reshape(n_dir, npiece, PIECE_GROUPS, p, SUB * p)

    e_re, e_im = powers([down, up])
    e_pair = jnp.stack([jnp.concatenate([e_re, e_re], axis=-1), jnp.concatenate([-e_im, e_im], axis=-1)])
    e_pair = e_pair.reshape(2, n_dir, SUB, npiece, PIECE_GROUPS, 2 * n).transpose(0, 1, 3, 2, 4, 5)
    b_pair = jnp.stack([jnp.concatenate([bb_re, bb_im], axis=-1), jnp.concatenate([bb_im, bb_re], axis=-1)])
    b_pair = b_pair.reshape(2, n_dir, npiece, PIECE_GROUPS * p, 2 * n)

    co_re, co_im = c_times_powers([[k + 1 for k in up], [SUB - k for k in up]])
    wout_src = jnp.stack([co_re, -co_im], axis=1).reshape(n_dir, 2, npiece, PIECE_GROUPS * n, SUB * p)

    steps = [SUB * m for m in range(1, SUBLANES + 1)]
    q_re, q_im = powers([steps, steps])
    q_re = q_re.reshape(n_dir, SUBLANES, g * n)
    q_im = q_im.reshape(n_dir, SUBLANES, g * n)
    row = jnp.arange(SUBLANES)
    consts = []
    for d in range(n_dir):
        kinds = []
        for dd in (1, 2, 4):
            keep = (row + dd <= SUBLANES - 1) if d == 1 else (row >= dd)
            m = keep.astype(F32)[:, None]
            kinds.append(jnp.stack([m * q_re[d, dd - 1][None], m * q_im[d, dd - 1][None]]))
        sel = (SUBLANES - 1 - row) if d == 1 else row
        kinds.append(jnp.stack([q_re[d][sel], q_im[d][sel]]))
        consts.append(jnp.stack(kinds))
    consts = jnp.stack(consts)
    return kcat, e_pair, b_pair, wout_src, consts


def _lru_kernel(*refs, bwd, nch, tc, width):
    if bwd:
        (x_ref, xp_ref, xn_ref, cw_ref, cb_ref, lam_ref, wa_ref, ba_ref, wx_ref, bx_ref,
         hf_ref, gd_ref, o_ref, pad_scr, a_scr, b_scr, carry_scr) = refs
    else:
        (x_ref, xp_ref, xn_ref, cw_ref, cb_ref, lam_ref, wa_ref, ba_ref, wx_ref, bx_ref,
         o_ref, pad_scr, a_scr, b_scr, carry_scr) = refs
    k = pl.program_id(1)
    chunk = jnp.where(k == 0, 0, nch - k) if bwd else k
    prev_ok = jnp.logical_and(chunk != 0, chunk != 1)
    next_ok = jnp.logical_and(chunk != 0, chunk != nch - 1)
    pad_scr[0:SUBLANES] = jnp.where(prev_ok, xp_ref[0, 0], 0.0)
    pad_scr[SUBLANES:SUBLANES + tc] = x_ref[0, 0]
    pad_scr[SUBLANES + tc:2 * SUBLANES + tc] = jnp.where(next_ok, xn_ref[0, 0], 0.0)
    xc = cb_ref[...]
    for tap in range(CONV_W):
        xc = xc + cw_ref[tap:tap + 1] * pad_scr[SUBLANES - 1 + tap:SUBLANES - 1 + tap + tc]

    bw = width // D_BLOCKS
    r_parts, i_parts = [], []
    for blk in range(D_BLOCKS):
        xb = xc[:, blk * bw:(blk + 1) * bw].astype(BF16)
        r_parts.append(jnp.dot(xb, wa_ref[blk], preferred_element_type=F32))
        i_parts.append(jnp.dot(xb, wx_ref[blk], preferred_element_type=F32))
    r = _sigmoid(jnp.concatenate(r_parts, axis=1) + ba_ref[...])
    ig = _sigmoid(jnp.concatenate(i_parts, axis=1) + bx_ref[...])
    nl = -lam_ref[...]
    e = jnp.exp(-jnp.abs(nl))
    e1 = 1.0 + e
    log1p_e = jnp.where(e1 == 1.0, e, jnp.log(e1) * (e / jnp.where(e1 == 1.0, 1.0, e1 - 1.0)))
    softplus = jnp.maximum(nl, 0.0) + log1p_e
    log_a = (-LRU_C * softplus) * r
    a = jnp.exp(log_a)
    a_scr[...] = a
    b_scr[...] = jnp.sqrt(-jnp.tanh(log_a) * (1.0 + a * a)) * (ig * xc)

    @pl.when(k == 0)
    def _():
        carry_scr[...] = jnp.zeros_like(carry_scr)

    rowi = lax.broadcasted_iota(jnp.int32, (SUBLANES, width), 0)
    n_tiles = tc // SUBLANES

    def body(n, carry):
        ti = n_tiles - 1 - n if bwd else n
        rr = pl.ds(pl.multiple_of(ti * SUBLANES, SUBLANES), SUBLANES)
        a, b = a_scr[rr], b_scr[rr]
        for dd in (1, 2, 4):
            keep = (rowi + dd <= SUBLANES - 1) if bwd else (rowi >= dd)
            sh = SUBLANES - dd if bwd else dd
            b = b + a * jnp.where(keep, pltpu.roll(b, sh, 0), 0.0)
            a = a * jnp.where(keep, pltpu.roll(a, sh, 0), 1.0)
        h = b + a * carry
        b_scr[rr] = h
        return h[0:1] if bwd else h[SUBLANES - 1:SUBLANES]

    carry_scr[...] = lax.fori_loop(0, n_tiles, body, carry_scr[...])
    if bwd:
        o_ref[0] = ((hf_ref[0] + b_scr[...]) * _silu(gd_ref[0, 0])).astype(BF16)
    else:
        o_ref[0] = b_scr[...]


def lru_pass(slabs, conv_w, conv_b, lam, wa, ba, wx, bx, ctx_len, hf=None):
    _, b, t, w = slabs.shape
    bwd = hf is not None
    tc = ctx_len
    nch = t // tc
    hb = tc // SUBLANES
    nhb = t // SUBLANES

    def ch(k):
        return jnp.where(k == 0, 0, nch - k) if bwd else k

    in_specs = [pl.BlockSpec((1, 1, tc, w), lambda bb, k: (1, bb, ch(k), 0)),
                pl.BlockSpec((1, 1, SUBLANES, w), lambda bb, k: (1, bb, jnp.maximum(ch(k) * hb - 1, 0), 0)),
                pl.BlockSpec((1, 1, SUBLANES, w), lambda bb, k: (1, bb, jnp.minimum((ch(k) + 1) * hb, nhb - 1), 0)),
                pl.BlockSpec((CONV_W, w), lambda bb, k: (0, 0)),
                pl.BlockSpec((1, w), lambda bb, k: (0, 0)),
                pl.BlockSpec((1, w), lambda bb, k: (0, 0)),
                pl.BlockSpec(wa.shape, lambda bb, k: (0, 0, 0)),
                pl.BlockSpec((1, w), lambda bb, k: (0, 0)),
                pl.BlockSpec(wx.shape, lambda bb, k: (0, 0, 0)),
                pl.BlockSpec((1, w), lambda bb, k: (0, 0))]
    args = [slabs, slabs, slabs, conv_w, conv_b, lam, wa, ba, wx, bx]
    if bwd:
        in_specs += [pl.BlockSpec((1, tc, w), lambda bb, k: (bb, ch(k), 0)),
                     pl.BlockSpec((1, 1, tc, w), lambda bb, k: (3, bb, ch(k), 0))]
        args += [hf, slabs]
    return pl.pallas_call(
        functools.partial(_lru_kernel, bwd=bwd, nch=nch, tc=tc, width=w),
        grid=(b, nch),
        in_specs=in_specs,
        out_specs=pl.BlockSpec((1, tc, w), lambda bb, k: (bb, ch(k), 0)),
        out_shape=jax.ShapeDtypeStruct((b, t, w), BF16 if bwd else F32),
        scratch_shapes=[pltpu.VMEM((tc + 2 * SUBLANES, w), F32),
                        pltpu.VMEM((tc, w), F32),
                        pltpu.VMEM((tc, w), F32),
                        pltpu.VMEM((1, w), F32)],
        compiler_params=_cparams("parallel", "arbitrary"),
        name="lru_bwd" if bwd else "lru_fwd",
    )(*args)


def rope_tables(ctx_len, seq):
    rows = seq // GRID_W
    axis_dim = HEAD_DIM // 2
    row = jnp.repeat(jnp.arange(rows), GRID_W).astype(F32)
    col = jnp.tile(jnp.arange(GRID_W), rows).astype(F32)
    inv = ROPE_THETA ** (-jnp.arange(0, axis_dim, 2, dtype=F32) / axis_dim)
    ang = jnp.concatenate([row[:, None] * inv, col[:, None] * inv], axis=-1)
    cos, sin = jnp.cos(ang), jnp.sin(ang)
    cos_t = jnp.repeat(cos, 2, axis=-1)
    sin_t = jnp.stack([-sin, sin], axis=-1).reshape(seq, HEAD_DIM)
    cos_t = jnp.concatenate([jnp.ones((ctx_len, HEAD_DIM), F32), cos_t], axis=0)
    sin_t = jnp.concatenate([jnp.zeros((ctx_len, HEAD_DIM), F32), sin_t], axis=0)
    return cos_t, sin_t


def even_layer(x_all, mod, g, w_in, w_out, j, q_g, k_g, sgu_g, ws, bs, cos_t, sin_t, ctx_len, latent_only):
    shift_l, scale_l, gate_l, shift_c, scale_c, gate_c = mod
    slabs = inproj(x_all, shift_l, scale_l, shift_c, scale_c, g, w_in, j, ctx_len, tn=512,
                   tm=_pick_tile(x_all.shape[1], 1152), out_dtype=BF16)
    q_gain = (q_g * (HEAD_DIM ** -0.5 * math.log2(math.e)))[None, :]
    mix_a = attention(slabs, q_gain, k_g[None, :], cos_t, sin_t, ctx_len)
    mix_b = sgu(slabs, ws.astype(BF16), bs[:, :, None], sgu_g[None, :])
    return outproj(mix_a, mix_b, w_out, j, x_all, gate_l, gate_c, ctx_len, latent_only)


def odd_layer(x_all, mod, g, w_in, w_out, j, s5_ops, d_skip, glu_w, glu_b, conv_w, conv_b, lam, wa, ba, wx, bx,
              ctx_len, latent_only):
    shift_l, scale_l, gate_l, shift_c, scale_c, gate_c = mod
    b, t, _ = x_all.shape
    slabs = inproj(x_all, shift_l, scale_l, shift_c, scale_c, g, w_in, j, ctx_len, tn=1024,
                   tm=_pick_tile(t, 1152), out_dtype=F32)
    cw = slabs.shape[3]
    kcat, e_pair, b_pair, wout_src, consts = s5_ops
    t_intra, w_s5in, w_s5out = s5_expand(kcat, e_pair, b_pair, wout_src, j, kcat.shape[-2], C_STATE)
    slabs2 = slabs.reshape(slabs.shape[0], b * t, cw)
    y = s5_core(slabs2, w_s5in, t_intra, w_s5out, consts, j, b, ctx_len)
    mix_c = s5_finish(y, slabs2, d_skip[None, :], glu_w.astype(BF16), glu_b[None, :],
                      tm=_pick_tile(b * t, 512))
    hf = lru_pass(slabs, conv_w, conv_b[None, :], lam[0][None, :], wa[0].astype(BF16), ba[0][None, :],
                  wx[0].astype(BF16), bx[0][None, :], ctx_len)
    mix_d = lru_pass(slabs, conv_w, conv_b[None, :], lam[1][None, :], wa[1].astype(BF16), ba[1][None, :],
                     wx[1].astype(BF16), bx[1][None, :], ctx_len, hf=hf)
    return outproj(mix_c.reshape(b, t, cw), mix_d, w_out, j, x_all, gate_l, gate_c, ctx_len, latent_only)


def kernel(x, c, ctx, c_ctx, ada_w, ada_b, norm_g, ev_w_in, ev_w_out, ev_q_g, ev_k_g, ev_sgu_g, ev_ws, ev_bs,
           od_w_in, od_w_out, s5_lam_re, s5_lam_im, s5_log_dt, s5_b_re, s5_b_im, s5_c_re, s5_c_im, s5_d,
           s5_glu_w, s5_glu_b, lru_conv_w, lru_conv_b, lru_lam, lru_wa, lru_ba, lru_wx, lru_bx):
    b, seq, d = x.shape
    ctx_len = ctx.shape[1]
    depth = ada_w.shape[0]
    assert b + 1 <= SUBLANES and seq % ctx_len == 0
    cos_t, sin_t = rope_tables(ctx_len, seq)
    cc = jnp.zeros((2 * SUBLANES, d), F32).at[:b].set(c).at[b].set(c_ctx)
    mods = adaln_all(cc, ada_w, ada_b)
    x_all = jnp.concatenate([ctx, x], axis=1)
    s5_ops = jax.vmap(s5_derive)(s5_lam_re, s5_lam_im, s5_log_dt, s5_b_re, s5_b_im, s5_c_re, s5_c_im)
    ev_w_in, ev_w_out = ev_w_in.astype(BF16), ev_w_out.astype(BF16)
    od_w_in, od_w_out = od_w_in.astype(BF16), od_w_out.astype(BF16)
    for layer in range(depth):
        m = mods[layer]
        mod = tuple(m[:b, None, k * d:(k + 1) * d] for k in range(3)) + \
            tuple(m[b:b + 1, k * d:(k + 1) * d] for k in range(3))
        last = layer == depth - 1
        j = layer // 2
        g = norm_g[layer][None, :]
        if layer % 2 == 0:
            x_all = even_layer(x_all, mod, g, ev_w_in, ev_w_out, j, ev_q_g[j], ev_k_g[j], ev_sgu_g[j],
                               ev_ws[j], ev_bs[j], cos_t, sin_t, ctx_len, last)
        else:
            x_all = odd_layer(x_all, mod, g, od_w_in, od_w_out, j, s5_ops,
                              s5_d[j], s5_glu_w[j], s5_glu_b[j],
                              lru_conv_w[j], lru_conv_b[j], lru_lam[j], lru_wa[j], lru_ba[j], lru_wx[j],
                              lru_bx[j], ctx_len, last)
    return x_all
```
